```python
import functools
import jax, jax.numpy as jnp
from jax import lax
import numpy as np

D_MODEL = 1024
BATCH = 4
SEQ = 4096
DEPTH = 4
DEC_BATCH = 128
DEC_SEQ = 1
PAST_LEN = 8192
PAGE_SIZE = 128

N_META = 16
NH_M = 4
DV_M = D_MODEL // NH_M
DQK_M = DV_M // 2
CHUNK = 128
N_HEADS = 8
N_KV = 2
GQ = N_HEADS // N_KV
HD = D_MODEL // N_HEADS
WINDOW = 128
BLK = WINDOW
ROT_DIM = HD // 4
ROPE_THETA = 500000.0
D_FF = 4 * D_MODEL
EPS = 1e-6
IN_WIDTHS = (NH_M * DQK_M, NH_M * DQK_M, NH_M * DV_M, NH_M, NH_M, NH_M * DV_M,
             N_HEADS * HD, N_KV * HD, N_KV * HD, D_MODEL, D_MODEL)
IN_COLS = sum(IN_WIDTHS)

kernel_name = 'hybrid_mlstm_swa_sink_decoder_step'


def rmsnorm(x, g):
    xf = x.astype(jnp.float32)
    y = xf * lax.rsqrt(jnp.mean(xf * xf, axis=-1, keepdims=True) + EPS)
    return (y * g.astype(jnp.float32)).astype(x.dtype)


def rope_partial(x, pos):
    half = ROT_DIM // 2
    inv = jnp.float32(ROPE_THETA) ** (-jnp.arange(half, dtype=jnp.float32) * 2.0 / ROT_DIM)
    ang = pos.astype(jnp.float32)[:, None] * inv[None, :]
    cos = jnp.cos(ang)[:, None, :]
    sin = jnp.sin(ang)[:, None, :]
    xf = x.astype(jnp.float32)
    x1 = xf[..., :half]
    x2 = xf[..., half:ROT_DIM]
    out = jnp.concatenate([x1 * cos - x2 * sin, x2 * cos + x1 * sin, xf[..., ROT_DIM:]], axis=-1)
    return out.astype(x.dtype)


def sink_softmax(logits, sink):
    mx = jnp.maximum(jnp.max(logits, axis=-1, keepdims=True), sink)
    p = jnp.exp(logits - mx)
    return p / (jnp.sum(p, axis=-1, keepdims=True) + jnp.exp(sink - mx))


def mlstm_chunk(carry, xs):
    C, n, m = carry
    q, k, v, ig, lf = xs
    L = q.shape[2]
    b = jnp.cumsum(lf, axis=-1)
    a = b + m[..., None]
    D = b[..., :, None] - b[..., None, :] + ig[..., None, :]
    causal = jnp.tril(jnp.ones((L, L), dtype=bool))
    D = jnp.where(causal, D, -jnp.inf)
    m_t = jnp.maximum(a, jnp.max(D, axis=-1))
    w_inter = jnp.exp(a - m_t)
    W = jnp.exp(D - m_t[..., None])
    S = jnp.einsum('bhtd,bhsd->bhts', q, k) * W
    num = w_inter[..., None] * jnp.einsum('bhvd,bhtd->bhtv', C, q) + jnp.einsum('bhts,bhsv->bhtv', S, v)
    den = w_inter * jnp.einsum('bhd,bhtd->bht', n, q) + jnp.sum(S, axis=-1)
    h = num / jnp.maximum(jnp.abs(den), jnp.exp(-m_t))[..., None]
    m_new = m_t[..., -1]
    g = jnp.exp(b[..., -1:] - b + ig - m_new[..., None])
    decay = w_inter[..., -1]
    C_new = decay[..., None, None] * C + jnp.einsum('bhs,bhsv,bhsd->bhvd', g, v, k)
    n_new = decay[..., None] * n + jnp.einsum('bhs,bhsd->bhd', g, k)
    return (C_new, n_new, m_new), h


def mlstm_prompt(q, k, v, ig, lf):
    B, H, T, _ = q.shape
    S = T - N_META
    nc = S // CHUNK
    carry = (jnp.zeros((B, H, DV_M, DQK_M), jnp.float32),
             jnp.zeros((B, H, DQK_M), jnp.float32),
             jnp.zeros((B, H), jnp.float32))
    seqs = (q, k, v, ig, lf)
    carry, h_meta = mlstm_chunk(carry, tuple(a[:, :, :N_META] for a in seqs))

    def to_chunks(a):
        a = a[:, :, N_META:]
        return jnp.moveaxis(a.reshape(a.shape[:2] + (nc, CHUNK) + a.shape[3:]), 2, 0)

    carry, h_r = lax.scan(mlstm_chunk, carry, tuple(to_chunks(a) for a in seqs))
    h_r = jnp.moveaxis(h_r, 0, 2).reshape(B, H, S, DV_M)
    return jnp.concatenate([h_meta, h_r], axis=2), carry


def mlstm_sample(q, k, v, ig, lf, C0, n0, m0):
    carry = (C0.astype(jnp.float32), n0.astype(jnp.float32), m0.astype(jnp.float32))
    carry, h = mlstm_chunk(carry, (q, k, v, ig, lf))
    return h, carry


def swa_prompt(q, k, v, sinks):
    B, T = q.shape[:2]
    S = T - N_META
    nb = S // BLK
    f32 = jnp.float32
    qf = q.astype(f32) * (HD ** -0.5)
    kf = k.astype(f32)
    vf = v.astype(f32)
    sink = sinks.astype(f32).reshape(N_KV, GQ)
    km, vm = kf[:, :N_META], vf[:, :N_META]
    qm = qf[:, :N_META].reshape(B, N_META, N_KV, GQ, HD)
    s_mm = jnp.einsum('bqkgd,bskd->bkgqs', qm, km)
    s_mm = jnp.where(jnp.asarray(np.tril(np.ones((N_META, N_META), dtype=bool))), s_mm, -jnp.inf)
    p_mm = sink_softmax(s_mm, sink[None, :, :, None, None])
    o_meta = jnp.einsum('bkgqs,bskd->bqkgd', p_mm, vm).reshape(B, N_META, N_HEADS * HD)
    qb = qf[:, N_META:].reshape(B, nb, BLK, N_KV, GQ, HD)
    kb = kf[:, N_META:].reshape(B, nb, BLK, N_KV, HD)
    vb = vf[:, N_META:].reshape(B, nb, BLK, N_KV, HD)
    pad = jnp.zeros_like(kb[:, :1])
    kband = jnp.concatenate([jnp.concatenate([pad, kb[:, :-1]], axis=1), kb], axis=2)
    vband = jnp.concatenate([jnp.concatenate([pad, vb[:, :-1]], axis=1), vb], axis=2)
    i = np.arange(BLK)[:, None]
    j = np.arange(2 * BLK)[None, :]
    dist = i + BLK - j
    ok = (dist >= 0) & (dist < WINDOW)
    first = ok & (j >= BLK)
    band_mask = np.where((np.arange(nb) > 0)[:, None, None], ok[None], first[None])
    s_meta = jnp.einsum('bnqkgd,bskd->bnkgqs', qb, km)
    s_band = jnp.einsum('bnqkgd,bnskd->bnkgqs', qb, kband)
    s_band = jnp.where(jnp.asarray(band_mask)[None, :, None, None], s_band, -jnp.inf)
    p = sink_softmax(jnp.concatenate([s_meta, s_band], axis=-1), sink[None, None, :, :, None, None])
    o = (jnp.einsum('bnkgqs,bskd->bnqkgd', p[..., :N_META], vm)
         + jnp.einsum('bnkgqs,bnskd->bnqkgd', p[..., N_META:], vband))
    o_real = o.reshape(B, S, N_HEADS * HD)
    W = min(WINDOW, T)
    caches = (k[:, :N_META], v[:, :N_META], k[:, T - W:], v[:, T - W:])
    return jnp.concatenate([o_meta, o_real], axis=1), caches


def swa_sample(q, k, v, sinks, meta_k, meta_v, win_k, win_v):
    Bd, Ls = q.shape[:2]
    W = win_k.shape[1]
    f32 = jnp.float32
    qpos = PAST_LEN + np.arange(Ls)
    kpos = np.concatenate([np.arange(N_META), PAST_LEN - W + np.arange(W), PAST_LEN + np.arange(Ls)])
    is_meta = np.arange(kpos.size) < N_META
    mask = (kpos[None] <= qpos[:, None]) & (is_meta[None] | (((qpos[:, None] - kpos[None]) < WINDOW) & (kpos[None] >= N_META)))
    keys = jnp.concatenate([meta_k.astype(f32), win_k.astype(f32), k.astype(f32)], axis=1)
    vals = jnp.concatenate([meta_v.astype(f32), win_v.astype(f32), v.astype(f32)], axis=1)
    sink = sinks.astype(f32).reshape(N_KV, GQ)
    qg = q.astype(f32).reshape(Bd, Ls, N_KV, GQ, HD) * (HD ** -0.5)
    s = jnp.einsum('bqkgd,bskd->bkgqs', qg, keys)
    s = jnp.where(jnp.asarray(mask)[None, None, None], s, -jnp.inf)
    p = sink_softmax(s, sink[None, :, :, None, None])
    o = jnp.einsum('bkgqs,bskd->bqkgd', p, vals).reshape(Bd, Ls, N_HEADS * HD)
    new_k = jnp.concatenate([win_k, k.astype(win_k.dtype)], axis=1)[:, -W:]
    new_v = jnp.concatenate([win_v, v.astype(win_v.dtype)], axis=1)[:, -W:]
    return o, (new_k, new_v)


def split_points():
    pts, acc = [], 0
    for w in IN_WIDTHS[:-1]:
        acc += w
        pts.append(acc)
    return pts


def decoder_layer(x, pos, norm_mix_l, w_in_l, b_i_l, b_f_l, norm_mh_l, w_out_l, norm_mlp_l, w_up_l, w_down_l, mlstm_fn, attn_fn):
    B, T, _ = x.shape
    f32 = jnp.float32
    h = rmsnorm(x, norm_mix_l)
    proj = h @ w_in_l
    q_m, k_m, v_m, ig, fg, o_m, q_a, k_a, v_a, g_m, g_a = jnp.split(proj, split_points(), axis=-1)

    def heads_first(a, d):
        return jnp.transpose(a.astype(f32).reshape(B, T, NH_M, d), (0, 2, 1, 3))

    qm = heads_first(q_m, DQK_M) * (DQK_M ** -0.5)
    km = heads_first(k_m, DQK_M)
    vm = heads_first(v_m, DV_M)
    igm = jnp.transpose(ig.astype(f32), (0, 2, 1)) + b_i_l.astype(f32)[None, :, None]
    lfm = jax.nn.log_sigmoid(jnp.transpose(fg.astype(f32), (0, 2, 1)) + b_f_l.astype(f32)[None, :, None])
    h_t, m_state = mlstm_fn(qm, km, vm, igm, lfm)
    h_t = h_t * lax.rsqrt(jnp.mean(h_t * h_t, axis=-1, keepdims=True) + EPS)
    h_t = jnp.transpose(h_t, (0, 2, 1, 3)).reshape(B, T, NH_M * DV_M) * norm_mh_l.astype(f32)
    y_m = jax.nn.sigmoid(o_m.astype(f32)) * h_t
    qa = rope_partial(q_a.reshape(B, T, N_HEADS, HD), pos)
    ka = rope_partial(k_a.reshape(B, T, N_KV, HD), pos)
    va = v_a.reshape(B, T, N_KV, HD)
    y_a, a_cache = attn_fn(qa, ka, va)
    y = jax.nn.sigmoid(g_m.astype(f32)) * y_m + jax.nn.sigmoid(g_a.astype(f32)) * y_a
    x = x + y.astype(x.dtype) @ w_out_l
    hf = rmsnorm(x, norm_mlp_l)
    x = x + jnp.square(jax.nn.relu(hf @ w_up_l)) @ w_down_l
    return x, m_state, a_cache


def setup_inputs(seed: int = 0) -> dict:
    key = jax.random.key(seed)
    ks = jax.random.split(key, 24)
    f32 = jnp.float32
    W = min(WINDOW, PAST_LEN)
    nrm = lambda kk, shape: jax.random.normal(kk, shape, f32)
    return {
        'x_prompt': nrm(ks[0], (BATCH, SEQ, D_MODEL)),
        'x_sample': nrm(ks[1], (DEC_BATCH, DEC_SEQ, D_MODEL)),
        'state_C': 0.1 * nrm(ks[2], (DEPTH, DEC_BATCH, NH_M, DV_M, DQK_M)),
        'state_n': 0.5 * nrm(ks[3], (DEPTH, DEC_BATCH, NH_M, DQK_M)),
        'state_m': 0.5 * nrm(ks[4], (DEPTH, DEC_BATCH, NH_M)),
        'cache_meta_k': nrm(ks[5], (DEPTH, DEC_BATCH, N_META, N_KV, HD)),
        'cache_meta_v': nrm(ks[6], (DEPTH, DEC_BATCH, N_META, N_KV, HD)),
        'cache_win_k': nrm(ks[7], (DEPTH, DEC_BATCH, W, N_KV, HD)),
        'cache_win_v': nrm(ks[8], (DEPTH, DEC_BATCH, W, N_KV, HD)),
        'meta_tokens': nrm(ks[9], (N_META, D_MODEL)),
        'norm_mix': 1.0 + 0.02 * nrm(ks[10], (DEPTH, D_MODEL)),
        'w_in': nrm(ks[11], (DEPTH, D_MODEL, IN_COLS)) * D_MODEL ** -0.5,
        'b_igate': 0.1 * nrm(ks[12], (DEPTH, NH_M)),
        'b_fgate': 3.0 + 0.5 * nrm(ks[13], (DEPTH, NH_M)),
        'norm_mh': 1.0 + 0.02 * nrm(ks[14], (DEPTH, NH_M * DV_M)),
        'attn_sinks': 0.5 * nrm(ks[15], (DEPTH, N_HEADS)),
        'w_out': nrm(ks[16], (DEPTH, D_MODEL, D_MODEL)) * D_MODEL ** -0.5,
        'norm_mlp': 1.0 + 0.02 * nrm(ks[17], (DEPTH, D_MODEL)),
        'w_up': nrm(ks[18], (DEPTH, D_MODEL, D_FF)) * D_MODEL ** -0.5,
        'w_down': nrm(ks[19], (DEPTH, D_FF, D_MODEL)) * D_FF ** -0.5,
        'norm_final': 1.0 + 0.02 * nrm(ks[20], (D_MODEL,)),
    }


def reference(x_prompt, x_sample, state_C, state_n, state_m, cache_meta_k, cache_meta_v, cache_win_k, cache_win_v,
              meta_tokens, norm_mix, w_in, b_igate, b_fgate, norm_mh, attn_sinks, w_out, norm_mlp, w_up, w_down, norm_final):
    B, S, _ = x_prompt.shape
    meta = jnp.broadcast_to(meta_tokens.astype(x_prompt.dtype)[None], (B, N_META, D_MODEL))
    xp = jnp.concatenate([meta, x_prompt], axis=1)
    pos_p = jnp.arange(S + N_META, dtype=jnp.int32)
    xs = x_sample
    pos_s = PAST_LEN + jnp.arange(xs.shape[1], dtype=jnp.int32)
    pC, pn, pm, pmk, pmv, pwk, pwv = [], [], [], [], [], [], []
    sC, sn, sm, swk, swv = [], [], [], [], []
    for l in range(DEPTH):
        lw = (norm_mix[l], w_in[l], b_igate[l], b_fgate[l], norm_mh[l], w_out[l], norm_mlp[l], w_up[l], w_down[l])
        xp, (c_, n_, m_), (mk_, mv_, wk_, wv_) = decoder_layer(
            xp, pos_p, *lw, mlstm_fn=mlstm_prompt,
            attn_fn=functools.partial(swa_prompt, sinks=attn_sinks[l]))
        pC.append(c_.astype(state_C.dtype))
        pn.append(n_.astype(state_n.dtype))
        pm.append(m_.astype(state_m.dtype))
        pmk.append(mk_)
        pmv.append(mv_)
        pwk.append(wk_)
        pwv.append(wv_)
        xs, (c2, n2, m2), (wk2, wv2) = decoder_layer(
            xs, pos_s, *lw,
            mlstm_fn=functools.partial(mlstm_sample, C0=state_C[l], n0=state_n[l], m0=state_m[l]),
            attn_fn=functools.partial(swa_sample, sinks=attn_sinks[l], meta_k=cache_meta_k[l], meta_v=cache_meta_v[l],
                                      win_k=cache_win_k[l], win_v=cache_win_v[l]))
        sC.append(c2.astype(state_C.dtype))
        sn.append(n2.astype(state_n.dtype))
        sm.append(m2.astype(state_m.dtype))
        swk.append(wk2)
        swv.append(wv2)
    y_prompt = rmsnorm(xp[:, N_META:], norm_final)
    y_sample = rmsnorm(xs, norm_final)
    return (y_prompt, y_sample,
            jnp.stack(pC), jnp.stack(pn), jnp.stack(pm), jnp.stack(pmk), jnp.stack(pmv), jnp.stack(pwk), jnp.stack(pwv),
            jnp.stack(sC), jnp.stack(sn), jnp.stack(sm), jnp.stack(swk), jnp.stack(swv))
```

```python
import functools

import jax
import jax.numpy as jnp
import numpy as np
from jax import lax
from jax.experimental import pallas as pl
from jax.experimental.pallas import tpu as pltpu

F32 = jnp.float32
BF16 = jnp.bfloat16

D_MODEL = 1024
DEPTH = 4
PAST_LEN = 8192
N_META = 16
NH_M = 4
DV_M = 256
DQK_M = 128
CHUNK = 128
N_HEADS = 8
N_KV = 2
GQ = N_HEADS // N_KV
HD = 128
WINDOW = 128
ROT_DIM = 32
ROPE_THETA = 500000.0
D_FF = 4 * D_MODEL
EPS = 1e-6
IN_WIDTHS = (NH_M * DQK_M, NH_M * DQK_M, NH_M * DV_M, NH_M, NH_M, NH_M * DV_M,
             N_HEADS * HD, N_KV * HD, N_KV * HD, D_MODEL, D_MODEL)

P_QM, P_KA, P_VA, P_QA, P_VM, P_OM, P_GM, P_GA = 0, 512, 768, 1024, 2048, 3072, 4096, 5120
P_TOT = 6144
PROJ_TN = 1024
SMALL_ROWS = 256
META_ROW0 = 128
DEC_BLK = 8
VMEM_LIMIT = 56 * 1024 * 1024


def _cparams(n_axes):
    return pltpu.CompilerParams(dimension_semantics=("arbitrary",) * n_axes, vmem_limit_bytes=VMEM_LIMIT)


def _rms(xf, gain):
    ms = jnp.mean(xf * xf, axis=-1, keepdims=True)
    return xf * lax.rsqrt(ms + EPS) * gain


def _dot(a, b):
    return jnp.dot(a, b, preferred_element_type=F32)


def _dot_nt(a, b):
    return lax.dot_general(a, b, (((1,), (1,)), ((), ())), preferred_element_type=F32)


def _log_sigmoid(x):
    return jnp.minimum(x, 0.0) - jnp.log1p(jnp.exp(-jnp.abs(x)))


def _inproj_kernel(x_ref, gain_ref, w_ref, cs_ref, s1_ref, s2_ref, o_ref, h_ref):
    j = pl.program_id(1)

    @pl.when(j == 0)
    def _():
        h_ref[...] = _rms(x_ref[...], gain_ref[...]).astype(BF16)

    acc = _dot(h_ref[...], w_ref[...])

    def rope(seg):
        return (seg * cs_ref[...] + pltpu.roll(seg, 16, 1) * s1_ref[...]
                + pltpu.roll(seg, HD - 16, 1) * s2_ref[...])

    @pl.when(j == 0)
    def _():
        o_ref[:, P_QM:P_KA] = (acc[:, P_QM:P_KA] * (DQK_M ** -0.5)).astype(BF16)
        for c in range(N_KV):
            lo = P_KA + c * HD
            o_ref[:, lo:lo + HD] = rope(acc[:, lo:lo + HD]).astype(BF16)
        o_ref[:, P_VA:P_QA] = acc[:, P_VA:P_QA].astype(BF16)

    @pl.when(j == 1)
    def _():
        for c in range(N_HEADS):
            lo = c * HD
            o_ref[:, lo:lo + HD] = (rope(acc[:, lo:lo + HD]) * (HD ** -0.5)).astype(BF16)

    @pl.when(j >= 2)
    def _():
        o_ref[...] = acc.astype(BF16)


def _inproj(x, gain, w_nn, tabs, tm):
    rows = x.shape[0]
    tab_blocks = tabs[0].shape[0] // tm
    tab_spec = pl.BlockSpec((tm, HD), lambda i, j: (i % tab_blocks, 0))
    return pl.pallas_call(
        _inproj_kernel,
        grid=(rows // tm, P_TOT // PROJ_TN),
        in_specs=[pl.BlockSpec((tm, D_MODEL), lambda i, j: (i, 0)),
                  pl.BlockSpec((1, D_MODEL), lambda i, j: (0, 0)),
                  pl.BlockSpec((D_MODEL, PROJ_TN), lambda i, j: (0, j)),
                  tab_spec, tab_spec, tab_spec],
        out_specs=pl.BlockSpec((tm, PROJ_TN), lambda i, j: (i, j)),
        out_shape=jax.ShapeDtypeStruct((rows, P_TOT), BF16),
        scratch_shapes=[pltpu.VMEM((tm, D_MODEL), BF16)],
        compiler_params=_cparams(2),
        name="inproj",
    )(x, gain, w_nn, *tabs)


def _inproj_t_kernel(x_ref, gain_ref, wk_ref, wg_ref, kt_ref, gt_ref):
    h = _rms(x_ref[...], gain_ref[...]).astype(BF16)
    kt_ref[...] = _dot_nt(wk_ref[...], h).astype(BF16)
    gt_ref[...] = _dot_nt(wg_ref[...], h)[0:2 * NH_M]


def _inproj_t(x, gain, w_kt, w_gt, tm):
    rows = x.shape[0]
    return pl.pallas_call(
        _inproj_t_kernel,
        grid=(rows // tm,),
        in_specs=[pl.BlockSpec((tm, D_MODEL), lambda i: (i, 0)),
                  pl.BlockSpec((1, D_MODEL), lambda i: (0, 0)),
                  pl.BlockSpec((NH_M * DQK_M, D_MODEL), lambda i: (0, 0)),
                  pl.BlockSpec((16, D_MODEL), lambda i: (0, 0))],
        out_specs=[pl.BlockSpec((NH_M * DQK_M, tm), lambda i: (0, i)),
                   pl.BlockSpec((2 * NH_M, tm), lambda i: (0, i))],
        out_shape=[jax.ShapeDtypeStruct((NH_M * DQK_M, rows), BF16),
                   jax.ShapeDtypeStruct((2 * NH_M, rows), F32)],
        compiler_params=_cparams(1),
        name="inproj_t",
    )(x, gain, w_kt, w_gt)


def _mlstm_kernel(q_ref, kt_ref, v_ref, o_ref, gm_ref, g_ref, bias_ref, nmh_ref, ct0_ref, n0_ref, m0_ref,
                  y_ref, ctf_ref, nf_ref, mf_ref, ct_s, n_s, m_s, *, nc, valid_len):
    c = pl.program_id(1)
    L = CHUNK

    @pl.when(c == 0)
    def _():
        ct_s[...] = ct0_ref[...]
        n_s[...] = n0_ref[...]
        m_s[...] = m0_ref[...]

    row = lax.broadcasted_iota(jnp.int32, (L, L), 0)
    col = lax.broadcasted_iota(jnp.int32, (L, L), 1)
    tril = col <= row
    eye = col == row

    gates = g_ref[...] + bias_ref[...]
    ig_all = gates[0:NH_M]
    lf_all = _log_sigmoid(gates[NH_M:2 * NH_M])
    if valid_len < L:
        ok = lax.broadcasted_iota(jnp.int32, (NH_M, L), 1) < valid_len
        ig_all = jnp.where(ok, ig_all, -jnp.inf)
        lf_all = jnp.where(ok, lf_all, 0.0)

    for h in range(NH_M):
        ig = ig_all[h:h + 1]
        lf = lf_all[h:h + 1]
        m_prev = m_s[h, 0:1, 0:1]
        b_col = jnp.sum(jnp.where(tril, lf, 0.0), axis=1, keepdims=True)
        b_row = jnp.sum(jnp.where(eye, b_col, 0.0), axis=0, keepdims=True)
        a_col = b_col + m_prev
        dmat = jnp.where(tril, b_col - b_row + ig, -jnp.inf)
        m_col = jnp.maximum(a_col, jnp.max(dmat, axis=1, keepdims=True))
        w_inter = jnp.exp(a_col - m_col)
        wmat = jnp.exp(dmat - m_col)

        q = q_ref[:, h * DQK_M:(h + 1) * DQK_M]
        kt = kt_ref[h * DQK_M:(h + 1) * DQK_M, :]
        v = v_ref[:, h * DV_M:(h + 1) * DV_M]
        ct = ct_s[h]
        nm = n_s[h]

        s = _dot(q, kt) * wmat
        num = w_inter * _dot(q, ct.astype(BF16)) + _dot(s.astype(BF16), v)
        qn = _dot(q, nm.astype(BF16))[:, 0:1]
        den = w_inter * qn + jnp.sum(s, axis=1, keepdims=True)
        hh = num / jnp.maximum(jnp.abs(den), jnp.exp(-m_col))

        m_new = m_col[L - 1:L]
        decay = w_inter[L - 1:L]
        g_row = jnp.exp(b_col[L - 1:L] - b_row + ig - m_new)
        ktg = kt.astype(F32) * g_row
        ct_s[h] = decay * ct + _dot(ktg.astype(BF16), v)
        n_s[h] = decay * nm + jnp.sum(ktg, axis=1, keepdims=True)
        m_s[h] = jnp.broadcast_to(m_new, (8, 128))

        hs = slice(h * DV_M, (h + 1) * DV_M)
        hn = hh * lax.rsqrt(jnp.mean(hh * hh, axis=1, keepdims=True) + EPS) * nmh_ref[:, hs]
        y = jax.nn.sigmoid(o_ref[:, hs].astype(F32)) * hn
        y_ref[:, hs] = (jax.nn.sigmoid(gm_ref[:, hs].astype(F32)) * y).astype(BF16)

    @pl.when(c == nc - 1)
    def _():
        ctf_ref[0] = ct_s[...]
        nf_ref[0] = n_s[...]
        mf_ref[0] = m_s[...]


def _mlstm(pk, kt, gt, bias8, nmh, ct0, n0, m0, nb, nc, row0, valid_len=CHUNK):
    rows = pk.shape[0]
    r0 = row0 // CHUNK
    rb = lambda b, c: r0 + b * nc + c
    full3 = lambda b, c: (0, 0, 0)
    st4 = lambda b, c: (b, 0, 0, 0)
    return pl.pallas_call(
        functools.partial(_mlstm_kernel, nc=nc, valid_len=valid_len),
        grid=(nb, nc),
        in_specs=[pl.BlockSpec((CHUNK, NH_M * DQK_M), lambda b, c: (rb(b, c), P_QM // (NH_M * DQK_M))),
                  pl.BlockSpec((NH_M * DQK_M, CHUNK), lambda b, c: (0, rb(b, c))),
                  pl.BlockSpec((CHUNK, D_MODEL), lambda b, c: (rb(b, c), P_VM // D_MODEL)),
                  pl.BlockSpec((CHUNK, D_MODEL), lambda b, c: (rb(b, c), P_OM // D_MODEL)),
                  pl.BlockSpec((CHUNK, D_MODEL), lambda b, c: (rb(b, c), P_GM // D_MODEL)),
                  pl.BlockSpec((2 * NH_M, CHUNK), lambda b, c: (0, rb(b, c))),
                  pl.BlockSpec((2 * NH_M, CHUNK), lambda b, c: (0, 0)),
                  pl.BlockSpec((1, D_MODEL), lambda b, c: (0, 0)),
                  pl.BlockSpec((NH_M, DQK_M, DV_M), full3),
                  pl.BlockSpec((NH_M, DQK_M, 128), full3),
                  pl.BlockSpec((NH_M, 8, 128), full3)],
        out_specs=[pl.BlockSpec((CHUNK, D_MODEL), lambda b, c: (b * nc + c, 0)),
                   pl.BlockSpec((1, NH_M, DQK_M, DV_M), st4),
                   pl.BlockSpec((1, NH_M, DQK_M, 128), st4),
                   pl.BlockSpec((1, NH_M, 8, 128), st4)],
        out_shape=[jax.ShapeDtypeStruct((nb * nc * CHUNK, D_MODEL), BF16),
                   jax.ShapeDtypeStruct((nb, NH_M, DQK_M, DV_M), F32),
                   jax.ShapeDtypeStruct((nb, NH_M, DQK_M, 128), F32),
                   jax.ShapeDtypeStruct((nb, NH_M, 8, 128), F32)],
        scratch_shapes=[pltpu.VMEM((NH_M, DQK_M, DV_M), F32),
                        pltpu.VMEM((NH_M, DQK_M, 128), F32),
                        pltpu.VMEM((NH_M, 8, 128), F32)],
        compiler_params=_cparams(2),
        name="mlstm",
    )(pk, kt, pk, pk, pk, gt, bias8, nmh, ct0, n0, m0)


def _swa_kernel(sink_ref, q_ref, kc_ref, kp_ref, vc_ref, vp_ref, mk_ref, mv_ref, ga_ref, y_ref, *, has_meta):
    n = pl.program_id(1)
    L = WINDOW
    row = lax.broadcasted_iota(jnp.int32, (L, L), 0)
    col = lax.broadcasted_iota(jnp.int32, (L, L), 1)
    tril = col <= row
    prev_ok = jnp.full((L, L), n, jnp.int32) > 0
    for kv in range(N_KV):
        ks = slice(kv * HD, (kv + 1) * HD)
        kc, kp, vc, vp = kc_ref[:, ks], kp_ref[:, ks], vc_ref[:, ks], vp_ref[:, ks]
        for g in range(GQ):
            hd = kv * GQ + g
            hs = slice(hd * HD, (hd + 1) * HD)
            q = q_ref[:, hs]
            sink = sink_ref[hd]
            band = jnp.where(tril, _dot_nt(q, kc), jnp.where(prev_ok, _dot_nt(q, kp), -jnp.inf))
            mx = jnp.maximum(jnp.max(band, axis=1, keepdims=True), sink)
            if has_meta:
                s_m = _dot_nt(q, mk_ref[:, ks])
                mx = jnp.maximum(mx, jnp.max(s_m, axis=1, keepdims=True))
            p = jnp.exp(band - mx)
            denom = jnp.sum(p, axis=1, keepdims=True) + jnp.exp(sink - mx)
            o = (_dot(jnp.where(tril, p, 0.0).astype(BF16), vc)
                 + _dot(jnp.where(tril, 0.0, p).astype(BF16), vp))
            if has_meta:
                p_m = jnp.exp(s_m - mx)
                denom = denom + jnp.sum(p_m, axis=1, keepdims=True)
                o = o + _dot(p_m.astype(BF16), mv_ref[:, ks])
            y = (o / denom) * jax.nn.sigmoid(ga_ref[:, hs].astype(F32))
            y_ref[:, hs] = y.astype(BF16)


def _swa(pk, sinks, mk, mv, nb, nblk, row0, has_meta):
    r0 = row0 // WINDOW
    rb = lambda b, n: r0 + b * nblk + n
    rp = lambda b, n: r0 + b * nblk + jnp.maximum(n - 1, 0)
    kw = N_KV * HD
    return pl.pallas_call(
        functools.partial(_swa_kernel, has_meta=has_meta),
        grid=(nb, nblk),
        in_specs=[pl.BlockSpec(memory_space=pltpu.SMEM),
                  pl.BlockSpec((WINDOW, D_MODEL), lambda b, n: (rb(b, n), P_QA // D_MODEL)),
                  pl.BlockSpec((WINDOW, kw), lambda b, n: (rb(b, n), P_KA // kw)),
                  pl.BlockSpec((WINDOW, kw), lambda b, n: (rp(b, n), P_KA // kw)),
                  pl.BlockSpec((WINDOW, kw), lambda b, n: (rb(b, n), P_VA // kw)),
                  pl.BlockSpec((WINDOW, kw), lambda b, n: (rp(b, n), P_VA // kw)),
                  pl.BlockSpec((N_META, kw), lambda b, n: (0, 0)),
                  pl.BlockSpec((N_META, kw), lambda b, n: (0, 0)),
                  pl.BlockSpec((WINDOW, D_MODEL), lambda b, n: (rb(b, n), P_GA // D_MODEL))],
        out_specs=pl.BlockSpec((WINDOW, D_MODEL), lambda b, n: (b * nblk + n, 0)),
        out_shape=jax.ShapeDtypeStruct((nb * nblk * WINDOW, D_MODEL), BF16),
        compiler_params=_cparams(2),
        name="swa",
    )(sinks, pk, pk, pk, pk, pk, mk, mv, pk)


def _tail_kernel(ym_ref, ya_ref, x_ref, wo_ref, gn_ref, wu_ref, wd_ref, gf_ref, o_ref, hf_ref, *, nj, final_norm):
    j = pl.program_id(1)

    @pl.when(j == 0)
    def _():
        y = (ym_ref[...].astype(F32) + ya_ref[...].astype(F32)).astype(BF16)
        x1 = x_ref[...] + _dot(y, wo_ref[...])
        o_ref[...] = x1
        hf_ref[...] = _rms(x1, gn_ref[...]).astype(BF16)

    a = jnp.square(jnp.maximum(_dot(hf_ref[...], wu_ref[...]), 0.0)).astype(BF16)
    o_ref[...] += _dot(a, wd_ref[...])

    if final_norm:
        @pl.when(j == nj - 1)
        def _():
            o_ref[...] = _rms(o_ref[...], gf_ref[...])


def _tail(ym, ya, x, w_out, g_mlp, w_up, w_down, g_final, tm, tf, final_norm):
    rows = x.shape[0]
    nj = D_FF // tf
    return pl.pallas_call(
        functools.partial(_tail_kernel, nj=nj, final_norm=final_norm),
        grid=(rows // tm, nj),
        in_specs=[pl.BlockSpec((tm, D_MODEL), lambda i, j: (i, 0)),
                  pl.BlockSpec((tm, D_MODEL), lambda i, j: (i, 0)),
                  pl.BlockSpec((tm, D_MODEL), lambda i, j: (i, 0)),
                  pl.BlockSpec((D_MODEL, D_MODEL), lambda i, j: (0, 0)),
                  pl.BlockSpec((1, D_MODEL), lambda i, j: (0, 0)),
                  pl.BlockSpec((D_MODEL, tf), lambda i, j: (0, j)),
                  pl.BlockSpec((tf, D_MODEL), lambda i, j: (j, 0)),
                  pl.BlockSpec((1, D_MODEL), lambda i, j: (0, 0))],
        out_specs=pl.BlockSpec((tm, D_MODEL), lambda i, j: (i, 0)),
        out_shape=jax.ShapeDtypeStruct((rows, D_MODEL), F32),
        scratch_shapes=[pltpu.VMEM((tm, D_MODEL), BF16)],
        compiler_params=_cparams(2),
        name="tail",
    )(ym, ya, x, w_out, g_mlp, w_up, w_down, g_final)


def _mlstm_step_kernel(q_ref, k_ref, v_ref, o_ref, gm_ref, ig_ref, fg_ref, m_ref, n_ref, c_ref,
                       kall_ref, vt_ref, bi_ref, bf_ref, nmh_ref,
                       y_ref, co_ref, no_ref, mo_ref):
    i = pl.program_id(0)
    nb = DEC_BLK
    row8 = lax.broadcasted_iota(jnp.int32, (nb, DV_M), 0)
    rowb = lax.broadcasted_iota(jnp.int32, (kall_ref.shape[0], DQK_M), 0)
    for h in range(NH_M):
        ds = slice(h * DQK_M, (h + 1) * DQK_M)
        vs = slice(h * DV_M, (h + 1) * DV_M)
        q, k, v, nrm = q_ref[:, ds], k_ref[:, ds], v_ref[:, vs], n_ref[:, ds]
        ig = ig_ref[:, ds] + bi_ref[:, ds]
        lf = _log_sigmoid(fg_ref[:, ds] + bf_ref[:, ds])
        m_prev = m_ref[:, ds]
        m_t = jnp.maximum(lf + m_prev, ig)
        w_inter = jnp.exp(lf + m_prev - m_t)
        w_in = jnp.exp(ig - m_t)
        s = jnp.sum(q * k, axis=1, keepdims=True) * w_in
        qb = q.astype(BF16)
        kall = kall_ref[:, ds]
        vt = vt_ref[vs, :]
        cq = jnp.zeros((nb, DV_M), F32)
        for bb in range(nb):
            c_old = c_ref[bb, h]
            cq = jnp.where(row8 == bb, _dot_nt(qb, c_old.astype(BF16)), cq)
            outer = _dot(vt, jnp.where(rowb == i * nb + bb, kall, jnp.zeros_like(kall)))
            co_ref[bb, h] = w_inter[bb:bb + 1, :] * c_old + w_in[bb:bb + 1, :] * outer
        num = w_inter[:, 0:1] * cq + s[:, 0:1] * v
        den = w_inter * jnp.sum(nrm * q, axis=1, keepdims=True) + s
        hh = num / jnp.maximum(jnp.abs(den), jnp.exp(-m_t))[:, 0:1]
        no_ref[:, ds] = w_inter * nrm + w_in * k
        mo_ref[:, ds] = m_t
        hn = hh * lax.rsqrt(jnp.mean(hh * hh, axis=1, keepdims=True) + EPS) * nmh_ref[:, vs]
        y_ref[:, vs] = jax.nn.sigmoid(gm_ref[:, vs]) * (jax.nn.sigmoid(o_ref[:, vs]) * hn)


def _mlstm_step(q, k, v, o, gm, ig, fg, m, n, c, kall, vt, bi, bf, nmh):
    nbatch = q.shape[0]
    qd = NH_M * DQK_M
    rowq = pl.BlockSpec((DEC_BLK, qd), lambda i: (i, 0))
    rowv = pl.BlockSpec((DEC_BLK, D_MODEL), lambda i: (i, 0))
    cspec = pl.BlockSpec((DEC_BLK, NH_M, DV_M, DQK_M), lambda i: (i, 0, 0, 0))
    const = lambda shape: pl.BlockSpec(shape, lambda i: (0, 0))
    return pl.pallas_call(
        _mlstm_step_kernel,
        grid=(nbatch // DEC_BLK,),
        in_specs=[rowq, rowq, rowv, rowv, rowv, rowq, rowq, rowq, rowq, cspec,
                  const((nbatch, qd)), const((D_MODEL, nbatch)), const((1, qd)), const((1, qd)),
                  const((1, D_MODEL))],
        out_specs=[rowv, cspec, rowq, rowq],
        out_shape=[jax.ShapeDtypeStruct((nbatch, D_MODEL), F32),
                   jax.ShapeDtypeStruct(c.shape, F32),
                   jax.ShapeDtypeStruct((nbatch, qd), F32),
                   jax.ShapeDtypeStruct((nbatch, qd), F32)],
        compiler_params=_cparams(1),
        name="mlstm_step",
    )(q, k, v, o, gm, ig, fg, m, n, c, kall, vt, bi, bf, nmh)


def _swa_step_kernel(q_ref, ga_ref, kn_ref, vn_ref, mk_ref, mv_ref, wk_ref, wv_ref, sink_ref,
                     y_ref, ok_ref, ov_ref):
    W = WINDOW
    hrow = lax.broadcasted_iota(jnp.int32, (N_HEADS, HD), 0)
    first = hrow < GQ
    wcol = lax.broadcasted_iota(jnp.int32, (N_HEADS, W), 1)
    wrow = lax.broadcasted_iota(jnp.int32, (W, N_KV * HD), 0)
    sink = sink_ref[...][:, 0:1]

    def per_kv(fn):
        a0, a1 = fn(0), fn(1)
        return jnp.where(first[:, 0:a0.shape[1]], a0, a1)

    for bb in range(DEC_BLK):
        q = q_ref[bb]
        qb = q.astype(BF16)
        wk, wv = wk_ref[bb], wv_ref[bb]
        mk, mv = mk_ref[bb], mv_ref[bb]
        kn, vn = kn_ref[bb:bb + 1, :], vn_ref[bb:bb + 1, :]
        kvs = lambda a, kv: a[:, kv * HD:(kv + 1) * HD]
        s_w = jnp.where(wcol >= 1, per_kv(lambda kv: _dot_nt(qb, kvs(wk, kv).astype(BF16))), -jnp.inf)
        s_m = per_kv(lambda kv: _dot_nt(qb, kvs(mk, kv).astype(BF16)))
        kn8 = per_kv(lambda kv: jnp.broadcast_to(kvs(kn, kv), (N_HEADS, HD)))
        vn8 = per_kv(lambda kv: jnp.broadcast_to(kvs(vn, kv), (N_HEADS, HD)))
        s_n = jnp.sum(q * kn8, axis=1, keepdims=True)
        mx = jnp.maximum(jnp.maximum(jnp.max(s_w, axis=1, keepdims=True), jnp.max(s_m, axis=1, keepdims=True)),
                         jnp.maximum(s_n, sink))
        p_w, p_m, p_n = jnp.exp(s_w - mx), jnp.exp(s_m - mx), jnp.exp(s_n - mx)
        denom = (jnp.sum(p_w, axis=1, keepdims=True) + jnp.sum(p_m, axis=1, keepdims=True) + p_n
                 + jnp.exp(sink - mx))
        o = (per_kv(lambda kv: _dot(p_w.astype(BF16), kvs(wv, kv).astype(BF16)))
             + per_kv(lambda kv: _dot(p_m.astype(BF16), kvs(mv, kv).astype(BF16)))
             + p_n * vn8)
        y_ref[bb] = (o / denom) * jax.nn.sigmoid(ga_ref[bb])
        ok_ref[bb] = jnp.where(wrow == W - 1, kn, pltpu.roll(wk, W - 1, 0))
        ov_ref[bb] = jnp.where(wrow == W - 1, vn, pltpu.roll(wv, W - 1, 0))


def _swa_step(q, ga, kn, vn, mk, mv, wk, wv, sink8):
    nbatch = q.shape[0]
    kw = N_KV * HD
    hspec = pl.BlockSpec((DEC_BLK, N_HEADS, HD), lambda i: (i, 0, 0))
    nspec = pl.BlockSpec((DEC_BLK, kw), lambda i: (i, 0))
    mspec = pl.BlockSpec((DEC_BLK, N_META, kw), lambda i: (i, 0, 0))
    wspec = pl.BlockSpec((DEC_BLK, WINDOW, kw), lambda i: (i, 0, 0))
    return pl.pallas_call(
        _swa_step_kernel,
        grid=(nbatch // DEC_BLK,),
        in_specs=[hspec, hspec, nspec, nspec, mspec, mspec, wspec, wspec,
                  pl.BlockSpec((N_HEADS, HD), lambda i: (0, 0))],
        out_specs=[hspec, wspec, wspec],
        out_shape=[jax.ShapeDtypeStruct((nbatch, N_HEADS, HD), F32),
                   jax.ShapeDtypeStruct((nbatch, WINDOW, kw), F32),
                   jax.ShapeDtypeStruct((nbatch, WINDOW, kw), F32)],
        compiler_params=_cparams(1),
        name="swa_step",
    )(q, ga, kn, vn, mk, mv, wk, wv, sink8)


def _rope_tables(pos):
    half = ROT_DIM // 2
    inv = jnp.float32(ROPE_THETA) ** (-jnp.arange(half, dtype=F32) * 2.0 / ROT_DIM)
    ang = pos.astype(F32)[:, None] * inv[None, :]
    cos, sin = jnp.cos(ang), jnp.sin(ang)
    t = pos.shape[0]
    cs = jnp.concatenate([cos, cos, jnp.ones((t, HD - ROT_DIM), F32)], axis=1)
    s1 = jnp.concatenate([jnp.zeros((t, half), F32), sin, jnp.zeros((t, HD - ROT_DIM), F32)], axis=1)
    s2 = jnp.concatenate([-sin, jnp.zeros((t, HD - half), F32)], axis=1)
    return cs, s1, s2


def _pack_in_weights(w_in):
    edges = np.concatenate([[0], np.cumsum(IN_WIDTHS)])
    seg = [w_in[:, :, edges[i]:edges[i + 1]] for i in range(len(IN_WIDTHS))]
    q_m, k_m, v_m, ig, fg, o_m, q_a, k_a, v_a, g_m, g_a = seg
    w_nn = jnp.concatenate([q_m, k_a, v_a, q_a, v_m, o_m, g_m, g_a], axis=2).astype(BF16)
    w_kt = jnp.swapaxes(k_m, 1, 2).astype(BF16)
    w_gt = jnp.swapaxes(jnp.concatenate([ig, fg], axis=2), 1, 2)
    w_gt = jnp.pad(w_gt, ((0, 0), (0, 16 - 2 * NH_M), (0, 0))).astype(BF16)
    return w_nn, w_kt, w_gt


def _rep(a, width=128):
    return jnp.repeat(a, width, axis=1)


def kernel(x_prompt, x_sample, state_C, state_n, state_m, cache_meta_k, cache_meta_v, cache_win_k, cache_win_v,
           meta_tokens, norm_mix, w_in, b_igate, b_fgate, norm_mh, attn_sinks, w_out, norm_mlp, w_up, w_down,
           norm_final):
    B, S, _ = x_prompt.shape
    DB = x_sample.shape[0]
    W = cache_win_k.shape[2]
    assert (S % CHUNK, W, x_sample.shape[1], DB % DEC_BLK) == (0, WINDOW, 1, 0)
    nchunk = S // CHUNK
    kw = N_KV * HD
    big_tm = 1024

    w_nn, w_kt, w_gt = _pack_in_weights(w_in)
    w_out_b, w_up_b, w_down_b = w_out.astype(BF16), w_up.astype(BF16), w_down.astype(BF16)

    tabs_p = _rope_tables(N_META + jnp.arange(S, dtype=jnp.int32))
    pos_small = jnp.concatenate([jnp.full((DB,), PAST_LEN, jnp.int32), jnp.arange(N_META, dtype=jnp.int32),
                                 jnp.zeros((SMALL_ROWS - DB - N_META,), jnp.int32)])
    tabs_s = _rope_tables(pos_small)

    xp = x_prompt.reshape(B * S, D_MODEL)
    xs = jnp.concatenate([x_sample.reshape(DB, D_MODEL), meta_tokens.astype(F32),
                          jnp.zeros((SMALL_ROWS - DB - N_META, D_MODEL), F32)], axis=0)
    g_final = norm_final.reshape(1, D_MODEL)
    zero_state = (jnp.zeros((NH_M, DQK_M, DV_M), F32), jnp.zeros((NH_M, DQK_M, 128), F32),
                  jnp.zeros((NH_M, 8, 128), F32))
    no_meta = jnp.zeros((N_META, kw), BF16)

    outs = {k: [] for k in ("pC", "pn", "pm", "pmk", "pmv", "pwk", "pwv", "sC", "sn", "sm", "swk", "swv")}
    for l in range(DEPTH):
        last = l == DEPTH - 1
        g_mix = norm_mix[l].reshape(1, D_MODEL)
        g_mlp = norm_mlp[l].reshape(1, D_MODEL)
        nmh = norm_mh[l].reshape(1, D_MODEL)
        bias8 = jnp.broadcast_to(jnp.concatenate([b_igate[l], b_fgate[l]])[:, None], (2 * NH_M, CHUNK)).astype(F32)
        sinks = attn_sinks[l].astype(F32)

        pk_s = _inproj(xs, g_mix, w_nn[l], tabs_s, SMALL_ROWS)
        kt_s, gt_s = _inproj_t(xs, g_mix, w_kt[l], w_gt[l], SMALL_ROWS)

        ym_meta, ct_m, n_m, m_m = _mlstm(pk_s, kt_s, gt_s, bias8, nmh, *zero_state, nb=1, nc=1,
                                         row0=META_ROW0, valid_len=N_META)
        ya_meta = _swa(pk_s, sinks, no_meta, no_meta, nb=1, nblk=1, row0=META_ROW0, has_meta=False)
        mk = pk_s[META_ROW0:META_ROW0 + N_META, P_KA:P_KA + kw]
        mv = pk_s[META_ROW0:META_ROW0 + N_META, P_VA:P_VA + kw]

        f32s = lambda lo, width: pk_s[:DB, lo:lo + width].astype(F32)
        gcol = gt_s[:, :DB].T
        k_s = kt_s[:, :DB].T
        ys_m, sC, sn, sm = _mlstm_step(
            f32s(P_QM, NH_M * DQK_M), k_s.astype(F32), f32s(P_VM, D_MODEL), f32s(P_OM, D_MODEL),
            f32s(P_GM, D_MODEL), _rep(gcol[:, :NH_M]), _rep(gcol[:, NH_M:]), _rep(state_m[l].astype(F32)),
            state_n[l].astype(F32).reshape(DB, NH_M * DQK_M), state_C[l].astype(F32),
            k_s, pk_s[:DB, P_VM:P_VM + D_MODEL].T,
            _rep(b_igate[l].astype(F32)[None]), _rep(b_fgate[l].astype(F32)[None]), nmh)
        kn, vn = f32s(P_KA, kw), f32s(P_VA, kw)
        ys_a, swk, swv = _swa_step(
            f32s(P_QA, D_MODEL).reshape(DB, N_HEADS, HD), f32s(P_GA, D_MODEL).reshape(DB, N_HEADS, HD), kn, vn,
            cache_meta_k[l].reshape(DB, N_META, kw), cache_meta_v[l].reshape(DB, N_META, kw),
            cache_win_k[l].reshape(DB, W, kw), cache_win_v[l].reshape(DB, W, kw),
            jnp.broadcast_to(sinks[:, None], (N_HEADS, HD)))
        ym_s = jnp.concatenate([ys_m.astype(BF16), ym_meta], axis=0)
        ya_s = jnp.concatenate([ys_a.reshape(DB, D_MODEL).astype(BF16), ya_meta], axis=0)
        xs = _tail(ym_s, ya_s, xs, w_out_b[l], g_mlp, w_up_b[l], w_down_b[l], g_final,
                   tm=SMALL_ROWS, tf=512, final_norm=last)

        pk = _inproj(xp, g_mix, w_nn[l], tabs_p, big_tm)
        kt, gt = _inproj_t(xp, g_mix, w_kt[l], w_gt[l], big_tm)
        ym, ct_f, n_f, m_f = _mlstm(pk, kt, gt, bias8, nmh, ct_m[0], n_m[0], m_m[0], nb=B, nc=nchunk, row0=0)
        ya = _swa(pk, sinks, mk, mv, nb=B, nblk=nchunk, row0=0, has_meta=True)
        xp = _tail(ym, ya, xp, w_out_b[l], g_mlp, w_up_b[l], w_down_b[l], g_final,
                   tm=big_tm, tf=512, final_norm=last)

        outs["pC"].append(jnp.swapaxes(ct_f, 2, 3).astype(state_C.dtype))
        outs["pn"].append(n_f[:, :, :, 0].astype(state_n.dtype))
        outs["pm"].append(m_f[:, :, 0, 0].astype(state_m.dtype))
        outs["pmk"].append(jnp.broadcast_to(mk.astype(F32).reshape(1, N_META, N_KV, HD), (B, N_META, N_KV, HD)))
        outs["pmv"].append(jnp.broadcast_to(mv.astype(F32).reshape(1, N_META, N_KV, HD), (B, N_META, N_KV, HD)))
        pk3 = pk.reshape(B, S, P_TOT)
        outs["pwk"].append(pk3[:, S - WINDOW:, P_KA:P_KA + kw].astype(F32).reshape(B, WINDOW, N_KV, HD))
        outs["pwv"].append(pk3[:, S - WINDOW:, P_VA:P_VA + kw].astype(F32).reshape(B, WINDOW, N_KV, HD))
        outs["sC"].append(sC.astype(state_C.dtype))
        outs["sn"].append(sn.reshape(DB, NH_M, DQK_M).astype(state_n.dtype))
        outs["sm"].append(sm.reshape(DB, NH_M, 128)[:, :, 0].astype(state_m.dtype))
        outs["swk"].append(swk.reshape(DB, W, N_KV, HD).astype(cache_win_k.dtype))
        outs["swv"].append(swv.reshape(DB, W, N_KV, HD).astype(cache_win_v.dtype))

    y_prompt = xp.reshape(B, S, D_MODEL)
    y_sample = xs[:DB].reshape(DB, 1, D_MODEL)
    st = lambda k: jnp.stack(outs[k])
    return (y_prompt, y_sample, st("pC"), st("pn"), st("pm"), st("pmk"), st("pmv"), st("pwk"), st("pwv"),
            st("sC"), st("sn"), st("sm"), st("swk"), st("swv"))
```

```python
import functools

import jax
import jax.numpy as jnp
import numpy as np
from jax import lax
from jax.experimental import pallas as pl
from jax.experimental.pallas import tpu as pltpu

F32 = jnp.float32
BF16 = jnp.bfloat16

D_MODEL = 1024
DEPTH = 4
PAST_LEN = 8192
N_META = 16
NH_M = 4
DV_M = 256
DQK_M = 128
CHUNK = 128
N_HEADS = 8
N_KV = 2
GQ = N_HEADS // N_KV
HD = 128
WINDOW = 128
ROT_DIM = 32
ROPE_THETA = 500000.0
D_FF = 4 * D_MODEL
EPS = 1e-6
IN_WIDTHS = (NH_M * DQK_M, NH_M * DQK_M, NH_M * DV_M, NH_M, NH_M, NH_M * DV_M,
             N_HEADS * HD, N_KV * HD, N_KV * HD, D_MODEL, D_MODEL)

P_QM, P_KA, P_VA, P_QA, P_VM, P_OM, P_GM, P_GA = 0, 512, 768, 1024, 2048, 3072, 4096, 5120
P_TOT = 6144
PROJ_TN = 1024
SMALL_ROWS = 256
META_ROW0 = 128
DEC_BLK = 8
VMEM_LIMIT = 56 * 1024 * 1024


def _cparams(n_axes):
    return pltpu.CompilerParams(dimension_semantics=("arbitrary",) * n_axes, vmem_limit_bytes=VMEM_LIMIT)


def _rms(xf, gain):
    ms = jnp.mean(xf * xf, axis=-1, keepdims=True)
    return xf * lax.rsqrt(ms + EPS) * gain


def _dot(a, b):
    return jnp.dot(a, b, preferred_element_type=F32)


def _dot_nt(a, b):
    return lax.dot_general(a, b, (((1,), (1,)), ((), ())), preferred_element_type=F32)


def _log_sigmoid(x):
    return jnp.minimum(x, 0.0) - jnp.log1p(jnp.exp(-jnp.abs(x)))


def _inproj_kernel(x_ref, gain_ref, w_ref, cs_ref, s1_ref, s2_ref, o_ref, h_ref):
    j = pl.program_id(1)

    @pl.when(j == 0)
    def _():
        h_ref[...] = _rms(x_ref[...], gain_ref[...]).astype(BF16)

    acc = _dot(h_ref[...], w_ref[...])

    def rope(seg):
        return (seg * cs_ref[...] + pltpu.roll(seg, 16, 1) * s1_ref[...]
                + pltpu.roll(seg, HD - 16, 1) * s2_ref[...])

    @pl.when(j == 0)
    def _():
        o_ref[:, P_QM:P_KA] = (acc[:, P_QM:P_KA] * (DQK_M ** -0.5)).astype(BF16)
        for c in range(N_KV):
            lo = P_KA + c * HD
            o_ref[:, lo:lo + HD] = rope(acc[:, lo:lo + HD]).astype(BF16)
        o_ref[:, P_VA:P_QA] = acc[:, P_VA:P_QA].astype(BF16)

    @pl.when(j == 1)
    def _():
        for c in range(N_HEADS):
            lo = c * HD
            o_ref[:, lo:lo + HD] = (rope(acc[:, lo:lo + HD]) * (HD ** -0.5)).astype(BF16)

    @pl.when(j >= 2)
    def _():
        o_ref[...] = acc.astype(BF16)


def _inproj(l, x, gains, w_nn, tabs, tm):
    rows = x.shape[0]
    tab_blocks = tabs[0].shape[0] // tm
    tab_spec = pl.BlockSpec((tm, HD), lambda i, j: (i % tab_blocks, 0))
    return pl.pallas_call(
        _inproj_kernel,
        grid=(rows // tm, P_TOT // PROJ_TN),
        in_specs=[pl.BlockSpec((tm, D_MODEL), lambda i, j: (i, 0)),
                  pl.BlockSpec((None, 1, D_MODEL), lambda i, j: (l, 0, 0)),
                  pl.BlockSpec((None, D_MODEL, PROJ_TN), lambda i, j: (l, 0, j)),
                  tab_spec, tab_spec, tab_spec],
        out_specs=pl.BlockSpec((tm, PROJ_TN), lambda i, j: (i, j)),
        out_shape=jax.ShapeDtypeStruct((rows, P_TOT), BF16),
        scratch_shapes=[pltpu.VMEM((tm, D_MODEL), BF16)],
        compiler_params=_cparams(2),
        name="inproj",
    )(x, gains, w_nn, *tabs)


KG_COLS = NH_M * DQK_M + 128


def _inproj_t_kernel(x_ref, gain_ref, w_ref, kt_ref, gt_ref):
    h = _rms(x_ref[...], gain_ref[...]).astype(BF16)
    acc = _dot(h, w_ref[...])
    kt_ref[...] = acc[:, 0:NH_M * DQK_M].T.astype(BF16)
    gt_ref[...] = acc[:, NH_M * DQK_M:KG_COLS].T[0:2 * NH_M]


def _inproj_t(l, x, gains, w_kg, tm):
    rows = x.shape[0]
    return pl.pallas_call(
        _inproj_t_kernel,
        grid=(rows // tm,),
        in_specs=[pl.BlockSpec((tm, D_MODEL), lambda i: (i, 0)),
                  pl.BlockSpec((None, 1, D_MODEL), lambda i: (l, 0, 0)),
                  pl.BlockSpec((None, D_MODEL, KG_COLS), lambda i: (l, 0, 0))],
        out_specs=[pl.BlockSpec((NH_M * DQK_M, tm), lambda i: (0, i)),
                   pl.BlockSpec((2 * NH_M, tm), lambda i: (0, i))],
        out_shape=[jax.ShapeDtypeStruct((NH_M * DQK_M, rows), BF16),
                   jax.ShapeDtypeStruct((2 * NH_M, rows), F32)],
        compiler_params=_cparams(1),
        name="inproj_t",
    )(x, gains, w_kg)


def _mlstm_kernel(q_ref, kt_ref, v_ref, o_ref, gm_ref, g_ref, bias_ref, nmh_ref, ct0_ref, n0_ref, m0_ref,
                  y_ref, ctf_ref, nf_ref, mf_ref, ct_s, n_s, m_s, *, nc, valid_len):
    c = pl.program_id(1)
    L = CHUNK

    @pl.when(c == 0)
    def _():
        ct_s[...] = ct0_ref[...]
        n_s[...] = n0_ref[...]
        m_s[...] = m0_ref[...]

    row = lax.broadcasted_iota(jnp.int32, (L, L), 0)
    col = lax.broadcasted_iota(jnp.int32, (L, L), 1)
    tril = col <= row
    eye = col == row

    gates = g_ref[...] + bias_ref[...]
    ig_all = gates[0:NH_M]
    lf_all = _log_sigmoid(gates[NH_M:2 * NH_M])
    if valid_len < L:
        ok = lax.broadcasted_iota(jnp.int32, (NH_M, L), 1) < valid_len
        ig_all = jnp.where(ok, ig_all, -jnp.inf)
        lf_all = jnp.where(ok, lf_all, 0.0)

    for h in range(NH_M):
        ig = ig_all[h:h + 1]
        lf = lf_all[h:h + 1]
        m_prev = m_s[h, 0:1, 0:1]
        b_col = jnp.sum(jnp.where(tril, lf, 0.0), axis=1, keepdims=True)
        b_row = jnp.sum(jnp.where(eye, b_col, 0.0), axis=0, keepdims=True)
        a_col = b_col + m_prev
        dmat = jnp.where(tril, b_col - b_row + ig, -jnp.inf)
        m_col = jnp.maximum(a_col, jnp.max(dmat, axis=1, keepdims=True))
        w_inter = jnp.exp(a_col - m_col)
        wmat = jnp.exp(dmat - m_col)

        q = q_ref[:, h * DQK_M:(h + 1) * DQK_M]
        kt = kt_ref[h * DQK_M:(h + 1) * DQK_M, :]
        v = v_ref[:, h * DV_M:(h + 1) * DV_M]
        ct = ct_s[h]
        nm = n_s[h]

        s = _dot(q, kt) * wmat
        num = w_inter * _dot(q, ct.astype(BF16)) + _dot(s.astype(BF16), v)
        qn = _dot(q, nm.astype(BF16))[:, 0:1]
        den = w_inter * qn + jnp.sum(s, axis=1, keepdims=True)
        hh = num / jnp.maximum(jnp.abs(den), jnp.exp(-m_col))

        m_new = m_col[L - 1:L]
        decay = w_inter[L - 1:L]
        g_row = jnp.exp(b_col[L - 1:L] - b_row + ig - m_new)
        ktg = kt.astype(F32) * g_row
        ct_s[h] = decay * ct + _dot(ktg.astype(BF16), v)
        n_s[h] = decay * nm + jnp.sum(ktg, axis=1, keepdims=True)
        m_s[h] = jnp.broadcast_to(m_new, (8, 128))

        hs = slice(h * DV_M, (h + 1) * DV_M)
        hn = hh * lax.rsqrt(jnp.mean(hh * hh, axis=1, keepdims=True) + EPS) * nmh_ref[:, hs]
        y = jax.nn.sigmoid(o_ref[:, hs].astype(F32)) * hn
        y_ref[:, hs] = (jax.nn.sigmoid(gm_ref[:, hs].astype(F32)) * y).astype(BF16)

    @pl.when(c == nc - 1)
    def _():
        ctf_ref[0] = ct_s[...]
        nf_ref[0] = n_s[...]
        mf_ref[0] = m_s[...]


def _mlstm(pk, kt, gt, bias8, nmh, ct0, n0, m0, nb, nc, row0, valid_len=CHUNK):
    rows = pk.shape[0]
    r0 = row0 // CHUNK
    rb = lambda b, c: r0 + b * nc + c
    full3 = lambda b, c: (0, 0, 0)
    st4 = lambda b, c: (b, 0, 0, 0)
    return pl.pallas_call(
        functools.partial(_mlstm_kernel, nc=nc, valid_len=valid_len),
        grid=(nb, nc),
        in_specs=[pl.BlockSpec((CHUNK, NH_M * DQK_M), lambda b, c: (rb(b, c), P_QM // (NH_M * DQK_M))),
                  pl.BlockSpec((NH_M * DQK_M, CHUNK), lambda b, c: (0, rb(b, c))),
                  pl.BlockSpec((CHUNK, D_MODEL), lambda b, c: (rb(b, c), P_VM // D_MODEL)),
                  pl.BlockSpec((CHUNK, D_MODEL), lambda b, c: (rb(b, c), P_OM // D_MODEL)),
                  pl.BlockSpec((CHUNK, D_MODEL), lambda b, c: (rb(b, c), P_GM // D_MODEL)),
                  pl.BlockSpec((2 * NH_M, CHUNK), lambda b, c: (0, rb(b, c))),
                  pl.BlockSpec((2 * NH_M, CHUNK), lambda b, c: (0, 0)),
                  pl.BlockSpec((1, D_MODEL), lambda b, c: (0, 0)),
                  pl.BlockSpec((NH_M, DQK_M, DV_M), full3),
                  pl.BlockSpec((NH_M, DQK_M, 128), full3),
                  pl.BlockSpec((NH_M, 8, 128), full3)],
        out_specs=[pl.BlockSpec((CHUNK, D_MODEL), lambda b, c: (b * nc + c, 0)),
                   pl.BlockSpec((1, NH_M, DQK_M, DV_M), st4),
                   pl.BlockSpec((1, NH_M, DQK_M, 128), st4),
                   pl.BlockSpec((1, NH_M, 8, 128), st4)],
        out_shape=[jax.ShapeDtypeStruct((nb * nc * CHUNK, D_MODEL), BF16),
                   jax.ShapeDtypeStruct((nb, NH_M, DQK_M, DV_M), F32),
                   jax.ShapeDtypeStruct((nb, NH_M, DQK_M, 128), F32),
                   jax.ShapeDtypeStruct((nb, NH_M, 8, 128), F32)],
        scratch_shapes=[pltpu.VMEM((NH_M, DQK_M, DV_M), F32),
                        pltpu.VMEM((NH_M, DQK_M, 128), F32),
                        pltpu.VMEM((NH_M, 8, 128), F32)],
        compiler_params=_cparams(2),
        name="mlstm",
    )(pk, kt, pk, pk, pk, gt, bias8, nmh, ct0, n0, m0)


def _swa_kernel(sink_ref, q_ref, kc_ref, kp_ref, vc_ref, vp_ref, mk_ref, mv_ref, ga_ref, y_ref, *, has_meta):
    n = pl.program_id(1)
    L = WINDOW
    row = lax.broadcasted_iota(jnp.int32, (L, L), 0)
    col = lax.broadcasted_iota(jnp.int32, (L, L), 1)
    tril = col <= row
    prev_ok = jnp.full((L, L), n, jnp.int32) > 0
    for kv in range(N_KV):
        ks = slice(kv * HD, (kv + 1) * HD)
        kc, kp, vc, vp = kc_ref[:, ks], kp_ref[:, ks], vc_ref[:, ks], vp_ref[:, ks]
        for g in range(GQ):
            hd = kv * GQ + g
            hs = slice(hd * HD, (hd + 1) * HD)
            q = q_ref[:, hs]
            sink = sink_ref[hd]
            band = jnp.where(tril, _dot_nt(q, kc), jnp.where(prev_ok, _dot_nt(q, kp), -jnp.inf))
            mx = jnp.maximum(jnp.max(band, axis=1, keepdims=True), sink)
            if has_meta:
                s_m = _dot_nt(q, mk_ref[:, ks])
                mx = jnp.maximum(mx, jnp.max(s_m, axis=1, keepdims=True))
            p = jnp.exp(band - mx)
            denom = jnp.sum(p, axis=1, keepdims=True) + jnp.exp(sink - mx)
            o = (_dot(jnp.where(tril, p, 0.0).astype(BF16), vc)
                 + _dot(jnp.where(tril, 0.0, p).astype(BF16), vp))
            if has_meta:
                p_m = jnp.exp(s_m - mx)
                denom = denom + jnp.sum(p_m, axis=1, keepdims=True)
                o = o + _dot(p_m.astype(BF16), mv_ref[:, ks])
            y = (o / denom) * jax.nn.sigmoid(ga_ref[:, hs].astype(F32))
            y_ref[:, hs] = y.astype(BF16)


def _swa(pk, sinks, mk, mv, nb, nblk, row0, has_meta):
    r0 = row0 // WINDOW
    rb = lambda b, n: r0 + b * nblk + n
    rp = lambda b, n: r0 + b * nblk + jnp.maximum(n - 1, 0)
    kw = N_KV * HD
    return pl.pallas_call(
        functools.partial(_swa_kernel, has_meta=has_meta),
        grid=(nb, nblk),
        in_specs=[pl.BlockSpec(memory_space=pltpu.SMEM),
                  pl.BlockSpec((WINDOW, D_MODEL), lambda b, n: (rb(b, n), P_QA // D_MODEL)),
                  pl.BlockSpec((WINDOW, kw), lambda b, n: (rb(b, n), P_KA // kw)),
                  pl.BlockSpec((WINDOW, kw), lambda b, n: (rp(b, n), P_KA // kw)),
                  pl.BlockSpec((WINDOW, kw), lambda b, n: (rb(b, n), P_VA // kw)),
                  pl.BlockSpec((WINDOW, kw), lambda b, n: (rp(b, n), P_VA // kw)),
                  pl.BlockSpec((N_META, kw), lambda b, n: (0, 0)),
                  pl.BlockSpec((N_META, kw), lambda b, n: (0, 0)),
                  pl.BlockSpec((WINDOW, D_MODEL), lambda b, n: (rb(b, n), P_GA // D_MODEL))],
        out_specs=pl.BlockSpec((WINDOW, D_MODEL), lambda b, n: (b * nblk + n, 0)),
        out_shape=jax.ShapeDtypeStruct((nb * nblk * WINDOW, D_MODEL), BF16),
        compiler_params=_cparams(2),
        name="swa",
    )(sinks, pk, pk, pk, pk, pk, mk, mv, pk)


def _tail_kernel(ym_ref, ya_ref, x_ref, wo_ref, gn_ref, wu_ref, wd_ref, gf_ref, o_ref, hf_ref, *, nj, final_norm):
    j = pl.program_id(1)

    @pl.when(j == 0)
    def _():
        y = (ym_ref[...].astype(F32) + ya_ref[...].astype(F32)).astype(BF16)
        x1 = x_ref[...] + _dot(y, wo_ref[...])
        o_ref[...] = x1
        hf_ref[...] = _rms(x1, gn_ref[...]).astype(BF16)

    a = jnp.square(jnp.maximum(_dot(hf_ref[...], wu_ref[...]), 0.0)).astype(BF16)
    o_ref[...] += _dot(a, wd_ref[...])

    if final_norm:
        @pl.when(j == nj - 1)
        def _():
            o_ref[...] = _rms(o_ref[...], gf_ref[...])


def _tail(l, ym, ya, x, w_out, g_mlp, w_up, w_down, g_final, tm, tf, final_norm):
    rows = x.shape[0]
    nj = D_FF // tf
    return pl.pallas_call(
        functools.partial(_tail_kernel, nj=nj, final_norm=final_norm),
        grid=(rows // tm, nj),
        in_specs=[pl.BlockSpec((tm, D_MODEL), lambda i, j: (i, 0)),
                  pl.BlockSpec((tm, D_MODEL), lambda i, j: (i, 0)),
                  pl.BlockSpec((tm, D_MODEL), lambda i, j: (i, 0)),
                  pl.BlockSpec((None, D_MODEL, D_MODEL), lambda i, j: (l, 0, 0)),
                  pl.BlockSpec((None, 1, D_MODEL), lambda i, j: (l, 0, 0)),
                  pl.BlockSpec((None, D_MODEL, tf), lambda i, j: (l, 0, j)),
                  pl.BlockSpec((None, tf, D_MODEL), lambda i, j: (l, j, 0)),
                  pl.BlockSpec((1, D_MODEL), lambda i, j: (0, 0))],
        out_specs=pl.BlockSpec((tm, D_MODEL), lambda i, j: (i, 0)),
        out_shape=jax.ShapeDtypeStruct((rows, D_MODEL), F32),
        scratch_shapes=[pltpu.VMEM((tm, D_MODEL), BF16)],
        compiler_params=_cparams(2),
        name="tail",
    )(ym, ya, x, w_out, g_mlp, w_up, w_down, g_final)


def _mlstm_step_kernel(q_ref, k_ref, v_ref, o_ref, gm_ref, ig_ref, fg_ref, m_ref, n_ref, c_ref,
                       kall_ref, vt_ref, bi_ref, bf_ref, nmh_ref, *rest):
    y_ref, co_ref, no_ref, mo_ref = rest[-4:]
    i = pl.program_id(0)
    nb = DEC_BLK
    row8 = lax.broadcasted_iota(jnp.int32, (nb, DV_M), 0)
    rowb = lax.broadcasted_iota(jnp.int32, (kall_ref.shape[0], DQK_M), 0)
    for h in range(NH_M):
        ds = slice(h * DQK_M, (h + 1) * DQK_M)
        vs = slice(h * DV_M, (h + 1) * DV_M)
        q, k, v, nrm = q_ref[:, ds], k_ref[:, ds], v_ref[:, vs], n_ref[:, ds]
        ig = ig_ref[:, ds] + bi_ref[:, ds]
        lf = _log_sigmoid(fg_ref[:, ds] + bf_ref[:, ds])
        m_prev = m_ref[:, ds]
        m_t = jnp.maximum(lf + m_prev, ig)
        w_inter = jnp.exp(lf + m_prev - m_t)
        w_in = jnp.exp(ig - m_t)
        s = jnp.sum(q * k, axis=1, keepdims=True) * w_in
        qb = q.astype(BF16)
        kall = kall_ref[:, ds]
        vt = vt_ref[vs, :]
        cq = jnp.zeros((nb, DV_M), F32)
        for bb in range(nb):
            c_old = c_ref[bb, h]
            cq = jnp.where(row8 == bb, _dot_nt(qb, c_old.astype(BF16)), cq)
            outer = _dot(vt, jnp.where(rowb == i * nb + bb, kall, jnp.zeros_like(kall)))
            co_ref[bb, h] = w_inter[bb:bb + 1, :] * c_old + w_in[bb:bb + 1, :] * outer
        num = w_inter[:, 0:1] * cq + s[:, 0:1] * v
        den = w_inter * jnp.sum(nrm * q, axis=1, keepdims=True) + s
        hh = num / jnp.maximum(jnp.abs(den), jnp.exp(-m_t))[:, 0:1]
        no_ref[:, ds] = w_inter * nrm + w_in * k
        mo_ref[:, ds] = m_t
        hn = hh * lax.rsqrt(jnp.mean(hh * hh, axis=1, keepdims=True) + EPS) * nmh_ref[:, vs]
        y_ref[:, vs] = jax.nn.sigmoid(gm_ref[:, vs]) * (jax.nn.sigmoid(o_ref[:, vs]) * hn)


def _mlstm_step(l, q, k, v, o, gm, ig, fg, m, n, c_all, c_out_prev, kall, vt, bi, bf, nmh):
    nbatch = q.shape[0]
    qd = NH_M * DQK_M
    rowq = pl.BlockSpec((DEC_BLK, qd), lambda i: (i, 0))
    rowv = pl.BlockSpec((DEC_BLK, D_MODEL), lambda i: (i, 0))
    cspec = pl.BlockSpec((None, DEC_BLK, NH_M, DV_M, DQK_M), lambda i: (l, i, 0, 0, 0))
    const = lambda shape: pl.BlockSpec(shape, lambda i: (0, 0))
    in_specs = [rowq, rowq, rowv, rowv, rowv, rowq, rowq, rowq, rowq, cspec,
                const((nbatch, qd)), const((D_MODEL, nbatch)), const((1, qd)), const((1, qd)),
                const((1, D_MODEL))]
    args = [q, k, v, o, gm, ig, fg, m, n, c_all, kall, vt, bi, bf, nmh]
    aliases = {}
    if c_out_prev is not None:
        aliases = {len(args): 1}
        in_specs.append(pl.BlockSpec(memory_space=pl.ANY))
        args.append(c_out_prev)
    return pl.pallas_call(
        _mlstm_step_kernel,
        grid=(nbatch // DEC_BLK,),
        in_specs=in_specs,
        out_specs=[rowv, cspec, rowq, rowq],
        out_shape=[jax.ShapeDtypeStruct((nbatch, D_MODEL), F32),
                   jax.ShapeDtypeStruct(c_all.shape, F32),
                   jax.ShapeDtypeStruct((nbatch, qd), F32),
                   jax.ShapeDtypeStruct((nbatch, qd), F32)],
        input_output_aliases=aliases,
        compiler_params=_cparams(1),
        name="mlstm_step",
    )(*args)


def _swa_step_kernel(q_ref, ga_ref, kn_ref, vn_ref, mk_ref, mv_ref, wk_ref, wv_ref, sink_ref, *rest):
    y_ref, ok_ref, ov_ref = rest[-3:]
    W = WINDOW
    hrow = lax.broadcasted_iota(jnp.int32, (N_HEADS, HD), 0)
    first = hrow < GQ
    wcol = lax.broadcasted_iota(jnp.int32, (N_HEADS, W), 1)
    wrow = lax.broadcasted_iota(jnp.int32, (N_KV * W, HD), 0)
    sink = sink_ref[...][:, 0:1]

    def per_kv(fn):
        a0, a1 = fn(0), fn(1)
        return jnp.where(first[:, 0:a0.shape[1]], a0, a1)

    for bb in range(DEC_BLK):
        q = q_ref[bb]
        qb = q.astype(BF16)
        rows_of = lambda ref, kv, n: ref[bb, pl.ds(kv, n, stride=N_KV), :].astype(BF16)
        kn, vn = kn_ref[bb:bb + 1, :], vn_ref[bb:bb + 1, :]
        kvs = lambda a, kv: a[:, kv * HD:(kv + 1) * HD]
        s_w = jnp.where(wcol >= 1, per_kv(lambda kv: _dot_nt(qb, rows_of(wk_ref, kv, W))), -jnp.inf)
        s_m = per_kv(lambda kv: _dot_nt(qb, rows_of(mk_ref, kv, N_META)))
        kn8 = per_kv(lambda kv: jnp.broadcast_to(kvs(kn, kv), (N_HEADS, HD)))
        vn8 = per_kv(lambda kv: jnp.broadcast_to(kvs(vn, kv), (N_HEADS, HD)))
        s_n = jnp.sum(q * kn8, axis=1, keepdims=True)
        mx = jnp.maximum(jnp.maximum(jnp.max(s_w, axis=1, keepdims=True), jnp.max(s_m, axis=1, keepdims=True)),
                         jnp.maximum(s_n, sink))
        p_w, p_m, p_n = jnp.exp(s_w - mx), jnp.exp(s_m - mx), jnp.exp(s_n - mx)
        denom = (jnp.sum(p_w, axis=1, keepdims=True) + jnp.sum(p_m, axis=1, keepdims=True) + p_n
                 + jnp.exp(sink - mx))
        o = (per_kv(lambda kv: _dot(p_w.astype(BF16), rows_of(wv_ref, kv, W)))
             + per_kv(lambda kv: _dot(p_m.astype(BF16), rows_of(mv_ref, kv, N_META)))
             + p_n * vn8)
        y_ref[bb] = (o / denom) * jax.nn.sigmoid(ga_ref[bb])
        for src_ref, new, dst_ref in ((wk_ref, kn, ok_ref), (wv_ref, vn, ov_ref)):
            shifted = pltpu.roll(src_ref[bb], N_KV * (W - 1), 0)
            for kv in range(N_KV):
                shifted = jnp.where(wrow == N_KV * (W - 1) + kv, kvs(new, kv), shifted)
            dst_ref[bb] = shifted


def _swa_step(l, q, ga, kn, vn, mk_all, mv_all, wk_all, wv_all, ok_prev, ov_prev, sink8):
    nbatch = q.shape[0]
    kw = N_KV * HD
    hspec = pl.BlockSpec((DEC_BLK, N_HEADS, HD), lambda i: (i, 0, 0))
    nspec = pl.BlockSpec((DEC_BLK, kw), lambda i: (i, 0))
    mspec = pl.BlockSpec((None, DEC_BLK, N_KV * N_META, HD), lambda i: (l, i, 0, 0))
    wspec = pl.BlockSpec((None, DEC_BLK, N_KV * WINDOW, HD), lambda i: (l, i, 0, 0))
    in_specs = [hspec, hspec, nspec, nspec, mspec, mspec, wspec, wspec,
                pl.BlockSpec((N_HEADS, HD), lambda i: (0, 0))]
    args = [q, ga, kn, vn, mk_all, mv_all, wk_all, wv_all, sink8]
    aliases = {}
    if ok_prev is not None:
        aliases = {len(args): 1, len(args) + 1: 2}
        in_specs += [pl.BlockSpec(memory_space=pl.ANY)] * 2
        args += [ok_prev, ov_prev]
    return pl.pallas_call(
        _swa_step_kernel,
        grid=(nbatch // DEC_BLK,),
        in_specs=in_specs,
        out_specs=[hspec, wspec, wspec],
        out_shape=[jax.ShapeDtypeStruct((nbatch, N_HEADS, HD), F32),
                   jax.ShapeDtypeStruct(wk_all.shape, F32),
                   jax.ShapeDtypeStruct(wv_all.shape, F32)],
        input_output_aliases=aliases,
        compiler_params=_cparams(1),
        name="swa_step",
    )(*args)


def _rope_tables(pos):
    half = ROT_DIM // 2
    inv = jnp.float32(ROPE_THETA) ** (-jnp.arange(half, dtype=F32) * 2.0 / ROT_DIM)
    ang = pos.astype(F32)[:, None] * inv[None, :]
    cos, sin = jnp.cos(ang), jnp.sin(ang)
    t = pos.shape[0]
    cs = jnp.concatenate([cos, cos, jnp.ones((t, HD - ROT_DIM), F32)], axis=1)
    s1 = jnp.concatenate([jnp.zeros((t, half), F32), sin, jnp.zeros((t, HD - ROT_DIM), F32)], axis=1)
    s2 = jnp.concatenate([-sin, jnp.zeros((t, HD - half), F32)], axis=1)
    return cs, s1, s2


def _pack_in_weights(w_in):
    edges = np.concatenate([[0], np.cumsum(IN_WIDTHS)])
    seg = [w_in[:, :, edges[i]:edges[i + 1]] for i in range(len(IN_WIDTHS))]
    q_m, k_m, v_m, ig, fg, o_m, q_a, k_a, v_a, g_m, g_a = seg
    w_nn = jnp.concatenate([q_m, k_a, v_a, q_a, v_m, o_m, g_m, g_a], axis=2).astype(BF16)
    pad = jnp.zeros(w_in.shape[:2] + (KG_COLS - NH_M * DQK_M - 2 * NH_M,), w_in.dtype)
    w_kg = jnp.concatenate([k_m, ig, fg, pad], axis=2).astype(BF16)
    return w_nn, w_kg


def _rep(a, width=128):
    return jnp.repeat(a, width, axis=1)


def kernel(x_prompt, x_sample, state_C, state_n, state_m, cache_meta_k, cache_meta_v, cache_win_k, cache_win_v,
           meta_tokens, norm_mix, w_in, b_igate, b_fgate, norm_mh, attn_sinks, w_out, norm_mlp, w_up, w_down,
           norm_final):
    B, S, _ = x_prompt.shape
    DB = x_sample.shape[0]
    W = cache_win_k.shape[2]
    assert (S % CHUNK, W, x_sample.shape[1], DB % DEC_BLK) == (0, WINDOW, 1, 0)
    assert (state_C.dtype, cache_win_k.dtype, cache_win_v.dtype) == (F32, F32, F32)
    nchunk = S // CHUNK
    kw = N_KV * HD
    big_tm = 1024

    w_nn, w_kg = _pack_in_weights(w_in)
    w_out_b, w_up_b, w_down_b = w_out.astype(BF16), w_up.astype(BF16), w_down.astype(BF16)
    g_mix = norm_mix.astype(F32).reshape(DEPTH, 1, D_MODEL)
    g_mlp = norm_mlp.astype(F32).reshape(DEPTH, 1, D_MODEL)
    rows_view = lambda a: a.reshape(DEPTH, DB, -1, HD)
    mk_all, mv_all = rows_view(cache_meta_k), rows_view(cache_meta_v)
    wk_all, wv_all = rows_view(cache_win_k), rows_view(cache_win_v)

    tabs_p = _rope_tables(N_META + jnp.arange(S, dtype=jnp.int32))
    pos_small = jnp.concatenate([jnp.full((DB,), PAST_LEN, jnp.int32), jnp.arange(N_META, dtype=jnp.int32),
                                 jnp.zeros((SMALL_ROWS - DB - N_META,), jnp.int32)])
    tabs_s = _rope_tables(pos_small)

    xp = x_prompt.reshape(B * S, D_MODEL)
    xs = jnp.concatenate([x_sample.reshape(DB, D_MODEL), meta_tokens.astype(F32),
                          jnp.zeros((SMALL_ROWS - DB - N_META, D_MODEL), F32)], axis=0)
    g_final = norm_final.reshape(1, D_MODEL)
    zero_state = (jnp.zeros((NH_M, DQK_M, DV_M), F32), jnp.zeros((NH_M, DQK_M, 128), F32),
                  jnp.zeros((NH_M, 8, 128), F32))
    no_meta = jnp.zeros((N_META, kw), BF16)

    outs = {k: [] for k in ("pC", "pn", "pm", "pmk", "pmv", "pwk", "pwv", "sn", "sm")}
    sC = swk = swv = None
    for l in range(DEPTH):
        last = l == DEPTH - 1
        nmh = norm_mh[l].reshape(1, D_MODEL)
        bias8 = jnp.broadcast_to(jnp.concatenate([b_igate[l], b_fgate[l]])[:, None], (2 * NH_M, CHUNK)).astype(F32)
        sinks = attn_sinks[l].astype(F32)

        pk_s = _inproj(l, xs, g_mix, w_nn, tabs_s, SMALL_ROWS)
        kt_s, gt_s = _inproj_t(l, xs, g_mix, w_kg, SMALL_ROWS)

        ym_meta, ct_m, n_m, m_m = _mlstm(pk_s, kt_s, gt_s, bias8, nmh, *zero_state, nb=1, nc=1,
                                         row0=META_ROW0, valid_len=N_META)
        ya_meta = _swa(pk_s, sinks, no_meta, no_meta, nb=1, nblk=1, row0=META_ROW0, has_meta=False)
        mk = pk_s[META_ROW0:META_ROW0 + N_META, P_KA:P_KA + kw]
        mv = pk_s[META_ROW0:META_ROW0 + N_META, P_VA:P_VA + kw]

        f32s = lambda lo, width: pk_s[:DB, lo:lo + width].astype(F32)
        gcol = gt_s[:, :DB].T
        k_s = kt_s[:, :DB].T
        ys_m, sC, sn, sm = _mlstm_step(
            l, f32s(P_QM, NH_M * DQK_M), k_s.astype(F32), f32s(P_VM, D_MODEL), f32s(P_OM, D_MODEL),
            f32s(P_GM, D_MODEL), _rep(gcol[:, :NH_M]), _rep(gcol[:, NH_M:]), _rep(state_m[l].astype(F32)),
            state_n[l].astype(F32).reshape(DB, NH_M * DQK_M), state_C, sC,
            k_s, pk_s[:DB, P_VM:P_VM + D_MODEL].T,
            _rep(b_igate[l].astype(F32)[None]), _rep(b_fgate[l].astype(F32)[None]), nmh)
        ys_a, swk, swv = _swa_step(
            l, f32s(P_QA, D_MODEL).reshape(DB, N_HEADS, HD), f32s(P_GA, D_MODEL).reshape(DB, N_HEADS, HD),
            f32s(P_KA, kw), f32s(P_VA, kw), mk_all, mv_all, wk_all, wv_all, swk, swv,
            jnp.broadcast_to(sinks[:, None], (N_HEADS, HD)))
        ym_s = jnp.concatenate([ys_m.astype(BF16), ym_meta], axis=0)
        ya_s = jnp.concatenate([ys_a.reshape(DB, D_MODEL).astype(BF16), ya_meta], axis=0)
        xs = _tail(l, ym_s, ya_s, xs, w_out_b, g_mlp, w_up_b, w_down_b, g_final,
                   tm=SMALL_ROWS, tf=512, final_norm=last)

        pk = _inproj(l, xp, g_mix, w_nn, tabs_p, big_tm)
        kt, gt = _inproj_t(l, xp, g_mix, w_kg, big_tm)
        ym, ct_f, n_f, m_f = _mlstm(pk, kt, gt, bias8, nmh, ct_m[0], n_m[0], m_m[0], nb=B, nc=nchunk, row0=0)
        ya = _swa(pk, sinks, mk, mv, nb=B, nblk=nchunk, row0=0, has_meta=True)
        xp = _tail(l, ym, ya, xp, w_out_b, g_mlp, w_up_b, w_down_b, g_final,
                   tm=big_tm, tf=512, final_norm=last)

        outs["pC"].append(jnp.swapaxes(ct_f, 2, 3).astype(state_C.dtype))
        outs["pn"].append(n_f[:, :, :, 0].astype(state_n.dtype))
        outs["pm"].append(m_f[:, :, 0, 0].astype(state_m.dtype))
        outs["pmk"].append(jnp.broadcast_to(mk.astype(F32).reshape(1, N_META, N_KV, HD), (B, N_META, N_KV, HD)))
        outs["pmv"].append(jnp.broadcast_to(mv.astype(F32).reshape(1, N_META, N_KV, HD), (B, N_META, N_KV, HD)))
        pk3 = pk.reshape(B, S, P_TOT)
        outs["pwk"].append(pk3[:, S - WINDOW:, P_KA:P_KA + kw].astype(F32).reshape(B, WINDOW, N_KV, HD))
        outs["pwv"].append(pk3[:, S - WINDOW:, P_VA:P_VA + kw].astype(F32).reshape(B, WINDOW, N_KV, HD))
        outs["sn"].append(sn.reshape(DB, NH_M, DQK_M).astype(state_n.dtype))
        outs["sm"].append(sm.reshape(DB, NH_M, 128)[:, :, 0].astype(state_m.dtype))

    y_prompt = xp.reshape(B, S, D_MODEL)
    y_sample = xs[:DB].reshape(DB, 1, D_MODEL)
    st = lambda k: jnp.stack(outs[k])
    return (y_prompt, y_sample, st("pC"), st("pn"), st("pm"), st("pmk"), st("pmv"), st("pwk"), st("pwv"),
            sC, st("sn"), st("sm"), swk.reshape(cache_win_k.shape), swv.reshape(cache_win_v.shape))
```

```python
import functools

import jax
import jax.numpy as jnp
import numpy as np
from jax import lax
from jax.experimental import pallas as pl
from jax.experimental.pallas import tpu as pltpu

F32 = jnp.float32
BF16 = jnp.bfloat16

D_MODEL = 1024
DEPTH = 4
PAST_LEN = 8192
N_META = 16
NH_M = 4
DV_M = 256
DQK_M = 128
CHUNK = 128
N_HEADS = 8
N_KV = 2
GQ = N_HEADS // N_KV
HD = 128
WINDOW = 128
ROT_DIM = 32
ROPE_THETA = 500000.0
D_FF = 4 * D_MODEL
EPS = 1e-6
IN_WIDTHS = (NH_M * DQK_M, NH_M * DQK_M, NH_M * DV_M, NH_M, NH_M, NH_M * DV_M,
             N_HEADS * HD, N_KV * HD, N_KV * HD, D_MODEL, D_MODEL)

A_VM, A_OM, A_GM, A_GA, A_TOT = 0, 1024, 2048, 3072, 4096
A_TN = 2048
B_QA, B_KA, B_VA, B_QM, B_KM, B_G, B_TOT = 0, 1024, 1280, 1536, 2048, 2560, 2688
M_KA, M_VA, M_QM, M_TOT = 0, 256, 512, 1024
TAIL_TF = 1024
SMALL_ROWS = 256
META_ROW0 = 128
DEC_BLK = 8
VMEM_LIMIT = 56 * 1024 * 1024


def _cparams(n_axes):
    return pltpu.CompilerParams(dimension_semantics=("arbitrary",) * n_axes, vmem_limit_bytes=VMEM_LIMIT)


def _rms(xf, gain):
    ms = jnp.mean(xf * xf, axis=-1, keepdims=True)
    return xf * lax.rsqrt(ms + EPS) * gain


def _dot(a, b):
    return jnp.dot(a, b, preferred_element_type=F32)


def _dot_nt(a, b):
    return lax.dot_general(a, b, (((1,), (1,)), ((), ())), preferred_element_type=F32)


def _log_sigmoid(x):
    return jnp.minimum(x, 0.0) - jnp.log1p(jnp.exp(-jnp.abs(x)))


def _norm_kernel(x_ref, gain_ref, o_ref):
    o_ref[...] = _rms(x_ref[...], gain_ref[...]).astype(BF16)


def _norm(l, x, gains, tm):
    rows = x.shape[0]
    return pl.pallas_call(
        _norm_kernel,
        grid=(rows // tm,),
        in_specs=[pl.BlockSpec((tm, D_MODEL), lambda i: (i, 0)),
                  pl.BlockSpec((None, 1, D_MODEL), lambda i: (l, 0, 0))],
        out_specs=pl.BlockSpec((tm, D_MODEL), lambda i: (i, 0)),
        out_shape=jax.ShapeDtypeStruct((rows, D_MODEL), BF16),
        compiler_params=_cparams(1),
        name="norm",
    )(x, gains)


def _inproj_a_kernel(h_ref, w_ref, o_ref):
    o_ref[...] = _dot(h_ref[...], w_ref[...]).astype(BF16)


def _inproj_a(l, h, w_a, tm):
    rows = h.shape[0]
    return pl.pallas_call(
        _inproj_a_kernel,
        grid=(rows // tm, A_TOT // A_TN),
        in_specs=[pl.BlockSpec((tm, D_MODEL), lambda i, j: (i, 0)),
                  pl.BlockSpec((None, D_MODEL, A_TN), lambda i, j: (l, 0, j))],
        out_specs=pl.BlockSpec((tm, A_TN), lambda i, j: (i, j)),
        out_shape=jax.ShapeDtypeStruct((rows, A_TOT), BF16),
        compiler_params=_cparams(2),
        name="inproj_a",
    )(h, w_a)


def _inproj_b_kernel(h_ref, w_ref, cs_ref, s1_ref, s2_ref, qa_ref, misc_ref, kt_ref, gt_ref):
    h = h_ref[...]

    def rope(seg):
        return (seg * cs_ref[...] + pltpu.roll(seg, 16, 1) * s1_ref[...]
                + pltpu.roll(seg, HD - 16, 1) * s2_ref[...])

    acc = _dot(h, w_ref[:, B_QA:B_KA])
    for c in range(N_HEADS):
        cs = slice(c * HD, (c + 1) * HD)
        qa_ref[:, cs] = (rope(acc[:, cs]) * (HD ** -0.5)).astype(BF16)

    acc = _dot(h, w_ref[:, B_KA:B_KM])
    for c in range(N_KV):
        cs = slice(M_KA + c * HD, M_KA + (c + 1) * HD)
        misc_ref[:, cs] = rope(acc[:, cs]).astype(BF16)
    misc_ref[:, M_VA:M_QM] = acc[:, M_VA:M_QM].astype(BF16)
    misc_ref[:, M_QM:M_TOT] = (acc[:, M_QM:M_TOT] * (DQK_M ** -0.5)).astype(BF16)

    acc = _dot(h, w_ref[:, B_KM:B_TOT])
    kt_ref[...] = acc[:, 0:NH_M * DQK_M].T.astype(BF16)
    gt_ref[...] = acc[:, NH_M * DQK_M:B_TOT - B_KM].T[0:2 * NH_M]


def _inproj_b(l, h, w_b, tabs, tm):
    rows = h.shape[0]
    tab_blocks = tabs[0].shape[0] // tm
    tab_spec = pl.BlockSpec((tm, HD), lambda i: (i % tab_blocks, 0))
    return pl.pallas_call(
        _inproj_b_kernel,
        grid=(rows // tm,),
        in_specs=[pl.BlockSpec((tm, D_MODEL), lambda i: (i, 0)),
                  pl.BlockSpec((None, D_MODEL, B_TOT), lambda i: (l, 0, 0)),
                  tab_spec, tab_spec, tab_spec],
        out_specs=[pl.BlockSpec((tm, D_MODEL), lambda i: (i, 0)),
                   pl.BlockSpec((tm, M_TOT), lambda i: (i, 0)),
                   pl.BlockSpec((NH_M * DQK_M, tm), lambda i: (0, i)),
                   pl.BlockSpec((2 * NH_M, tm), lambda i: (0, i))],
        out_shape=[jax.ShapeDtypeStruct((rows, D_MODEL), BF16),
                   jax.ShapeDtypeStruct((rows, M_TOT), BF16),
                   jax.ShapeDtypeStruct((NH_M * DQK_M, rows), BF16),
                   jax.ShapeDtypeStruct((2 * NH_M, rows), F32)],
        compiler_params=_cparams(1),
        name="inproj_b",
    )(h, w_b, *tabs)


def _mlstm_kernel(q_ref, kt_ref, v_ref, o_ref, gm_ref, g_ref, bias_ref, nmh_ref, ct0_ref, n0_ref, m0_ref,
                  y_ref, ctf_ref, nf_ref, mf_ref, ct_s, n_s, m_s, *, nc, valid_len):
    c = pl.program_id(1)
    L = CHUNK

    @pl.when(c == 0)
    def _():
        ct_s[...] = ct0_ref[...]
        n_s[...] = n0_ref[...]
        m_s[...] = m0_ref[...]

    row = lax.broadcasted_iota(jnp.int32, (L, L), 0)
    col = lax.broadcasted_iota(jnp.int32, (L, L), 1)
    tril = col <= row
    eye = col == row

    gates = g_ref[...] + bias_ref[...]
    ig_all = gates[0:NH_M]
    lf_all = _log_sigmoid(gates[NH_M:2 * NH_M])
    if valid_len < L:
        ok = lax.broadcasted_iota(jnp.int32, (NH_M, L), 1) < valid_len
        ig_all = jnp.where(ok, ig_all, -jnp.inf)
        lf_all = jnp.where(ok, lf_all, 0.0)

    for h in range(NH_M):
        ig = ig_all[h:h + 1]
        lf = lf_all[h:h + 1]
        m_prev = m_s[h, 0:1, 0:1]
        b_col = jnp.sum(jnp.where(tril, lf, 0.0), axis=1, keepdims=True)
        b_row = jnp.sum(jnp.where(eye, b_col, 0.0), axis=0, keepdims=True)
        a_col = b_col + m_prev
        dmat = jnp.where(tril, b_col - b_row + ig, -jnp.inf)
        m_col = jnp.maximum(a_col, jnp.max(dmat, axis=1, keepdims=True))
        w_inter = jnp.exp(a_col - m_col)
        wmat = jnp.exp(dmat - m_col)

        q = q_ref[:, h * DQK_M:(h + 1) * DQK_M]
        kt = kt_ref[h * DQK_M:(h + 1) * DQK_M, :]
        v = v_ref[:, h * DV_M:(h + 1) * DV_M]
        ct = ct_s[h]
        nm = n_s[h]

        s = _dot(q, kt) * wmat
        num = w_inter * _dot(q, ct.astype(BF16)) + _dot(s.astype(BF16), v)
        qn = _dot(q, nm.astype(BF16))[:, 0:1]
        den = w_inter * qn + jnp.sum(s, axis=1, keepdims=True)
        hh = num / jnp.maximum(jnp.abs(den), jnp.exp(-m_col))

        m_new = m_col[L - 1:L]
        decay = w_inter[L - 1:L]
        g_row = jnp.exp(b_col[L - 1:L] - b_row + ig - m_new)
        ktg = kt.astype(F32) * g_row
        ct_s[h] = decay * ct + _dot(ktg.astype(BF16), v)
        n_s[h] = decay * nm + jnp.sum(ktg, axis=1, keepdims=True)
        m_s[h] = jnp.broadcast_to(m_new, (8, 128))

        hs = slice(h * DV_M, (h + 1) * DV_M)
        hn = hh * lax.rsqrt(jnp.mean(hh * hh, axis=1, keepdims=True) + EPS) * nmh_ref[:, hs]
        y = jax.nn.sigmoid(o_ref[:, hs].astype(F32)) * hn
        y_ref[:, hs] = (jax.nn.sigmoid(gm_ref[:, hs].astype(F32)) * y).astype(BF16)

    @pl.when(c == nc - 1)
    def _():
        ctf_ref[0] = ct_s[...]
        nf_ref[0] = n_s[...]
        mf_ref[0] = m_s[...]


def _mlstm(misc, pa, kt, gt, bias8, nmh, ct0, n0, m0, nb, nc, row0, valid_len=CHUNK):
    r0 = row0 // CHUNK
    rb = lambda b, c: r0 + b * nc + c
    full3 = lambda b, c: (0, 0, 0)
    st4 = lambda b, c: (b, 0, 0, 0)
    return pl.pallas_call(
        functools.partial(_mlstm_kernel, nc=nc, valid_len=valid_len),
        grid=(nb, nc),
        in_specs=[pl.BlockSpec((CHUNK, NH_M * DQK_M), lambda b, c: (rb(b, c), M_QM // (NH_M * DQK_M))),
                  pl.BlockSpec((NH_M * DQK_M, CHUNK), lambda b, c: (0, rb(b, c))),
                  pl.BlockSpec((CHUNK, D_MODEL), lambda b, c: (rb(b, c), A_VM // D_MODEL)),
                  pl.BlockSpec((CHUNK, D_MODEL), lambda b, c: (rb(b, c), A_OM // D_MODEL)),
                  pl.BlockSpec((CHUNK, D_MODEL), lambda b, c: (rb(b, c), A_GM // D_MODEL)),
                  pl.BlockSpec((2 * NH_M, CHUNK), lambda b, c: (0, rb(b, c))),
                  pl.BlockSpec((2 * NH_M, CHUNK), lambda b, c: (0, 0)),
                  pl.BlockSpec((1, D_MODEL), lambda b, c: (0, 0)),
                  pl.BlockSpec((NH_M, DQK_M, DV_M), full3),
                  pl.BlockSpec((NH_M, DQK_M, 128), full3),
                  pl.BlockSpec((NH_M, 8, 128), full3)],
        out_specs=[pl.BlockSpec((CHUNK, D_MODEL), lambda b, c: (b * nc + c, 0)),
                   pl.BlockSpec((1, NH_M, DQK_M, DV_M), st4),
                   pl.BlockSpec((1, NH_M, DQK_M, 128), st4),
                   pl.BlockSpec((1, NH_M, 8, 128), st4)],
        out_shape=[jax.ShapeDtypeStruct((nb * nc * CHUNK, D_MODEL), BF16),
                   jax.ShapeDtypeStruct((nb, NH_M, DQK_M, DV_M), F32),
                   jax.ShapeDtypeStruct((nb, NH_M, DQK_M, 128), F32),
                   jax.ShapeDtypeStruct((nb, NH_M, 8, 128), F32)],
        scratch_shapes=[pltpu.VMEM((NH_M, DQK_M, DV_M), F32),
                        pltpu.VMEM((NH_M, DQK_M, 128), F32),
                        pltpu.VMEM((NH_M, 8, 128), F32)],
        compiler_params=_cparams(2),
        name="mlstm",
    )(misc, kt, pa, pa, pa, gt, bias8, nmh, ct0, n0, m0)


def _swa_kernel(sink_ref, q_ref, kc_ref, kp_ref, vc_ref, vp_ref, mk_ref, mv_ref, ga_ref, y_ref, *, has_meta):
    n = pl.program_id(1)
    L = WINDOW
    R = GQ * L
    row = lax.broadcasted_iota(jnp.int32, (R, L), 0)
    col = lax.broadcasted_iota(jnp.int32, (R, L), 1)
    tril = col <= (row & (L - 1))
    prev_ok = jnp.full((R, L), n, jnp.int32) > 0
    head_of_row = lax.shift_right_logical(lax.broadcasted_iota(jnp.int32, (R, 1), 0), L.bit_length() - 1)
    with_ones = lambda v: jnp.concatenate([v, jnp.ones((v.shape[0], HD), BF16)], axis=1)
    for kv in range(N_KV):
        ks = slice(kv * HD, (kv + 1) * HD)
        heads = [kv * GQ + g for g in range(GQ)]
        q = jnp.concatenate([q_ref[:, hd * HD:(hd + 1) * HD] for hd in heads], axis=0)
        sink = jnp.zeros((R, 1), F32)
        for g, hd in enumerate(heads):
            sink = jnp.where(head_of_row == g, sink_ref[hd], sink)
        band = jnp.where(tril, _dot_nt(q, kc_ref[:, ks]),
                         jnp.where(prev_ok, _dot_nt(q, kp_ref[:, ks]), -jnp.inf))
        mx = jnp.maximum(jnp.max(band, axis=1, keepdims=True), sink)
        if has_meta:
            s_m = _dot_nt(q, mk_ref[:, ks])
            mx = jnp.maximum(mx, jnp.max(s_m, axis=1, keepdims=True))
        p = jnp.exp(band - mx)
        oe = (_dot(jnp.where(tril, p, 0.0).astype(BF16), with_ones(vc_ref[:, ks]))
              + _dot(jnp.where(tril, 0.0, p).astype(BF16), with_ones(vp_ref[:, ks])))
        if has_meta:
            oe = oe + _dot(jnp.exp(s_m - mx).astype(BF16), with_ones(mv_ref[:, ks]))
        denom = oe[:, HD:HD + 1] + jnp.exp(sink - mx)
        o = oe[:, 0:HD] / denom
        for g, hd in enumerate(heads):
            hs = slice(hd * HD, (hd + 1) * HD)
            y = o[g * L:(g + 1) * L] * jax.nn.sigmoid(ga_ref[:, hs].astype(F32))
            y_ref[:, hs] = y.astype(BF16)


def _swa(qa, misc, pa, sinks, mk, mv, nb, nblk, row0, has_meta):
    r0 = row0 // WINDOW
    rb = lambda b, n: r0 + b * nblk + n
    rp = lambda b, n: r0 + b * nblk + jnp.maximum(n - 1, 0)
    kw = N_KV * HD
    return pl.pallas_call(
        functools.partial(_swa_kernel, has_meta=has_meta),
        grid=(nb, nblk),
        in_specs=[pl.BlockSpec(memory_space=pltpu.SMEM),
                  pl.BlockSpec((WINDOW, D_MODEL), lambda b, n: (rb(b, n), 0)),
                  pl.BlockSpec((WINDOW, kw), lambda b, n: (rb(b, n), M_KA // kw)),
                  pl.BlockSpec((WINDOW, kw), lambda b, n: (rp(b, n), M_KA // kw)),
                  pl.BlockSpec((WINDOW, kw), lambda b, n: (rb(b, n), M_VA // kw)),
                  pl.BlockSpec((WINDOW, kw), lambda b, n: (rp(b, n), M_VA // kw)),
                  pl.BlockSpec((N_META, kw), lambda b, n: (0, 0)),
                  pl.BlockSpec((N_META, kw), lambda b, n: (0, 0)),
                  pl.BlockSpec((WINDOW, D_MODEL), lambda b, n: (rb(b, n), A_GA // D_MODEL))],
        out_specs=pl.BlockSpec((WINDOW, D_MODEL), lambda b, n: (b * nblk + n, 0)),
        out_shape=jax.ShapeDtypeStruct((nb * nblk * WINDOW, D_MODEL), BF16),
        compiler_params=_cparams(2),
        name="swa",
    )(sinks, qa, misc, misc, misc, misc, mk, mv, pa)


def _tail_kernel(ym_ref, ya_ref, x_ref, wo_ref, gn_ref, wu_ref, wd_ref, gnext_ref, o_ref, *rest, nj, last_layer):
    hf_ref = rest[-1]
    j = pl.program_id(1)

    @pl.when(j == 0)
    def _():
        y = (ym_ref[...].astype(F32) + ya_ref[...].astype(F32)).astype(BF16)
        x1 = x_ref[...] + _dot(y, wo_ref[...])
        o_ref[...] = x1
        hf_ref[...] = _rms(x1, gn_ref[...]).astype(BF16)

    a = jnp.square(jnp.maximum(_dot(hf_ref[...], wu_ref[...]), 0.0)).astype(BF16)
    o_ref[...] += _dot(a, wd_ref[...])

    @pl.when(j == nj - 1)
    def _():
        if last_layer:
            o_ref[...] = _rms(o_ref[...], gnext_ref[...])
        else:
            rest[0][...] = _rms(o_ref[...], gnext_ref[...]).astype(BF16)


def _tail(l, ym, ya, x, w_out, g_mlp, w_up, w_down, g_next, tm, tf):
    rows = x.shape[0]
    nj = D_FF // tf
    last_layer = l == DEPTH - 1
    row_spec = pl.BlockSpec((tm, D_MODEL), lambda i, j: (i, 0))
    out_specs, out_shape = [row_spec], [jax.ShapeDtypeStruct((rows, D_MODEL), F32)]
    if not last_layer:
        out_specs.append(row_spec)
        out_shape.append(jax.ShapeDtypeStruct((rows, D_MODEL), BF16))
    return pl.pallas_call(
        functools.partial(_tail_kernel, nj=nj, last_layer=last_layer),
        grid=(rows // tm, nj),
        in_specs=[row_spec, row_spec, row_spec,
                  pl.BlockSpec((None, D_MODEL, D_MODEL), lambda i, j: (l, 0, 0)),
                  pl.BlockSpec((None, 1, D_MODEL), lambda i, j: (l, 0, 0)),
                  pl.BlockSpec((None, D_MODEL, tf), lambda i, j: (l, 0, j)),
                  pl.BlockSpec((None, tf, D_MODEL), lambda i, j: (l, j, 0)),
                  pl.BlockSpec((None, 1, D_MODEL), lambda i, j: (l + 1, 0, 0))],
        out_specs=out_specs,
        out_shape=out_shape,
        scratch_shapes=[pltpu.VMEM((tm, D_MODEL), BF16)],
        compiler_params=_cparams(2),
        name="tail",
    )(ym, ya, x, w_out, g_mlp, w_up, w_down, g_next)


def _mlstm_step_kernel(q_ref, k_ref, v_ref, o_ref, gm_ref, ig_ref, fg_ref, m_ref, n_ref, c_ref,
                       kall_ref, vt_ref, bi_ref, bf_ref, nmh_ref, *rest):
    y_ref, co_ref, no_ref, mo_ref = rest[-4:]
    i = pl.program_id(0)
    nb = DEC_BLK
    row8 = lax.broadcasted_iota(jnp.int32, (nb, DV_M), 0)
    rowb = lax.broadcasted_iota(jnp.int32, (kall_ref.shape[0], DQK_M), 0)
    for h in range(NH_M):
        ds = slice(h * DQK_M, (h + 1) * DQK_M)
        vs = slice(h * DV_M, (h + 1) * DV_M)
        q, k, v, nrm = q_ref[:, ds], k_ref[:, ds], v_ref[:, vs], n_ref[:, ds]
        ig = ig_ref[:, ds] + bi_ref[:, ds]
        lf = _log_sigmoid(fg_ref[:, ds] + bf_ref[:, ds])
        m_prev = m_ref[:, ds]
        m_t = jnp.maximum(lf + m_prev, ig)
        w_inter = jnp.exp(lf + m_prev - m_t)
        w_in = jnp.exp(ig - m_t)
        s = jnp.sum(q * k, axis=1, keepdims=True) * w_in
        qb = q.astype(BF16)
        kall = kall_ref[:, ds]
        vt = vt_ref[vs, :]
        cq = jnp.zeros((nb, DV_M), F32)
        for bb in range(nb):
            c_old = c_ref[bb, h]
            cq = jnp.where(row8 == bb, _dot_nt(qb, c_old.astype(BF16)), cq)
            outer = _dot(vt, jnp.where(rowb == i * nb + bb, kall, jnp.zeros_like(kall)))
            co_ref[bb, h] = w_inter[bb:bb + 1, :] * c_old + w_in[bb:bb + 1, :] * outer
        num = w_inter[:, 0:1] * cq + s[:, 0:1] * v
        den = w_inter * jnp.sum(nrm * q, axis=1, keepdims=True) + s
        hh = num / jnp.maximum(jnp.abs(den), jnp.exp(-m_t))[:, 0:1]
        no_ref[:, ds] = w_inter * nrm + w_in * k
        mo_ref[:, ds] = m_t
        hn = hh * lax.rsqrt(jnp.mean(hh * hh, axis=1, keepdims=True) + EPS) * nmh_ref[:, vs]
        y_ref[:, vs] = jax.nn.sigmoid(gm_ref[:, vs]) * (jax.nn.sigmoid(o_ref[:, vs]) * hn)


def _mlstm_step(l, q, k, v, o, gm, ig, fg, m, n, c_all, c_out_prev, kall, vt, bi, bf, nmh):
    nbatch = q.shape[0]
    qd = NH_M * DQK_M
    rowq = pl.BlockSpec((DEC_BLK, qd), lambda i: (i, 0))
    rowv = pl.BlockSpec((DEC_BLK, D_MODEL), lambda i: (i, 0))
    cspec = pl.BlockSpec((None, DEC_BLK, NH_M, DV_M, DQK_M), lambda i: (l, i, 0, 0, 0))
    const = lambda shape: pl.BlockSpec(shape, lambda i: (0, 0))
    in_specs = [rowq, rowq, rowv, rowv, rowv, rowq, rowq, rowq, rowq, cspec,
                const((nbatch, qd)), const((D_MODEL, nbatch)), const((1, qd)), const((1, qd)),
                const((1, D_MODEL))]
    args = [q, k, v, o, gm, ig, fg, m, n, c_all, kall, vt, bi, bf, nmh]
    aliases = {}
    if c_out_prev is not None:
        aliases = {len(args): 1}
        in_specs.append(pl.BlockSpec(memory_space=pl.ANY))
        args.append(c_out_prev)
    return pl.pallas_call(
        _mlstm_step_kernel,
        grid=(nbatch // DEC_BLK,),
        in_specs=in_specs,
        out_specs=[rowv, cspec, rowq, rowq],
        out_shape=[jax.ShapeDtypeStruct((nbatch, D_MODEL), F32),
                   jax.ShapeDtypeStruct(c_all.shape, F32),
                   jax.ShapeDtypeStruct((nbatch, qd), F32),
                   jax.ShapeDtypeStruct((nbatch, qd), F32)],
        input_output_aliases=aliases,
        compiler_params=_cparams(1),
        name="mlstm_step",
    )(*args)


def _swa_step_kernel(q_ref, ga_ref, kn_ref, vn_ref, mk_ref, mv_ref, wk_ref, wv_ref, sink_ref, *rest):
    y_ref, ok_ref, ov_ref = rest[-3:]
    W = WINDOW
    hrow = lax.broadcasted_iota(jnp.int32, (N_HEADS, HD), 0)
    first = hrow < GQ
    wcol = lax.broadcasted_iota(jnp.int32, (N_HEADS, W), 1)
    wrow = lax.broadcasted_iota(jnp.int32, (N_KV * W, HD), 0)
    sink = sink_ref[...][:, 0:1]

    def per_kv(fn):
        a0, a1 = fn(0), fn(1)
        return jnp.where(first[:, 0:a0.shape[1]], a0, a1)

    for bb in range(DEC_BLK):
        q = q_ref[bb]
        qb = q.astype(BF16)
        rows_of = lambda ref, kv, n: ref[bb, pl.ds(kv, n, stride=N_KV), :].astype(BF16)
        kn, vn = kn_ref[bb:bb + 1, :], vn_ref[bb:bb + 1, :]
        kvs = lambda a, kv: a[:, kv * HD:(kv + 1) * HD]
        s_w = jnp.where(wcol >= 1, per_kv(lambda kv: _dot_nt(qb, rows_of(wk_ref, kv, W))), -jnp.inf)
        s_m = per_kv(lambda kv: _dot_nt(qb, rows_of(mk_ref, kv, N_META)))
        kn8 = per_kv(lambda kv: jnp.broadcast_to(kvs(kn, kv), (N_HEADS, HD)))
        vn8 = per_kv(lambda kv: jnp.broadcast_to(kvs(vn, kv), (N_HEADS, HD)))
        s_n = jnp.sum(q * kn8, axis=1, keepdims=True)
        mx = jnp.maximum(jnp.maximum(jnp.max(s_w, axis=1, keepdims=True), jnp.max(s_m, axis=1, keepdims=True)),
                         jnp.maximum(s_n, sink))
        p_w, p_m, p_n = jnp.exp(s_w - mx), jnp.exp(s_m - mx), jnp.exp(s_n - mx)
        denom = (jnp.sum(p_w, axis=1, keepdims=True) + jnp.sum(p_m, axis=1, keepdims=True) + p_n
                 + jnp.exp(sink - mx))
        o = (per_kv(lambda kv: _dot(p_w.astype(BF16), rows_of(wv_ref, kv, W)))
             + per_kv(lambda kv: _dot(p_m.astype(BF16), rows_of(mv_ref, kv, N_META)))
             + p_n * vn8)
        y_ref[bb] = (o / denom) * jax.nn.sigmoid(ga_ref[bb])
        for src_ref, new, dst_ref in ((wk_ref, kn, ok_ref), (wv_ref, vn, ov_ref)):
            shifted = pltpu.roll(src_ref[bb], N_KV * (W - 1), 0)
            for kv in range(N_KV):
                shifted = jnp.where(wrow == N_KV * (W - 1) + kv, kvs(new, kv), shifted)
            dst_ref[bb] = shifted


def _swa_step(l, q, ga, kn, vn, mk_all, mv_all, wk_all, wv_all, ok_prev, ov_prev, sink8):
    nbatch = q.shape[0]
    kw = N_KV * HD
    hspec = pl.BlockSpec((DEC_BLK, N_HEADS, HD), lambda i: (i, 0, 0))
    nspec = pl.BlockSpec((DEC_BLK, kw), lambda i: (i, 0))
    mspec = pl.BlockSpec((None, DEC_BLK, N_KV * N_META, HD), lambda i: (l, i, 0, 0))
    wspec = pl.BlockSpec((None, DEC_BLK, N_KV * WINDOW, HD), lambda i: (l, i, 0, 0))
    in_specs = [hspec, hspec, nspec, nspec, mspec, mspec, wspec, wspec,
                pl.BlockSpec((N_HEADS, HD), lambda i: (0, 0))]
    args = [q, ga, kn, vn, mk_all, mv_all, wk_all, wv_all, sink8]
    aliases = {}
    if ok_prev is not None:
        aliases = {len(args): 1, len(args) + 1: 2}
        in_specs += [pl.BlockSpec(memory_space=pl.ANY)] * 2
        args += [ok_prev, ov_prev]
    return pl.pallas_call(
        _swa_step_kernel,
        grid=(nbatch // DEC_BLK,),
        in_specs=in_specs,
        out_specs=[hspec, wspec, wspec],
        out_shape=[jax.ShapeDtypeStruct((nbatch, N_HEADS, HD), F32),
                   jax.ShapeDtypeStruct(wk_all.shape, F32),
                   jax.ShapeDtypeStruct(wv_all.shape, F32)],
        input_output_aliases=aliases,
        compiler_params=_cparams(1),
        name="swa_step",
    )(*args)


def _rope_tables(pos):
    half = ROT_DIM // 2
    inv = jnp.float32(ROPE_THETA) ** (-jnp.arange(half, dtype=F32) * 2.0 / ROT_DIM)
    ang = pos.astype(F32)[:, None] * inv[None, :]
    cos, sin = jnp.cos(ang), jnp.sin(ang)
    t = pos.shape[0]
    cs = jnp.concatenate([cos, cos, jnp.ones((t, HD - ROT_DIM), F32)], axis=1)
    s1 = jnp.concatenate([jnp.zeros((t, half), F32), sin, jnp.zeros((t, HD - ROT_DIM), F32)], axis=1)
    s2 = jnp.concatenate([-sin, jnp.zeros((t, HD - half), F32)], axis=1)
    return cs, s1, s2


def _pack_in_weights(w_in):
    edges = np.concatenate([[0], np.cumsum(IN_WIDTHS)])
    seg = [w_in[:, :, edges[i]:edges[i + 1]] for i in range(len(IN_WIDTHS))]
    q_m, k_m, v_m, ig, fg, o_m, q_a, k_a, v_a, g_m, g_a = seg
    w_a = jnp.concatenate([v_m, o_m, g_m, g_a], axis=2).astype(BF16)
    pad = jnp.zeros(w_in.shape[:2] + (B_TOT - B_G - 2 * NH_M,), w_in.dtype)
    w_b = jnp.concatenate([q_a, k_a, v_a, q_m, k_m, ig, fg, pad], axis=2).astype(BF16)
    return w_a, w_b


def _rep(a, width=128):
    return jnp.repeat(a, width, axis=1)


def kernel(x_prompt, x_sample, state_C, state_n, state_m, cache_meta_k, cache_meta_v, cache_win_k, cache_win_v,
           meta_tokens, norm_mix, w_in, b_igate, b_fgate, norm_mh, attn_sinks, w_out, norm_mlp, w_up, w_down,
           norm_final):
    B, S, _ = x_prompt.shape
    DB = x_sample.shape[0]
    W = cache_win_k.shape[2]
    assert (S % CHUNK, W, x_sample.shape[1], DB % DEC_BLK) == (0, WINDOW, 1, 0)
    assert (state_C.dtype, cache_win_k.dtype, cache_win_v.dtype) == (F32, F32, F32)
    nchunk = S // CHUNK
    kw = N_KV * HD
    big_tm = 1024

    w_a, w_b = _pack_in_weights(w_in)
    w_out_b, w_up_b, w_down_b = w_out.astype(BF16), w_up.astype(BF16), w_down.astype(BF16)
    g_mix = norm_mix.astype(F32).reshape(DEPTH, 1, D_MODEL)
    g_mlp = norm_mlp.astype(F32).reshape(DEPTH, 1, D_MODEL)
    g_next = jnp.concatenate([g_mix, norm_final.astype(F32).reshape(1, 1, D_MODEL)], axis=0)
    rows_view = lambda a: a.reshape(DEPTH, DB, -1, HD)
    mk_all, mv_all = rows_view(cache_meta_k), rows_view(cache_meta_v)
    wk_all, wv_all = rows_view(cache_win_k), rows_view(cache_win_v)

    tabs_p = _rope_tables(N_META + jnp.arange(S, dtype=jnp.int32))
    pos_small = jnp.concatenate([jnp.full((DB,), PAST_LEN, jnp.int32), jnp.arange(N_META, dtype=jnp.int32),
                                 jnp.zeros((SMALL_ROWS - DB - N_META,), jnp.int32)])
    tabs_s = _rope_tables(pos_small)

    xp = x_prompt.reshape(B * S, D_MODEL)
    xs = jnp.concatenate([x_sample.reshape(DB, D_MODEL), meta_tokens.astype(F32),
                          jnp.zeros((SMALL_ROWS - DB - N_META, D_MODEL), F32)], axis=0)
    zero_state = (jnp.zeros((NH_M, DQK_M, DV_M), F32), jnp.zeros((NH_M, DQK_M, 128), F32),
                  jnp.zeros((NH_M, 8, 128), F32))
    no_meta = jnp.zeros((N_META, kw), BF16)

    outs = {k: [] for k in ("pC", "pn", "pm", "pmk", "pmv", "pwk", "pwv", "sn", "sm")}
    sC = swk = swv = None
    hs_ = _norm(0, xs, g_mix, SMALL_ROWS)
    hp = _norm(0, xp, g_mix, big_tm)
    for l in range(DEPTH):
        nmh = norm_mh[l].reshape(1, D_MODEL)
        bias8 = jnp.broadcast_to(jnp.concatenate([b_igate[l], b_fgate[l]])[:, None], (2 * NH_M, CHUNK)).astype(F32)
        sinks = attn_sinks[l].astype(F32)

        pa_s = _inproj_a(l, hs_, w_a, SMALL_ROWS)
        qa_s, misc_s, kt_s, gt_s = _inproj_b(l, hs_, w_b, tabs_s, SMALL_ROWS)

        ym_meta, ct_m, n_m, m_m = _mlstm(misc_s, pa_s, kt_s, gt_s, bias8, nmh, *zero_state, nb=1, nc=1,
                                         row0=META_ROW0, valid_len=N_META)
        ya_meta = _swa(qa_s, misc_s, pa_s, sinks, no_meta, no_meta, nb=1, nblk=1, row0=META_ROW0, has_meta=False)
        mk = misc_s[META_ROW0:META_ROW0 + N_META, M_KA:M_KA + kw]
        mv = misc_s[META_ROW0:META_ROW0 + N_META, M_VA:M_VA + kw]

        f32s = lambda a, lo, width: a[:DB, lo:lo + width].astype(F32)
        gcol = gt_s[:, :DB].T
        k_s = kt_s[:, :DB].T
        ys_m, sC, sn, sm = _mlstm_step(
            l, f32s(misc_s, M_QM, NH_M * DQK_M), k_s.astype(F32), f32s(pa_s, A_VM, D_MODEL),
            f32s(pa_s, A_OM, D_MODEL), f32s(pa_s, A_GM, D_MODEL),
            _rep(gcol[:, :NH_M]), _rep(gcol[:, NH_M:]), _rep(state_m[l].astype(F32)),
            state_n[l].astype(F32).reshape(DB, NH_M * DQK_M), state_C, sC,
            k_s, pa_s[:DB, A_VM:A_VM + D_MODEL].T,
            _rep(b_igate[l].astype(F32)[None]), _rep(b_fgate[l].astype(F32)[None]), nmh)
        ys_a, swk, swv = _swa_step(
            l, qa_s[:DB].astype(F32).reshape(DB, N_HEADS, HD), f32s(pa_s, A_GA, D_MODEL).reshape(DB, N_HEADS, HD),
            f32s(misc_s, M_KA, kw), f32s(misc_s, M_VA, kw), mk_all, mv_all, wk_all, wv_all, swk, swv,
            jnp.broadcast_to(sinks[:, None], (N_HEADS, HD)))
        ym_s = jnp.concatenate([ys_m.astype(BF16), ym_meta], axis=0)
        ya_s = jnp.concatenate([ys_a.reshape(DB, D_MODEL).astype(BF16), ya_meta], axis=0)
        res = _tail(l, ym_s, ya_s, xs, w_out_b, g_mlp, w_up_b, w_down_b, g_next, tm=SMALL_ROWS, tf=512)
        xs, hs_ = res[0], res[-1]

        pa = _inproj_a(l, hp, w_a, big_tm)
        qa, misc, kt, gt = _inproj_b(l, hp, w_b, tabs_p, big_tm)
        ym, ct_f, n_f, m_f = _mlstm(misc, pa, kt, gt, bias8, nmh, ct_m[0], n_m[0], m_m[0], nb=B, nc=nchunk, row0=0)
        ya = _swa(qa, misc, pa, sinks, mk, mv, nb=B, nblk=nchunk, row0=0, has_meta=True)
        res = _tail(l, ym, ya, xp, w_out_b, g_mlp, w_up_b, w_down_b, g_next, tm=big_tm, tf=TAIL_TF)
        xp, hp = res[0], res[-1]

        outs["pC"].append(jnp.swapaxes(ct_f, 2, 3).astype(state_C.dtype))
        outs["pn"].append(n_f[:, :, :, 0].astype(state_n.dtype))
        outs["pm"].append(m_f[:, :, 0, 0].astype(state_m.dtype))
        outs["pmk"].append(jnp.broadcast_to(mk.astype(F32).reshape(1, N_META, N_KV, HD), (B, N_META, N_KV, HD)))
        outs["pmv"].append(jnp.broadcast_to(mv.astype(F32).reshape(1, N_META, N_KV, HD), (B, N_META, N_KV, HD)))
        misc3 = misc.reshape(B, S, M_TOT)
        outs["pwk"].append(misc3[:, S - WINDOW:, M_KA:M_KA + kw].astype(F32).reshape(B, WINDOW, N_KV, HD))
        outs["pwv"].append(misc3[:, S - WINDOW:, M_VA:M_VA + kw].astype(F32).reshape(B, WINDOW, N_KV, HD))
        outs["sn"].append(sn.reshape(DB, NH_M, DQK_M).astype(state_n.dtype))
        outs["sm"].append(sm.reshape(DB, NH_M, 128)[:, :, 0].astype(state_m.dtype))

    y_prompt = xp.reshape(B, S, D_MODEL)
    y_sample = xs[:DB].reshape(DB, 1, D_MODEL)
    st = lambda k: jnp.stack(outs[k])
    return (y_prompt, y_sample, st("pC"), st("pn"), st("pm"), st("pmk"), st("pmv"), st("pwk"), st("pwv"),
            sC, st("sn"), st("sm"), swk.reshape(cache_win_k.shape), swv.reshape(cache_win_v.shape))
```

```python
import functools

import jax
import jax.numpy as jnp
import numpy as np
from jax import lax
from jax.experimental import pallas as pl
from jax.experimental.pallas import tpu as pltpu

F32 = jnp.float32
BF16 = jnp.bfloat16

D_MODEL = 1024
DEPTH = 4
PAST_LEN = 8192
N_META = 16
NH_M = 4
DV_M = 256
DQK_M = 128
CHUNK = 128
N_HEADS = 8
N_KV = 2
GQ = N_HEADS // N_KV
HD = 128
WINDOW = 128
ROT_DIM = 32
ROPE_THETA = 500000.0
D_FF = 4 * D_MODEL
EPS = 1e-6
IN_WIDTHS = (NH_M * DQK_M, NH_M * DQK_M, NH_M * DV_M, NH_M, NH_M, NH_M * DV_M,
             N_HEADS * HD, N_KV * HD, N_KV * HD, D_MODEL, D_MODEL)

A_VM, A_OM, A_GM, A_GA, A_TOT = 0, 1024, 2048, 3072, 4096
A_TN = 2048
B_QA, B_KA, B_VA, B_QM, B_KM, B_G, B_TOT = 0, 1024, 1280, 1536, 2048, 2560, 2688
M_KA, M_VA, M_QM, M_TOT = 0, 256, 512, 1024
TAIL_TF = 1024
MIX_SUB = 1
SMALL_ROWS = 256
META_ROW0 = 128
DEC_BLK = 8
VMEM_LIMIT = 56 * 1024 * 1024


def _cparams(n_axes):
    return pltpu.CompilerParams(dimension_semantics=("arbitrary",) * n_axes, vmem_limit_bytes=VMEM_LIMIT)


def _rms(xf, gain):
    ms = jnp.mean(xf * xf, axis=-1, keepdims=True)
    return xf * lax.rsqrt(ms + EPS) * gain


def _dot(a, b):
    return jnp.dot(a, b, preferred_element_type=F32)


def _dot_nt(a, b):
    return lax.dot_general(a, b, (((1,), (1,)), ((), ())), preferred_element_type=F32)


def _log_sigmoid(x):
    return jnp.minimum(x, 0.0) - jnp.log1p(jnp.exp(-jnp.abs(x)))


def _sigmoid_bf16(x):
    half = jnp.asarray(0.5, x.dtype)
    return jnp.tanh(x * half) * half + half


def _norm_kernel(x_ref, gain_ref, o_ref):
    o_ref[...] = _rms(x_ref[...], gain_ref[...]).astype(BF16)


def _norm(l, x, gains, tm):
    rows = x.shape[0]
    return pl.pallas_call(
        _norm_kernel,
        grid=(rows // tm,),
        in_specs=[pl.BlockSpec((tm, D_MODEL), lambda i: (i, 0)),
                  pl.BlockSpec((None, 1, D_MODEL), lambda i: (l, 0, 0))],
        out_specs=pl.BlockSpec((tm, D_MODEL), lambda i: (i, 0)),
        out_shape=jax.ShapeDtypeStruct((rows, D_MODEL), BF16),
        compiler_params=_cparams(1),
        name="norm",
    )(x, gains)


def _inproj_a_kernel(h_ref, w_ref, o_ref):
    o_ref[...] = _dot(h_ref[...], w_ref[...]).astype(BF16)


def _inproj_a(l, h, w_a, tm):
    rows = h.shape[0]
    return pl.pallas_call(
        _inproj_a_kernel,
        grid=(rows // tm, A_TOT // A_TN),
        in_specs=[pl.BlockSpec((tm, D_MODEL), lambda i, j: (i, 0)),
                  pl.BlockSpec((None, D_MODEL, A_TN), lambda i, j: (l, 0, j))],
        out_specs=pl.BlockSpec((tm, A_TN), lambda i, j: (i, j)),
        out_shape=jax.ShapeDtypeStruct((rows, A_TOT), BF16),
        compiler_params=_cparams(2),
        name="inproj_a",
    )(h, w_a)


def _inproj_b_kernel(h_ref, w_ref, cs_ref, s1_ref, s2_ref, qa_ref, misc_ref, kt_ref, gt_ref):
    h = h_ref[...]

    def rope(seg):
        return (seg * cs_ref[...] + pltpu.roll(seg, 16, 1) * s1_ref[...]
                + pltpu.roll(seg, HD - 16, 1) * s2_ref[...])

    acc = _dot(h, w_ref[:, B_QA:B_KA])
    for c in range(N_HEADS):
        cs = slice(c * HD, (c + 1) * HD)
        qa_ref[:, cs] = (rope(acc[:, cs]) * (HD ** -0.5)).astype(BF16)

    acc = _dot(h, w_ref[:, B_KA:B_KM])
    for c in range(N_KV):
        cs = slice(M_KA + c * HD, M_KA + (c + 1) * HD)
        misc_ref[:, cs] = rope(acc[:, cs]).astype(BF16)
    misc_ref[:, M_VA:M_QM] = acc[:, M_VA:M_QM].astype(BF16)
    misc_ref[:, M_QM:M_TOT] = (acc[:, M_QM:M_TOT] * (DQK_M ** -0.5)).astype(BF16)

    acc = _dot(h, w_ref[:, B_KM:B_TOT])
    kt_ref[...] = acc[:, 0:NH_M * DQK_M].T.astype(BF16)
    gt_ref[...] = acc[:, NH_M * DQK_M:B_TOT - B_KM].T[0:2 * NH_M]


def _inproj_b(l, h, w_b, tabs, tm):
    rows = h.shape[0]
    tab_blocks = tabs[0].shape[0] // tm
    tab_spec = pl.BlockSpec((tm, HD), lambda i: (i % tab_blocks, 0))
    return pl.pallas_call(
        _inproj_b_kernel,
        grid=(rows // tm,),
        in_specs=[pl.BlockSpec((tm, D_MODEL), lambda i: (i, 0)),
                  pl.BlockSpec((None, D_MODEL, B_TOT), lambda i: (l, 0, 0)),
                  tab_spec, tab_spec, tab_spec],
        out_specs=[pl.BlockSpec((tm, D_MODEL), lambda i: (i, 0)),
                   pl.BlockSpec((tm, M_TOT), lambda i: (i, 0)),
                   pl.BlockSpec((NH_M * DQK_M, tm), lambda i: (0, i)),
                   pl.BlockSpec((2 * NH_M, tm), lambda i: (0, i))],
        out_shape=[jax.ShapeDtypeStruct((rows, D_MODEL), BF16),
                   jax.ShapeDtypeStruct((rows, M_TOT), BF16),
                   jax.ShapeDtypeStruct((NH_M * DQK_M, rows), BF16),
                   jax.ShapeDtypeStruct((2 * NH_M, rows), F32)],
        compiler_params=_cparams(1),
        name="inproj_b",
    )(h, w_b, *tabs)


def _mix_kernel(sink_ref, misc_ref, miscp_ref, pa_ref, qa_ref, kt_ref, g_ref, bias_ref, nmh_ref,
                ct0_ref, n0_ref, m0_ref, mk_ref, mv_ref,
                y_ref, ctf_ref, nf_ref, mf_ref, ct_s, n_s, m_s, ym_s, *, nc, sub, valid_len, has_meta):
    c = pl.program_id(1)
    L = CHUNK

    @pl.when(c == 0)
    def _():
        ct_s[...] = ct0_ref[...]
        n_s[...] = n0_ref[...]
        m_s[...] = m0_ref[...]

    row = lax.broadcasted_iota(jnp.int32, (L, L), 0)
    col = lax.broadcasted_iota(jnp.int32, (L, L), 1)
    tril = col <= row
    eye = col == row

    gates = g_ref[...] + bias_ref[:, 0:1]
    ig_sub = gates[0:NH_M]
    lf_sub = _log_sigmoid(gates[NH_M:2 * NH_M])
    if valid_len < L:
        ok = lax.broadcasted_iota(jnp.int32, (NH_M, sub * L), 1) < valid_len
        ig_sub = jnp.where(ok, ig_sub, -jnp.inf)
        lf_sub = jnp.where(ok, lf_sub, 0.0)

    for sb in range(sub):
        rs = slice(sb * L, (sb + 1) * L)
        for h in range(NH_M):
            ig = ig_sub[h:h + 1, rs]
            lf = lf_sub[h:h + 1, rs]
            m_prev = m_s[h, 0:1, 0:1]
            b_col = jnp.sum(jnp.where(tril, lf, 0.0), axis=1, keepdims=True)
            b_row = jnp.sum(jnp.where(eye, b_col, 0.0), axis=0, keepdims=True)
            a_col = b_col + m_prev
            dmat = jnp.where(tril, b_col - b_row + ig, -jnp.inf)
            m_col = jnp.maximum(a_col, jnp.max(dmat, axis=1, keepdims=True))
            w_inter = jnp.exp(a_col - m_col)
            wmat = jnp.exp(dmat - m_col)

            q = misc_ref[rs, M_QM + h * DQK_M:M_QM + (h + 1) * DQK_M]
            kt = kt_ref[h * DQK_M:(h + 1) * DQK_M, rs]
            v = pa_ref[rs, A_VM + h * DV_M:A_VM + (h + 1) * DV_M]
            ct = ct_s[h]
            nm = n_s[h]

            s = _dot(q, kt) * wmat
            num = w_inter * _dot(q, ct.astype(BF16)) + _dot(s.astype(BF16), v)
            qn = _dot(q, nm.astype(BF16))[:, 0:1]
            den = w_inter * qn + jnp.sum(s, axis=1, keepdims=True)
            hh = num / jnp.maximum(jnp.abs(den), jnp.exp(-m_col))

            m_new = m_col[L - 1:L]
            decay = w_inter[L - 1:L]
            g_row = jnp.exp(b_col[L - 1:L] - b_row + ig - m_new)
            ktg = kt.astype(F32) * g_row
            ct_s[h] = decay * ct + _dot(ktg.astype(BF16), v)
            n_s[h] = decay * nm + jnp.sum(ktg, axis=1, keepdims=True)
            m_s[h] = jnp.broadcast_to(m_new, (8, 128))

            hs = slice(h * DV_M, (h + 1) * DV_M)
            hn = hh * lax.rsqrt(jnp.mean(hh * hh, axis=1, keepdims=True) + EPS) * nmh_ref[:, hs]
            gate = (_sigmoid_bf16(pa_ref[rs, A_GM + h * DV_M:A_GM + (h + 1) * DV_M])
                    * _sigmoid_bf16(pa_ref[rs, A_OM + h * DV_M:A_OM + (h + 1) * DV_M]))
            ym_s[rs, hs] = gate.astype(F32) * hn

        _swa_part(c, sb, sink_ref, qa_ref, misc_ref, miscp_ref, pa_ref, mk_ref, mv_ref, ym_s, y_ref, has_meta)

    @pl.when(c == nc - 1)
    def _():
        ctf_ref[0] = ct_s[...]
        nf_ref[0] = n_s[...]
        mf_ref[0] = m_s[...]


def _mix(sinks, qa, misc, pa, kt, gt, bias8, nmh, ct0, n0, m0, mk, mv, nb, nblk, sub, row0, valid_len, has_meta):
    nc = nblk // sub
    rows = sub * CHUNK
    r0 = row0 // rows
    rb = lambda b, c: r0 + b * nc + c
    rp = lambda b, c: (r0 + b * nc + c) * sub - jnp.minimum(c, 1)
    kw = N_KV * HD
    full3 = lambda b, c: (0, 0, 0)
    st4 = lambda b, c: (b, 0, 0, 0)
    return pl.pallas_call(
        functools.partial(_mix_kernel, nc=nc, sub=sub, valid_len=valid_len, has_meta=has_meta),
        grid=(nb, nc),
        in_specs=[pl.BlockSpec(memory_space=pltpu.SMEM),
                  pl.BlockSpec((rows, M_TOT), lambda b, c: (rb(b, c), 0)),
                  pl.BlockSpec((CHUNK, 2 * kw), lambda b, c: (rp(b, c), 0)),
                  pl.BlockSpec((rows, A_TOT), lambda b, c: (rb(b, c), 0)),
                  pl.BlockSpec((rows, D_MODEL), lambda b, c: (rb(b, c), 0)),
                  pl.BlockSpec((NH_M * DQK_M, rows), lambda b, c: (0, rb(b, c))),
                  pl.BlockSpec((2 * NH_M, rows), lambda b, c: (0, rb(b, c))),
                  pl.BlockSpec((2 * NH_M, CHUNK), lambda b, c: (0, 0)),
                  pl.BlockSpec((1, D_MODEL), lambda b, c: (0, 0)),
                  pl.BlockSpec((NH_M, DQK_M, DV_M), full3),
                  pl.BlockSpec((NH_M, DQK_M, 128), full3),
                  pl.BlockSpec((NH_M, 8, 128), full3),
                  pl.BlockSpec((N_META, kw), lambda b, c: (0, 0)),
                  pl.BlockSpec((N_META, kw), lambda b, c: (0, 0))],
        out_specs=[pl.BlockSpec((rows, D_MODEL), lambda b, c: (b * nc + c, 0)),
                   pl.BlockSpec((1, NH_M, DQK_M, DV_M), st4),
                   pl.BlockSpec((1, NH_M, DQK_M, 128), st4),
                   pl.BlockSpec((1, NH_M, 8, 128), st4)],
        out_shape=[jax.ShapeDtypeStruct((nb * nc * rows, D_MODEL), BF16),
                   jax.ShapeDtypeStruct((nb, NH_M, DQK_M, DV_M), F32),
                   jax.ShapeDtypeStruct((nb, NH_M, DQK_M, 128), F32),
                   jax.ShapeDtypeStruct((nb, NH_M, 8, 128), F32)],
        scratch_shapes=[pltpu.VMEM((NH_M, DQK_M, DV_M), F32),
                        pltpu.VMEM((NH_M, DQK_M, 128), F32),
                        pltpu.VMEM((NH_M, 8, 128), F32),
                        pltpu.VMEM((rows, D_MODEL), F32)],
        compiler_params=_cparams(2),
        name="mix",
    )(sinks, misc, misc, pa, qa, kt, gt, bias8, nmh, ct0, n0, m0, mk, mv)


def _swa_part(c, sb, sink_ref, q_ref, misc_ref, miscp_ref, pa_ref, mk_ref, mv_ref, ym_s, y_ref, has_meta):
    L = WINDOW
    R = GQ * L
    rs = slice(sb * L, (sb + 1) * L)
    row = lax.broadcasted_iota(jnp.int32, (R, L), 0)
    col = lax.broadcasted_iota(jnp.int32, (R, L), 1)
    tril = col <= (row & (L - 1))
    head_of_row = lax.shift_right_logical(lax.broadcasted_iota(jnp.int32, (R, 1), 0), L.bit_length() - 1)
    with_ones = lambda v: jnp.concatenate([v, jnp.ones((v.shape[0], HD), BF16)], axis=1)
    if sb == 0:
        prev = lambda cols: miscp_ref[:, cols]
        prev_ok = jnp.full((R, L), c, jnp.int32) > 0
    else:
        prev = lambda cols: misc_ref[(sb - 1) * L:sb * L, cols]
        prev_ok = None
    for kv in range(N_KV):
        ka = slice(M_KA + kv * HD, M_KA + (kv + 1) * HD)
        va = slice(M_VA + kv * HD, M_VA + (kv + 1) * HD)
        ks = slice(kv * HD, (kv + 1) * HD)
        heads = [kv * GQ + g for g in range(GQ)]
        q = jnp.concatenate([q_ref[rs, hd * HD:(hd + 1) * HD] for hd in heads], axis=0)
        sink = jnp.zeros((R, 1), F32)
        for g, hd in enumerate(heads):
            sink = jnp.where(head_of_row == g, sink_ref[hd], sink)
        s_prev = _dot_nt(q, prev(ka))
        if prev_ok is not None:
            s_prev = jnp.where(prev_ok, s_prev, -jnp.inf)
        band = jnp.where(tril, _dot_nt(q, misc_ref[rs, ka]), s_prev)
        mx = jnp.maximum(jnp.max(band, axis=1, keepdims=True), sink)
        if has_meta:
            s_m = _dot_nt(q, mk_ref[:, ks])
            mx = jnp.maximum(mx, jnp.max(s_m, axis=1, keepdims=True))
        p = jnp.exp(band - mx)
        oe = (_dot(jnp.where(tril, p, 0.0).astype(BF16), with_ones(misc_ref[rs, va]))
              + _dot(jnp.where(tril, 0.0, p).astype(BF16), with_ones(prev(va))))
        if has_meta:
            oe = oe + _dot(jnp.exp(s_m - mx).astype(BF16), with_ones(mv_ref[:, ks]))
        denom = oe[:, HD:HD + 1] + jnp.exp(sink - mx)
        o = oe[:, 0:HD] / denom
        for g, hd in enumerate(heads):
            hs = slice(hd * HD, (hd + 1) * HD)
            gate = _sigmoid_bf16(pa_ref[rs, A_GA + hd * HD:A_GA + (hd + 1) * HD]).astype(F32)
            y_ref[rs, hs] = (ym_s[rs, hs] + o[g * L:(g + 1) * L] * gate).astype(BF16)


def _tail_kernel(y_ref, x_ref, wo_ref, gn_ref, wu_ref, wd_ref, gnext_ref, o_ref, *rest, nj, last_layer):
    hf_ref = rest[-1]
    j = pl.program_id(1)

    @pl.when(j == 0)
    def _():
        x1 = x_ref[...] + _dot(y_ref[...], wo_ref[...])
        o_ref[...] = x1
        hf_ref[...] = _rms(x1, gn_ref[...]).astype(BF16)

    a = jnp.square(jnp.maximum(_dot(hf_ref[...], wu_ref[...]), 0.0)).astype(BF16)
    o_ref[...] += _dot(a, wd_ref[...])

    @pl.when(j == nj - 1)
    def _():
        if last_layer:
            o_ref[...] = _rms(o_ref[...], gnext_ref[...])
        else:
            rest[0][...] = _rms(o_ref[...], gnext_ref[...]).astype(BF16)


def _tail(l, y, x, w_out, g_mlp, w_up, w_down, g_next, tm, tf):
    rows = x.shape[0]
    nj = D_FF // tf
    last_layer = l == DEPTH - 1
    row_spec = pl.BlockSpec((tm, D_MODEL), lambda i, j: (i, 0))
    out_specs, out_shape = [row_spec], [jax.ShapeDtypeStruct((rows, D_MODEL), F32)]
    if not last_layer:
        out_specs.append(row_spec)
        out_shape.append(jax.ShapeDtypeStruct((rows, D_MODEL), BF16))
    return pl.pallas_call(
        functools.partial(_tail_kernel, nj=nj, last_layer=last_layer),
        grid=(rows // tm, nj),
        in_specs=[row_spec, row_spec,
                  pl.BlockSpec((None, D_MODEL, D_MODEL), lambda i, j: (l, 0, 0)),
                  pl.BlockSpec((None, 1, D_MODEL), lambda i, j: (l, 0, 0)),
                  pl.BlockSpec((None, D_MODEL, tf), lambda i, j: (l, 0, j)),
                  pl.BlockSpec((None, tf, D_MODEL), lambda i, j: (l, j, 0)),
                  pl.BlockSpec((None, 1, D_MODEL), lambda i, j: (l + 1, 0, 0))],
        out_specs=out_specs,
        out_shape=out_shape,
        scratch_shapes=[pltpu.VMEM((tm, D_MODEL), BF16)],
        compiler_params=_cparams(2),
        name="tail",
    )(y, x, w_out, g_mlp, w_up, w_down, g_next)


def _mlstm_step_kernel(q_ref, k_ref, v_ref, o_ref, gm_ref, ig_ref, fg_ref, m_ref, n_ref, c_ref,
                       kall_ref, vt_ref, bi_ref, bf_ref, nmh_ref, *rest):
    y_ref, co_ref, no_ref, mo_ref = rest[-4:]
    i = pl.program_id(0)
    nb = DEC_BLK
    row8 = lax.broadcasted_iota(jnp.int32, (nb, DV_M), 0)
    rowb = lax.broadcasted_iota(jnp.int32, (kall_ref.shape[0], DQK_M), 0)
    for h in range(NH_M):
        ds = slice(h * DQK_M, (h + 1) * DQK_M)
        vs = slice(h * DV_M, (h + 1) * DV_M)
        q, k, v, nrm = q_ref[:, ds], k_ref[:, ds], v_ref[:, vs], n_ref[:, ds]
        ig = ig_ref[:, ds] + bi_ref[:, ds]
        lf = _log_sigmoid(fg_ref[:, ds] + bf_ref[:, ds])
        m_prev = m_ref[:, ds]
        m_t = jnp.maximum(lf + m_prev, ig)
        w_inter = jnp.exp(lf + m_prev - m_t)
        w_in = jnp.exp(ig - m_t)
        s = jnp.sum(q * k, axis=1, keepdims=True) * w_in
        qb = q.astype(BF16)
        kall = kall_ref[:, ds]
        vt = vt_ref[vs, :]
        cq = jnp.zeros((nb, DV_M), F32)
        for bb in range(nb):
            c_old = c_ref[bb, h]
            cq = jnp.where(row8 == bb, _dot_nt(qb, c_old.astype(BF16)), cq)
            outer = _dot(vt, jnp.where(rowb == i * nb + bb, kall, jnp.zeros_like(kall)))
            co_ref[bb, h] = w_inter[bb:bb + 1, :] * c_old + w_in[bb:bb + 1, :] * outer
        num = w_inter[:, 0:1] * cq + s[:, 0:1] * v
        den = w_inter * jnp.sum(nrm * q, axis=1, keepdims=True) + s
        hh = num / jnp.maximum(jnp.abs(den), jnp.exp(-m_t))[:, 0:1]
        no_ref[:, ds] = w_inter * nrm + w_in * k
        mo_ref[:, ds] = m_t
        hn = hh * lax.rsqrt(jnp.mean(hh * hh, axis=1, keepdims=True) + EPS) * nmh_ref[:, vs]
        y_ref[:, vs] = jax.nn.sigmoid(gm_ref[:, vs]) * (jax.nn.sigmoid(o_ref[:, vs]) * hn)


def _mlstm_step(l, q, k, v, o, gm, ig, fg, m, n, c_all, c_out_prev, kall, vt, bi, bf, nmh):
    nbatch = q.shape[0]
    qd = NH_M * DQK_M
    rowq = pl.BlockSpec((DEC_BLK, qd), lambda i: (i, 0))
    rowv = pl.BlockSpec((DEC_BLK, D_MODEL), lambda i: (i, 0))
    cspec = pl.BlockSpec((None, DEC_BLK, NH_M, DV_M, DQK_M), lambda i: (l, i, 0, 0, 0))
    const = lambda shape: pl.BlockSpec(shape, lambda i: (0, 0))
    in_specs = [rowq, rowq, rowv, rowv, rowv, rowq, rowq, rowq, rowq, cspec,
                const((nbatch, qd)), const((D_MODEL, nbatch)), const((1, qd)), const((1, qd)),
                const((1, D_MODEL))]
    args = [q, k, v, o, gm, ig, fg, m, n, c_all, kall, vt, bi, bf, nmh]
    aliases = {}
    if c_out_prev is not None:
        aliases = {len(args): 1}
        in_specs.append(pl.BlockSpec(memory_space=pl.ANY))
        args.append(c_out_prev)
    return pl.pallas_call(
        _mlstm_step_kernel,
        grid=(nbatch // DEC_BLK,),
        in_specs=in_specs,
        out_specs=[rowv, cspec, rowq, rowq],
        out_shape=[jax.ShapeDtypeStruct((nbatch, D_MODEL), F32),
                   jax.ShapeDtypeStruct(c_all.shape, F32),
                   jax.ShapeDtypeStruct((nbatch, qd), F32),
                   jax.ShapeDtypeStruct((nbatch, qd), F32)],
        input_output_aliases=aliases,
        compiler_params=_cparams(1),
        name="mlstm_step",
    )(*args)


def _swa_step_kernel(q_ref, ga_ref, kn_ref, vn_ref, mk_ref, mv_ref, wk_ref, wv_ref, sink_ref, *rest):
    y_ref, ok_ref, ov_ref = rest[-3:]
    W = WINDOW
    hrow = lax.broadcasted_iota(jnp.int32, (N_HEADS, HD), 0)
    first = hrow < GQ
    wcol = lax.broadcasted_iota(jnp.int32, (N_HEADS, W), 1)
    wrow = lax.broadcasted_iota(jnp.int32, (N_KV * W, HD), 0)
    sink = sink_ref[...][:, 0:1]

    def per_kv(fn):
        a0, a1 = fn(0), fn(1)
        return jnp.where(first[:, 0:a0.shape[1]], a0, a1)

    for bb in range(DEC_BLK):
        q = q_ref[bb]
        qb = q.astype(BF16)
        rows_of = lambda ref, kv, n: ref[bb, pl.ds(kv, n, stride=N_KV), :].astype(BF16)
        kn, vn = kn_ref[bb:bb + 1, :], vn_ref[bb:bb + 1, :]
        kvs = lambda a, kv: a[:, kv * HD:(kv + 1) * HD]
        s_w = jnp.where(wcol >= 1, per_kv(lambda kv: _dot_nt(qb, rows_of(wk_ref, kv, W))), -jnp.inf)
        s_m = per_kv(lambda kv: _dot_nt(qb, rows_of(mk_ref, kv, N_META)))
        kn8 = per_kv(lambda kv: jnp.broadcast_to(kvs(kn, kv), (N_HEADS, HD)))
        vn8 = per_kv(lambda kv: jnp.broadcast_to(kvs(vn, kv), (N_HEADS, HD)))
        s_n = jnp.sum(q * kn8, axis=1, keepdims=True)
        mx = jnp.maximum(jnp.maximum(jnp.max(s_w, axis=1, keepdims=True), jnp.max(s_m, axis=1, keepdims=True)),
                         jnp.maximum(s_n, sink))
        p_w, p_m, p_n = jnp.exp(s_w - mx), jnp.exp(s_m - mx), jnp.exp(s_n - mx)
        denom = (jnp.sum(p_w, axis=1, keepdims=True) + jnp.sum(p_m, axis=1, keepdims=True) + p_n
                 + jnp.exp(sink - mx))
        o = (per_kv(lambda kv: _dot(p_w.astype(BF16), rows_of(wv_ref, kv, W)))
             + per_kv(lambda kv: _dot(p_m.astype(BF16), rows_of(mv_ref, kv, N_META)))
             + p_n * vn8)
        y_ref[bb] = (o / denom) * jax.nn.sigmoid(ga_ref[bb])
        for src_ref, new, dst_ref in ((wk_ref, kn, ok_ref), (wv_ref, vn, ov_ref)):
            shifted = pltpu.roll(src_ref[bb], N_KV * (W - 1), 0)
            for kv in range(N_KV):
                shifted = jnp.where(wrow == N_KV * (W - 1) + kv, kvs(new, kv), shifted)
            dst_ref[bb] = shifted


def _swa_step(l, q, ga, kn, vn, mk_all, mv_all, wk_all, wv_all, ok_prev, ov_prev, sink8):
    nbatch = q.shape[0]
    kw = N_KV * HD
    hspec = pl.BlockSpec((DEC_BLK, N_HEADS, HD), lambda i: (i, 0, 0))
    nspec = pl.BlockSpec((DEC_BLK, kw), lambda i: (i, 0))
    mspec = pl.BlockSpec((None, DEC_BLK, N_KV * N_META, HD), lambda i: (l, i, 0, 0))
    wspec = pl.BlockSpec((None, DEC_BLK, N_KV * WINDOW, HD), lambda i: (l, i, 0, 0))
    in_specs = [hspec, hspec, nspec, nspec, mspec, mspec, wspec, wspec,
                pl.BlockSpec((N_HEADS, HD), lambda i: (0, 0))]
    args = [q, ga, kn, vn, mk_all, mv_all, wk_all, wv_all, sink8]
    aliases = {}
    if ok_prev is not None:
        aliases = {len(args): 1, len(args) + 1: 2}
        in_specs += [pl.BlockSpec(memory_space=pl.ANY)] * 2
        args += [ok_prev, ov_prev]
    return pl.pallas_call(
        _swa_step_kernel,
        grid=(nbatch // DEC_BLK,),
        in_specs=in_specs,
        out_specs=[hspec, wspec, wspec],
        out_shape=[jax.ShapeDtypeStruct((nbatch, N_HEADS, HD), F32),
                   jax.ShapeDtypeStruct(wk_all.shape, F32),
                   jax.ShapeDtypeStruct(wv_all.shape, F32)],
        input_output_aliases=aliases,
        compiler_params=_cparams(1),
        name="swa_step",
    )(*args)


def _rope_tables(pos):
    half = ROT_DIM // 2
    inv = jnp.float32(ROPE_THETA) ** (-jnp.arange(half, dtype=F32) * 2.0 / ROT_DIM)
    ang = pos.astype(F32)[:, None] * inv[None, :]
    cos, sin = jnp.cos(ang), jnp.sin(ang)
    t = pos.shape[0]
    cs = jnp.concatenate([cos, cos, jnp.ones((t, HD - ROT_DIM), F32)], axis=1)
    s1 = jnp.concatenate([jnp.zeros((t, half), F32), sin, jnp.zeros((t, HD - ROT_DIM), F32)], axis=1)
    s2 = jnp.concatenate([-sin, jnp.zeros((t, HD - half), F32)], axis=1)
    return cs, s1, s2


def _pack_in_weights(w_in):
    edges = np.concatenate([[0], np.cumsum(IN_WIDTHS)])
    seg = [w_in[:, :, edges[i]:edges[i + 1]] for i in range(len(IN_WIDTHS))]
    q_m, k_m, v_m, ig, fg, o_m, q_a, k_a, v_a, g_m, g_a = seg
    w_a = jnp.concatenate([v_m, o_m, g_m, g_a], axis=2).astype(BF16)
    pad = jnp.zeros(w_in.shape[:2] + (B_TOT - B_G - 2 * NH_M,), w_in.dtype)
    w_b = jnp.concatenate([q_a, k_a, v_a, q_m, k_m, ig, fg, pad], axis=2).astype(BF16)
    return w_a, w_b


def _rep(a, width=128):
    return jnp.repeat(a, width, axis=1)


def kernel(x_prompt, x_sample, state_C, state_n, state_m, cache_meta_k, cache_meta_v, cache_win_k, cache_win_v,
           meta_tokens, norm_mix, w_in, b_igate, b_fgate, norm_mh, attn_sinks, w_out, norm_mlp, w_up, w_down,
           norm_final):
    B, S, _ = x_prompt.shape
    DB = x_sample.shape[0]
    W = cache_win_k.shape[2]
    assert (S % CHUNK, W, x_sample.shape[1], DB % DEC_BLK) == (0, WINDOW, 1, 0)
    assert (state_C.dtype, cache_win_k.dtype, cache_win_v.dtype) == (F32, F32, F32)
    nchunk = S // CHUNK
    kw = N_KV * HD
    big_tm = 1024

    w_a, w_b = _pack_in_weights(w_in)
    w_out_b, w_up_b, w_down_b = w_out.astype(BF16), w_up.astype(BF16), w_down.astype(BF16)
    g_mix = norm_mix.astype(F32).reshape(DEPTH, 1, D_MODEL)
    g_mlp = norm_mlp.astype(F32).reshape(DEPTH, 1, D_MODEL)
    g_next = jnp.concatenate([g_mix, norm_final.astype(F32).reshape(1, 1, D_MODEL)], axis=0)
    rows_view = lambda a: a.reshape(DEPTH, DB, -1, HD)
    mk_all, mv_all = rows_view(cache_meta_k), rows_view(cache_meta_v)
    wk_all, wv_all = rows_view(cache_win_k), rows_view(cache_win_v)

    tabs_p = _rope_tables(N_META + jnp.arange(S, dtype=jnp.int32))
    pos_small = jnp.concatenate([jnp.full((DB,), PAST_LEN, jnp.int32), jnp.arange(N_META, dtype=jnp.int32),
                                 jnp.zeros((SMALL_ROWS - DB - N_META,), jnp.int32)])
    tabs_s = _rope_tables(pos_small)

    xp = x_prompt.reshape(B * S, D_MODEL)
    xs = jnp.concatenate([x_sample.reshape(DB, D_MODEL), meta_tokens.astype(F32),
                          jnp.zeros((SMALL_ROWS - DB - N_META, D_MODEL), F32)], axis=0)
    zero_state = (jnp.zeros((NH_M, DQK_M, DV_M), F32), jnp.zeros((NH_M, DQK_M, 128), F32),
                  jnp.zeros((NH_M, 8, 128), F32))
    no_meta = jnp.zeros((N_META, kw), BF16)

    outs = {k: [] for k in ("pC", "pn", "pm", "pmk", "pmv", "pwk", "pwv", "sn", "sm")}
    sC = swk = swv = None
    hs_ = _norm(0, xs, g_mix, SMALL_ROWS)
    hp = _norm(0, xp, g_mix, big_tm)
    for l in range(DEPTH):
        nmh = norm_mh[l].reshape(1, D_MODEL)
        bias8 = jnp.broadcast_to(jnp.concatenate([b_igate[l], b_fgate[l]])[:, None], (2 * NH_M, CHUNK)).astype(F32)
        sinks = attn_sinks[l].astype(F32)

        pa_s = _inproj_a(l, hs_, w_a, SMALL_ROWS)
        qa_s, misc_s, kt_s, gt_s = _inproj_b(l, hs_, w_b, tabs_s, SMALL_ROWS)

        y_meta, ct_m, n_m, m_m = _mix(sinks, qa_s, misc_s, pa_s, kt_s, gt_s, bias8, nmh, *zero_state, no_meta, no_meta,
                                      nb=1, nblk=1, sub=1, row0=META_ROW0, valid_len=N_META, has_meta=False)
        mk = misc_s[META_ROW0:META_ROW0 + N_META, M_KA:M_KA + kw]
        mv = misc_s[META_ROW0:META_ROW0 + N_META, M_VA:M_VA + kw]

        f32s = lambda a, lo, width: a[:DB, lo:lo + width].astype(F32)
        gcol = gt_s[:, :DB].T
        k_s = kt_s[:, :DB].T
        ys_m, sC, sn, sm = _mlstm_step(
            l, f32s(misc_s, M_QM, NH_M * DQK_M), k_s.astype(F32), f32s(pa_s, A_VM, D_MODEL),
            f32s(pa_s, A_OM, D_MODEL), f32s(pa_s, A_GM, D_MODEL),
            _rep(gcol[:, :NH_M]), _rep(gcol[:, NH_M:]), _rep(state_m[l].astype(F32)),
            state_n[l].astype(F32).reshape(DB, NH_M * DQK_M), state_C, sC,
            k_s, pa_s[:DB, A_VM:A_VM + D_MODEL].T,
            _rep(b_igate[l].astype(F32)[None]), _rep(b_fgate[l].astype(F32)[None]), nmh)
        ys_a, swk, swv = _swa_step(
            l, qa_s[:DB].astype(F32).reshape(DB, N_HEADS, HD), f32s(pa_s, A_GA, D_MODEL).reshape(DB, N_HEADS, HD),
            f32s(misc_s, M_KA, kw), f32s(misc_s, M_VA, kw), mk_all, mv_all, wk_all, wv_all, swk, swv,
            jnp.broadcast_to(sinks[:, None], (N_HEADS, HD)))
        y_s = jnp.concatenate([(ys_m + ys_a.reshape(DB, D_MODEL)).astype(BF16), y_meta], axis=0)
        res = _tail(l, y_s, xs, w_out_b, g_mlp, w_up_b, w_down_b, g_next, tm=SMALL_ROWS, tf=512)
        xs, hs_ = res[0], res[-1]

        pa = _inproj_a(l, hp, w_a, big_tm)
        qa, misc, kt, gt = _inproj_b(l, hp, w_b, tabs_p, big_tm)
        y, ct_f, n_f, m_f = _mix(sinks, qa, misc, pa, kt, gt, bias8, nmh, ct_m[0], n_m[0], m_m[0], mk, mv,
                                 nb=B, nblk=nchunk, sub=MIX_SUB, row0=0, valid_len=CHUNK, has_meta=True)
        res = _tail(l, y, xp, w_out_b, g_mlp, w_up_b, w_down_b, g_next, tm=big_tm, tf=TAIL_TF)
        xp, hp = res[0], res[-1]

        outs["pC"].append(jnp.swapaxes(ct_f, 2, 3).astype(state_C.dtype))
        outs["pn"].append(n_f[:, :, :, 0].astype(state_n.dtype))
        outs["pm"].append(m_f[:, :, 0, 0].astype(state_m.dtype))
        outs["pmk"].append(jnp.broadcast_to(mk.astype(F32).reshape(1, N_META, N_KV, HD), (B, N_META, N_KV, HD)))
        outs["pmv"].append(jnp.broadcast_to(mv.astype(F32).reshape(1, N_META, N_KV, HD), (B, N_META, N_KV, HD)))
        misc3 = misc.reshape(B, S, M_TOT)
        outs["pwk"].append(misc3[:, S - WINDOW:, M_KA:M_KA + kw].astype(F32).reshape(B, WINDOW, N_KV, HD))
        outs["pwv"].append(misc3[:, S - WINDOW:, M_VA:M_VA + kw].astype(F32).reshape(B, WINDOW, N_KV, HD))
        outs["sn"].append(sn.reshape(DB, NH_M, DQK_M).astype(state_n.dtype))
        outs["sm"].append(sm.reshape(DB, NH_M, 128)[:, :, 0].astype(state_m.dtype))

    y_prompt = xp.reshape(B, S, D_MODEL)
    y_sample = xs[:DB].reshape(DB, 1, D_MODEL)
    st = lambda k: jnp.stack(outs[k])
    return (y_prompt, y_sample, st("pC"), st("pn"), st("pm"), st("pmk"), st("pmv"), st("pwk"), st("pwv"),
            sC, st("sn"), st("sm"), swk.reshape(cache_win_k.shape), swv.reshape(cache_win_v.shape))
```

```python
import functools

import jax
import jax.numpy as jnp
import numpy as np
from jax import lax
from jax.experimental import pallas as pl
from jax.experimental.pallas import tpu as pltpu

F32 = jnp.float32
BF16 = jnp.bfloat16

D_MODEL = 1024
DEPTH = 4
PAST_LEN = 8192
N_META = 16
NH_M = 4
DV_M = 256
DQK_M = 128
CHUNK = 128
N_HEADS = 8
N_KV = 2
GQ = N_HEADS // N_KV
HD = 128
WINDOW = 128
ROT_DIM = 32
ROPE_THETA = 500000.0
D_FF = 4 * D_MODEL
EPS = 1e-6
IN_WIDTHS = (NH_M * DQK_M, NH_M * DQK_M, NH_M * DV_M, NH_M, NH_M, NH_M * DV_M,
             N_HEADS * HD, N_KV * HD, N_KV * HD, D_MODEL, D_MODEL)

A_VM, A_OM, A_GM, A_GA, A_TOT = 0, 1024, 2048, 3072, 4096
A_TN = 2048
B_QA, B_KA, B_VA, B_QM, B_KM, B_G, B_TOT = 0, 1024, 1280, 1536, 2048, 2560, 2688
M_KA, M_VA, M_QM, M_TOT = 0, 256, 512, 1024
TAIL_TF = 1024
MIX_SUB = 1
SMALL_ROWS = 256
META_ROW0 = 128
DEC_BLK = 8
VMEM_LIMIT = 56 * 1024 * 1024


def _cparams(n_axes):
    return pltpu.CompilerParams(dimension_semantics=("arbitrary",) * n_axes, vmem_limit_bytes=VMEM_LIMIT)


def _rms(xf, gain):
    ms = jnp.mean(xf * xf, axis=-1, keepdims=True)
    return xf * lax.rsqrt(ms + EPS) * gain


def _dot(a, b):
    return jnp.dot(a, b, preferred_element_type=F32)


def _dot_nt(a, b):
    return lax.dot_general(a, b, (((1,), (1,)), ((), ())), preferred_element_type=F32)


def _log_sigmoid(x):
    return jnp.minimum(x, 0.0) - jnp.log1p(jnp.exp(-jnp.abs(x)))


def _sigmoid_bf16(x):
    half = jnp.asarray(0.5, x.dtype)
    return jnp.tanh(x * half) * half + half


def _norm_kernel(x_ref, gain_ref, o_ref):
    o_ref[...] = _rms(x_ref[...], gain_ref[...]).astype(BF16)


def _norm(l, x, gains, tm):
    rows = x.shape[0]
    return pl.pallas_call(
        _norm_kernel,
        grid=(rows // tm,),
        in_specs=[pl.BlockSpec((tm, D_MODEL), lambda i: (i, 0)),
                  pl.BlockSpec((None, 1, D_MODEL), lambda i: (l, 0, 0))],
        out_specs=pl.BlockSpec((tm, D_MODEL), lambda i: (i, 0)),
        out_shape=jax.ShapeDtypeStruct((rows, D_MODEL), BF16),
        compiler_params=_cparams(1),
        name="norm",
    )(x, gains)


def _inproj_a_kernel(h_ref, w_ref, o_ref):
    o_ref[...] = _dot(h_ref[...], w_ref[...]).astype(o_ref.dtype)


def _inproj_a(l, h, w_a, tm, out_dtype):
    rows = h.shape[0]
    return pl.pallas_call(
        _inproj_a_kernel,
        grid=(rows // tm, A_TOT // A_TN),
        in_specs=[pl.BlockSpec((tm, D_MODEL), lambda i, j: (i, 0)),
                  pl.BlockSpec((None, D_MODEL, A_TN), lambda i, j: (l, 0, j))],
        out_specs=pl.BlockSpec((tm, A_TN), lambda i, j: (i, j)),
        out_shape=jax.ShapeDtypeStruct((rows, A_TOT), out_dtype),
        compiler_params=_cparams(2),
        name="inproj_a",
    )(h, w_a)


def _inproj_b_kernel(h_ref, w_ref, cs_ref, s1_ref, s2_ref, qa_ref, misc_ref, kt_ref, gt_ref, *row_form):
    h = h_ref[...]

    def rope(seg):
        return (seg * cs_ref[...] + pltpu.roll(seg, 16, 1) * s1_ref[...]
                + pltpu.roll(seg, HD - 16, 1) * s2_ref[...])

    acc = _dot(h, w_ref[:, B_QA:B_KA])
    for c in range(N_HEADS):
        cs = slice(c * HD, (c + 1) * HD)
        qa_ref[:, cs] = (rope(acc[:, cs]) * (HD ** -0.5)).astype(qa_ref.dtype)

    acc = _dot(h, w_ref[:, B_KA:B_KM])
    for c in range(N_KV):
        cs = slice(M_KA + c * HD, M_KA + (c + 1) * HD)
        misc_ref[:, cs] = rope(acc[:, cs]).astype(misc_ref.dtype)
    misc_ref[:, M_VA:M_QM] = acc[:, M_VA:M_QM].astype(misc_ref.dtype)
    misc_ref[:, M_QM:M_TOT] = (acc[:, M_QM:M_TOT] * (DQK_M ** -0.5)).astype(misc_ref.dtype)

    acc = _dot(h, w_ref[:, B_KM:B_TOT])
    kt_ref[...] = acc[:, 0:NH_M * DQK_M].T.astype(BF16)
    gt_ref[...] = acc[:, NH_M * DQK_M:B_TOT - B_KM].T[0:2 * NH_M]

    if row_form:
        km_ref, grep_ref = row_form
        km_ref[...] = acc[:, 0:NH_M * DQK_M]
        grep_ref[...] = _dot(h, w_ref[:, B_TOT:B_TOT + G_REP])


G_REP = 2 * NH_M * 128


def _inproj_b(l, h, w_b, tabs, tm, small):
    rows = h.shape[0]
    tab_blocks = tabs[0].shape[0] // tm
    tab_spec = pl.BlockSpec((tm, HD), lambda i: (i % tab_blocks, 0))
    dt = F32 if small else BF16
    row_spec = lambda width: pl.BlockSpec((tm, width), lambda i: (i, 0))
    out_specs = [row_spec(D_MODEL), row_spec(M_TOT),
                 pl.BlockSpec((NH_M * DQK_M, tm), lambda i: (0, i)),
                 pl.BlockSpec((2 * NH_M, tm), lambda i: (0, i))]
    out_shape = [jax.ShapeDtypeStruct((rows, D_MODEL), dt),
                 jax.ShapeDtypeStruct((rows, M_TOT), dt),
                 jax.ShapeDtypeStruct((NH_M * DQK_M, rows), BF16),
                 jax.ShapeDtypeStruct((2 * NH_M, rows), F32)]
    if small:
        out_specs += [row_spec(NH_M * DQK_M), row_spec(G_REP)]
        out_shape += [jax.ShapeDtypeStruct((rows, NH_M * DQK_M), F32), jax.ShapeDtypeStruct((rows, G_REP), F32)]
    return pl.pallas_call(
        _inproj_b_kernel,
        grid=(rows // tm,),
        in_specs=[pl.BlockSpec((tm, D_MODEL), lambda i: (i, 0)),
                  pl.BlockSpec((None, D_MODEL, w_b.shape[2]), lambda i: (l, 0, 0)),
                  tab_spec, tab_spec, tab_spec],
        out_specs=out_specs,
        out_shape=out_shape,
        compiler_params=_cparams(1),
        name="inproj_b",
    )(h, w_b, *tabs)


def _mix_kernel(sink_ref, misc_ref, miscp_ref, pa_ref, qa_ref, kt_ref, g_ref, bias_ref, nmh_ref,
                ct0_ref, n0_ref, m0_ref, mk_ref, mv_ref,
                y_ref, ctf_ref, nf_ref, mf_ref, ct_s, n_s, m_s, *, l, nc, sub, valid_len, has_meta):
    c = pl.program_id(1)
    L = CHUNK

    @pl.when(c == 0)
    def _():
        ct_s[...] = ct0_ref[...]
        n_s[...] = n0_ref[...]
        m_s[...] = m0_ref[...]

    row = lax.broadcasted_iota(jnp.int32, (L, L), 0)
    col = lax.broadcasted_iota(jnp.int32, (L, L), 1)
    tril = col <= row
    eye = col == row

    gates = g_ref[...] + bias_ref[:, 0:1]
    ig_sub = gates[0:NH_M]
    lf_sub = _log_sigmoid(gates[NH_M:2 * NH_M])
    if valid_len < L:
        ok = lax.broadcasted_iota(jnp.int32, (NH_M, sub * L), 1) < valid_len
        ig_sub = jnp.where(ok, ig_sub, -jnp.inf)
        lf_sub = jnp.where(ok, lf_sub, 0.0)

    cts = [ct_s[h] for h in range(NH_M)]
    nms = [n_s[h] for h in range(NH_M)]
    ms = [m_s[h, 0:1, 0:1] for h in range(NH_M)]
    for sb in range(sub):
        rs = slice(sb * L, (sb + 1) * L)
        ym = []
        for h in range(NH_M):
            ig = ig_sub[h:h + 1, rs]
            lf = lf_sub[h:h + 1, rs]
            m_prev = ms[h]
            b_col = jnp.sum(jnp.where(tril, lf, 0.0), axis=1, keepdims=True)
            b_row = jnp.sum(jnp.where(eye, b_col, 0.0), axis=0, keepdims=True)
            a_col = b_col + m_prev
            dmat = jnp.where(tril, b_col - b_row + ig, -jnp.inf)
            m_col = jnp.maximum(a_col, jnp.max(dmat, axis=1, keepdims=True))
            w_inter = jnp.exp(a_col - m_col)
            wmat = jnp.exp(dmat - m_col)

            q = misc_ref[rs, M_QM + h * DQK_M:M_QM + (h + 1) * DQK_M].astype(BF16)
            kt = kt_ref[h * DQK_M:(h + 1) * DQK_M, rs]
            v = pa_ref[rs, A_VM + h * DV_M:A_VM + (h + 1) * DV_M].astype(BF16)
            ct = cts[h]
            nm = nms[h]

            s = _dot(q, kt) * wmat
            num = w_inter * _dot(q, ct.astype(BF16)) + _dot(s.astype(BF16), v)
            qn = _dot(q, nm.astype(BF16))[:, 0:1]
            den = w_inter * qn + jnp.sum(s, axis=1, keepdims=True)
            hh = num / jnp.maximum(jnp.abs(den), jnp.exp(-m_col))

            m_new = m_col[L - 1:L]
            decay = w_inter[L - 1:L]
            g_row = jnp.exp(b_col[L - 1:L] - b_row + ig - m_new)
            ktg = kt.astype(F32) * g_row
            cts[h] = decay * ct + _dot(ktg.astype(BF16), v)
            nms[h] = decay * nm + jnp.sum(ktg, axis=1, keepdims=True)
            ms[h] = m_new

            hs = slice(h * DV_M, (h + 1) * DV_M)
            hn = hh * lax.rsqrt(jnp.mean(hh * hh, axis=1, keepdims=True) + EPS) * nmh_ref[:, hs]
            gate = (_sigmoid_bf16(pa_ref[rs, A_GM + h * DV_M:A_GM + (h + 1) * DV_M])
                    * _sigmoid_bf16(pa_ref[rs, A_OM + h * DV_M:A_OM + (h + 1) * DV_M]))
            ym.append(gate.astype(F32) * hn)

        _swa_part(c, sb, sink_ref.at[l], qa_ref, misc_ref, miscp_ref, pa_ref, mk_ref, mv_ref, ym, y_ref, has_meta)

    for h in range(NH_M):
        ct_s[h] = cts[h]
        n_s[h] = nms[h]
        m_s[h] = jnp.broadcast_to(ms[h], (8, 128))

    @pl.when(c == nc - 1)
    def _():
        ctf_ref[0] = ct_s[...]
        nf_ref[0] = n_s[...]
        mf_ref[0] = m_s[...]


def _mix(l, sinks, qa, misc, pa, kt, gt, bias8, nmh, ct0, n0, m0, misc_small, nb, nblk, sub, row0, valid_len,
         has_meta):
    nc = nblk // sub
    rows = sub * CHUNK
    r0 = row0 // rows
    rb = lambda b, c: r0 + b * nc + c
    rp = lambda b, c: (r0 + b * nc + c) * sub - jnp.minimum(c, 1)
    kw = N_KV * HD
    first4 = lambda b, c: (0, 0, 0, 0)
    st4 = lambda b, c: (b, 0, 0, 0)
    return pl.pallas_call(
        functools.partial(_mix_kernel, l=l, nc=nc, sub=sub, valid_len=valid_len, has_meta=has_meta),
        grid=(nb, nc),
        in_specs=[pl.BlockSpec(memory_space=pltpu.SMEM),
                  pl.BlockSpec((rows, M_TOT), lambda b, c: (rb(b, c), 0)),
                  pl.BlockSpec((CHUNK, 2 * kw), lambda b, c: (rp(b, c), 0)),
                  pl.BlockSpec((rows, A_TOT), lambda b, c: (rb(b, c), 0)),
                  pl.BlockSpec((rows, D_MODEL), lambda b, c: (rb(b, c), 0)),
                  pl.BlockSpec((NH_M * DQK_M, rows), lambda b, c: (0, rb(b, c))),
                  pl.BlockSpec((2 * NH_M, rows), lambda b, c: (0, rb(b, c))),
                  pl.BlockSpec((None, 2 * NH_M, 128), lambda b, c: (l, 0, 0)),
                  pl.BlockSpec((None, 1, D_MODEL), lambda b, c: (l, 0, 0)),
                  pl.BlockSpec((None, NH_M, DQK_M, DV_M), first4),
                  pl.BlockSpec((None, NH_M, DQK_M, 128), first4),
                  pl.BlockSpec((None, NH_M, 8, 128), first4),
                  pl.BlockSpec((N_META, kw), lambda b, c: (META_ROW0 // N_META, M_KA // kw)),
                  pl.BlockSpec((N_META, kw), lambda b, c: (META_ROW0 // N_META, M_VA // kw))],
        out_specs=[pl.BlockSpec((rows, D_MODEL), lambda b, c: (b * nc + c, 0)),
                   pl.BlockSpec((1, NH_M, DQK_M, DV_M), st4),
                   pl.BlockSpec((1, NH_M, DQK_M, 128), st4),
                   pl.BlockSpec((1, NH_M, 8, 128), st4)],
        out_shape=[jax.ShapeDtypeStruct((nb * nc * rows, D_MODEL), BF16),
                   jax.ShapeDtypeStruct((nb, NH_M, DQK_M, DV_M), F32),
                   jax.ShapeDtypeStruct((nb, NH_M, DQK_M, 128), F32),
                   jax.ShapeDtypeStruct((nb, NH_M, 8, 128), F32)],
        scratch_shapes=[pltpu.VMEM((NH_M, DQK_M, DV_M), F32),
                        pltpu.VMEM((NH_M, DQK_M, 128), F32),
                        pltpu.VMEM((NH_M, 8, 128), F32)],
        compiler_params=_cparams(2),
        name="mix",
    )(sinks, misc, misc, pa, qa, kt, gt, bias8, nmh, ct0, n0, m0, misc_small, misc_small)


def _swa_part(c, sb, sink_ref, q_ref, misc_ref, miscp_ref, pa_ref, mk_ref, mv_ref, ym, y_ref, has_meta):
    L = WINDOW
    R = GQ * L
    rs = slice(sb * L, (sb + 1) * L)
    row = lax.broadcasted_iota(jnp.int32, (R, L), 0)
    col = lax.broadcasted_iota(jnp.int32, (R, L), 1)
    tril = col <= (row & (L - 1))
    head_of_row = lax.shift_right_logical(lax.broadcasted_iota(jnp.int32, (R, 1), 0), L.bit_length() - 1)
    with_ones = lambda v: jnp.concatenate([v, jnp.ones((v.shape[0], HD), BF16)], axis=1)
    if sb == 0:
        prev = lambda cols: miscp_ref[:, cols].astype(BF16)
        prev_ok = jnp.full((R, L), c, jnp.int32) > 0
    else:
        prev = lambda cols: misc_ref[(sb - 1) * L:sb * L, cols].astype(BF16)
        prev_ok = None
    for kv in range(N_KV):
        ka = slice(M_KA + kv * HD, M_KA + (kv + 1) * HD)
        va = slice(M_VA + kv * HD, M_VA + (kv + 1) * HD)
        ks = slice(kv * HD, (kv + 1) * HD)
        heads = [kv * GQ + g for g in range(GQ)]
        q = jnp.concatenate([q_ref[rs, hd * HD:(hd + 1) * HD].astype(BF16) for hd in heads], axis=0)
        sink = jnp.zeros((R, 1), F32)
        for g, hd in enumerate(heads):
            sink = jnp.where(head_of_row == g, sink_ref[hd], sink)
        s_prev = _dot_nt(q, prev(ka))
        if prev_ok is not None:
            s_prev = jnp.where(prev_ok, s_prev, -jnp.inf)
        band = jnp.where(tril, _dot_nt(q, misc_ref[rs, ka].astype(BF16)), s_prev)
        mx = jnp.maximum(jnp.max(band, axis=1, keepdims=True), sink)
        if has_meta:
            s_m = _dot_nt(q, mk_ref[:, ks].astype(BF16))
            mx = jnp.maximum(mx, jnp.max(s_m, axis=1, keepdims=True))
        p = jnp.exp(band - mx)
        oe = (_dot(jnp.where(tril, p, 0.0).astype(BF16), with_ones(misc_ref[rs, va].astype(BF16)))
              + _dot(jnp.where(tril, 0.0, p).astype(BF16), with_ones(prev(va))))
        if has_meta:
            oe = oe + _dot(jnp.exp(s_m - mx).astype(BF16), with_ones(mv_ref[:, ks].astype(BF16)))
        denom = oe[:, HD:HD + 1] + jnp.exp(sink - mx)
        o = oe[:, 0:HD] / denom
        for g, hd in enumerate(heads):
            hs = slice(hd * HD, (hd + 1) * HD)
            gate = _sigmoid_bf16(pa_ref[rs, A_GA + hd * HD:A_GA + (hd + 1) * HD]).astype(F32)
            ym_h = ym[(hd * HD) // DV_M][:, (hd * HD) % DV_M:(hd * HD) % DV_M + HD]
            y_ref[rs, hs] = (ym_h + o[g * L:(g + 1) * L] * gate).astype(BF16)


def _tail_kernel(y_ref, x_ref, wo_ref, gn_ref, wu_ref, wd_ref, gnext_ref, o_ref, *rest, nj, last_layer):
    hf_ref = rest[-1]
    j = pl.program_id(1)

    @pl.when(j == 0)
    def _():
        x1 = x_ref[...] + _dot(y_ref[...], wo_ref[...])
        o_ref[...] = x1
        hf_ref[...] = _rms(x1, gn_ref[...]).astype(BF16)

    a = jnp.square(jnp.maximum(_dot(hf_ref[...], wu_ref[...]), 0.0)).astype(BF16)
    o_ref[...] += _dot(a, wd_ref[...])

    @pl.when(j == nj - 1)
    def _():
        if last_layer:
            o_ref[...] = _rms(o_ref[...], gnext_ref[...])
        else:
            rest[0][...] = _rms(o_ref[...], gnext_ref[...]).astype(BF16)


def _tail(l, y, x, w_out, g_mlp, w_up, w_down, g_next, tm, tf):
    rows = x.shape[0]
    nj = D_FF // tf
    last_layer = l == DEPTH - 1
    row_spec = pl.BlockSpec((tm, D_MODEL), lambda i, j: (i, 0))
    out_specs, out_shape = [row_spec], [jax.ShapeDtypeStruct((rows, D_MODEL), F32)]
    if not last_layer:
        out_specs.append(row_spec)
        out_shape.append(jax.ShapeDtypeStruct((rows, D_MODEL), BF16))
    return pl.pallas_call(
        functools.partial(_tail_kernel, nj=nj, last_layer=last_layer),
        grid=(rows // tm, nj),
        in_specs=[row_spec, row_spec,
                  pl.BlockSpec((None, D_MODEL, D_MODEL), lambda i, j: (l, 0, 0)),
                  pl.BlockSpec((None, 1, D_MODEL), lambda i, j: (l, 0, 0)),
                  pl.BlockSpec((None, D_MODEL, tf), lambda i, j: (l, 0, j)),
                  pl.BlockSpec((None, tf, D_MODEL), lambda i, j: (l, j, 0)),
                  pl.BlockSpec((None, 1, D_MODEL), lambda i, j: (l + 1, 0, 0))],
        out_specs=out_specs,
        out_shape=out_shape,
        scratch_shapes=[pltpu.VMEM((tm, D_MODEL), BF16)],
        compiler_params=_cparams(2),
        name="tail",
    )(y, x, w_out, g_mlp, w_up, w_down, g_next)


def _mlstm_step_kernel(q_ref, k_ref, v_ref, o_ref, gm_ref, ig_ref, fg_ref, m_ref, n_ref, c_ref,
                       kall_ref, vall_ref, bi_ref, bf_ref, nmh_ref, *rest):
    y_ref, co_ref, no_ref, mo_ref = rest[-4:]
    i = pl.program_id(0)
    nb = DEC_BLK
    row8 = lax.broadcasted_iota(jnp.int32, (nb, DV_M), 0)
    rowb = lax.broadcasted_iota(jnp.int32, (kall_ref.shape[0], DQK_M), 0)
    for h in range(NH_M):
        ds = slice(h * DQK_M, (h + 1) * DQK_M)
        vs = slice(h * DV_M, (h + 1) * DV_M)
        q, k, v, nrm = q_ref[:, ds], k_ref[:, ds], v_ref[:, vs], n_ref[:, ds]
        ig = ig_ref[:, ds] + bi_ref[:, ds]
        lf = _log_sigmoid(fg_ref[:, ds] + bf_ref[:, ds])
        m_prev = m_ref[:, ds]
        m_t = jnp.maximum(lf + m_prev, ig)
        w_inter = jnp.exp(lf + m_prev - m_t)
        w_in = jnp.exp(ig - m_t)
        s = jnp.sum(q * k, axis=1, keepdims=True) * w_in
        qb = q.astype(BF16)
        kall = kall_ref[:, ds].astype(BF16)
        vt = vall_ref[:, vs].T.astype(BF16)
        cq = jnp.zeros((nb, DV_M), F32)
        for bb in range(nb):
            c_old = c_ref[bb, h]
            cq = jnp.where(row8 == bb, _dot_nt(qb, c_old.astype(BF16)), cq)
            outer = _dot(vt, jnp.where(rowb == i * nb + bb, kall, jnp.zeros_like(kall)))
            co_ref[bb, h] = w_inter[bb:bb + 1, :] * c_old + w_in[bb:bb + 1, :] * outer
        num = w_inter[:, 0:1] * cq + s[:, 0:1] * v
        den = w_inter * jnp.sum(nrm * q, axis=1, keepdims=True) + s
        hh = num / jnp.maximum(jnp.abs(den), jnp.exp(-m_t))[:, 0:1]
        no_ref[:, ds] = w_inter * nrm + w_in * k
        mo_ref[:, ds] = m_t
        hn = hh * lax.rsqrt(jnp.mean(hh * hh, axis=1, keepdims=True) + EPS) * nmh_ref[:, vs]
        y_ref[:, vs] = jax.nn.sigmoid(gm_ref[:, vs]) * (jax.nn.sigmoid(o_ref[:, vs]) * hn)


def _mlstm_step(l, misc, km, pa, grep, m_all, n_all, c_all, c_out_prev, bi, bf, nmh):
    nbatch = c_all.shape[1]
    qd = NH_M * DQK_M
    rowq = lambda j: pl.BlockSpec((DEC_BLK, qd), lambda i: (i, j))
    rowv = lambda j: pl.BlockSpec((DEC_BLK, D_MODEL), lambda i: (i, j))
    lrow = pl.BlockSpec((None, DEC_BLK, qd), lambda i: (l, i, 0))
    lconst = lambda width: pl.BlockSpec((None, 1, width), lambda i: (l, 0, 0))
    cspec = pl.BlockSpec((None, DEC_BLK, NH_M, DV_M, DQK_M), lambda i: (l, i, 0, 0, 0))
    in_specs = [rowq(M_QM // qd), rowq(0), rowv(A_VM // D_MODEL), rowv(A_OM // D_MODEL), rowv(A_GM // D_MODEL),
                rowq(0), rowq(1), lrow, lrow, cspec,
                pl.BlockSpec((nbatch, qd), lambda i: (0, 0)),
                pl.BlockSpec((nbatch, D_MODEL), lambda i: (0, A_VM // D_MODEL)),
                lconst(qd), lconst(qd), lconst(D_MODEL)]
    args = [misc, km, pa, pa, pa, grep, grep, m_all, n_all, c_all, km, pa, bi, bf, nmh]
    aliases = {}
    if c_out_prev is not None:
        aliases = {len(args): 1}
        in_specs.append(pl.BlockSpec(memory_space=pl.ANY))
        args.append(c_out_prev)
    return pl.pallas_call(
        _mlstm_step_kernel,
        grid=(nbatch // DEC_BLK,),
        in_specs=in_specs,
        out_specs=[rowv(0), cspec, rowq(0), rowq(0)],
        out_shape=[jax.ShapeDtypeStruct((nbatch, D_MODEL), F32),
                   jax.ShapeDtypeStruct(c_all.shape, F32),
                   jax.ShapeDtypeStruct((nbatch, qd), F32),
                   jax.ShapeDtypeStruct((nbatch, qd), F32)],
        input_output_aliases=aliases,
        compiler_params=_cparams(1),
        name="mlstm_step",
    )(*args)


def _swa_step_kernel(q_ref, ga_ref, kn_ref, vn_ref, mk_ref, mv_ref, wk_ref, wv_ref, sink_ref, *rest):
    y_ref, ok_ref, ov_ref = rest[-3:]
    W = WINDOW
    hrow = lax.broadcasted_iota(jnp.int32, (N_HEADS, HD), 0)
    first = hrow < GQ
    wcol = lax.broadcasted_iota(jnp.int32, (N_HEADS, W), 1)
    wrow = lax.broadcasted_iota(jnp.int32, (N_KV * W, HD), 0)
    sink = sink_ref[...][:, 0:1]

    def per_kv(fn):
        a0, a1 = fn(0), fn(1)
        return jnp.where(first[:, 0:a0.shape[1]], a0, a1)

    for bb in range(DEC_BLK):
        q = q_ref[bb]
        qb = q.astype(BF16)
        rows_of = lambda ref, kv, n: ref[bb, pl.ds(kv, n, stride=N_KV), :].astype(BF16)
        kn, vn = kn_ref[bb:bb + 1, :], vn_ref[bb:bb + 1, :]
        kvs = lambda a, kv: a[:, kv * HD:(kv + 1) * HD]
        s_w = jnp.where(wcol >= 1, per_kv(lambda kv: _dot_nt(qb, rows_of(wk_ref, kv, W))), -jnp.inf)
        s_m = per_kv(lambda kv: _dot_nt(qb, rows_of(mk_ref, kv, N_META)))
        kn8 = per_kv(lambda kv: jnp.broadcast_to(kvs(kn, kv), (N_HEADS, HD)))
        vn8 = per_kv(lambda kv: jnp.broadcast_to(kvs(vn, kv), (N_HEADS, HD)))
        s_n = jnp.sum(q * kn8, axis=1, keepdims=True)
        mx = jnp.maximum(jnp.maximum(jnp.max(s_w, axis=1, keepdims=True), jnp.max(s_m, axis=1, keepdims=True)),
                         jnp.maximum(s_n, sink))
        p_w, p_m, p_n = jnp.exp(s_w - mx), jnp.exp(s_m - mx), jnp.exp(s_n - mx)
        denom = (jnp.sum(p_w, axis=1, keepdims=True) + jnp.sum(p_m, axis=1, keepdims=True) + p_n
                 + jnp.exp(sink - mx))
        o = (per_kv(lambda kv: _dot(p_w.astype(BF16), rows_of(wv_ref, kv, W)))
             + per_kv(lambda kv: _dot(p_m.astype(BF16), rows_of(mv_ref, kv, N_META)))
             + p_n * vn8)
        y_ref[bb] = (o / denom) * jax.nn.sigmoid(ga_ref[bb])
        for src_ref, new, dst_ref in ((wk_ref, kn, ok_ref), (wv_ref, vn, ov_ref)):
            shifted = pltpu.roll(src_ref[bb], N_KV * (W - 1), 0)
            for kv in range(N_KV):
                shifted = jnp.where(wrow == N_KV * (W - 1) + kv, kvs(new, kv), shifted)
            dst_ref[bb] = shifted


def _swa_step(l, q, ga, misc, mk_all, mv_all, wk_all, wv_all, ok_prev, ov_prev, sink8):
    nbatch = q.shape[0]
    kw = N_KV * HD
    hspec = pl.BlockSpec((DEC_BLK, N_HEADS, HD), lambda i: (i, 0, 0))
    mspec = pl.BlockSpec((None, DEC_BLK, N_KV * N_META, HD), lambda i: (l, i, 0, 0))
    wspec = pl.BlockSpec((None, DEC_BLK, N_KV * WINDOW, HD), lambda i: (l, i, 0, 0))
    in_specs = [hspec, hspec,
                pl.BlockSpec((DEC_BLK, kw), lambda i: (i, M_KA // kw)),
                pl.BlockSpec((DEC_BLK, kw), lambda i: (i, M_VA // kw)),
                mspec, mspec, wspec, wspec,
                pl.BlockSpec((None, N_HEADS, HD), lambda i: (l, 0, 0))]
    args = [q, ga, misc, misc, mk_all, mv_all, wk_all, wv_all, sink8]
    aliases = {}
    if ok_prev is not None:
        aliases = {len(args): 1, len(args) + 1: 2}
        in_specs += [pl.BlockSpec(memory_space=pl.ANY)] * 2
        args += [ok_prev, ov_prev]
    return pl.pallas_call(
        _swa_step_kernel,
        grid=(nbatch // DEC_BLK,),
        in_specs=in_specs,
        out_specs=[hspec, wspec, wspec],
        out_shape=[jax.ShapeDtypeStruct((nbatch, N_HEADS, HD), F32),
                   jax.ShapeDtypeStruct(wk_all.shape, F32),
                   jax.ShapeDtypeStruct(wv_all.shape, F32)],
        input_output_aliases=aliases,
        compiler_params=_cparams(1),
        name="swa_step",
    )(*args)


def _rope_tables(pos):
    half = ROT_DIM // 2
    inv = jnp.float32(ROPE_THETA) ** (-jnp.arange(half, dtype=F32) * 2.0 / ROT_DIM)
    ang = pos.astype(F32)[:, None] * inv[None, :]
    cos, sin = jnp.cos(ang), jnp.sin(ang)
    t = pos.shape[0]
    cs = jnp.concatenate([cos, cos, jnp.ones((t, HD - ROT_DIM), F32)], axis=1)
    s1 = jnp.concatenate([jnp.zeros((t, half), F32), sin, jnp.zeros((t, HD - ROT_DIM), F32)], axis=1)
    s2 = jnp.concatenate([-sin, jnp.zeros((t, HD - half), F32)], axis=1)
    return cs, s1, s2


def _pack_in_weights(w_in):
    edges = np.concatenate([[0], np.cumsum(IN_WIDTHS)])
    seg = [w_in[:, :, edges[i]:edges[i + 1]] for i in range(len(IN_WIDTHS))]
    q_m, k_m, v_m, ig, fg, o_m, q_a, k_a, v_a, g_m, g_a = seg
    w_a = jnp.concatenate([v_m, o_m, g_m, g_a], axis=2).astype(BF16)
    pad = jnp.zeros(w_in.shape[:2] + (B_TOT - B_G - 2 * NH_M,), w_in.dtype)
    w_b = jnp.concatenate([q_a, k_a, v_a, q_m, k_m, ig, fg, pad], axis=2).astype(BF16)
    w_bs = jnp.concatenate([w_b, jnp.repeat(ig, 128, axis=2).astype(BF16), jnp.repeat(fg, 128, axis=2).astype(BF16)],
                           axis=2)
    return w_a, w_b, w_bs


def _rep(a, width=128):
    return jnp.repeat(a, width, axis=1)


def kernel(x_prompt, x_sample, state_C, state_n, state_m, cache_meta_k, cache_meta_v, cache_win_k, cache_win_v,
           meta_tokens, norm_mix, w_in, b_igate, b_fgate, norm_mh, attn_sinks, w_out, norm_mlp, w_up, w_down,
           norm_final):
    B, S, _ = x_prompt.shape
    DB = x_sample.shape[0]
    W = cache_win_k.shape[2]
    assert (S % CHUNK, W, x_sample.shape[1], DB % DEC_BLK) == (0, WINDOW, 1, 0)
    assert (state_C.dtype, cache_win_k.dtype, cache_win_v.dtype) == (F32, F32, F32)
    nchunk = S // CHUNK
    kw = N_KV * HD
    big_tm = 1024

    w_a, w_b, w_bs = _pack_in_weights(w_in)
    w_out_b, w_up_b, w_down_b = w_out.astype(BF16), w_up.astype(BF16), w_down.astype(BF16)
    g_mix = norm_mix.astype(F32).reshape(DEPTH, 1, D_MODEL)
    g_mlp = norm_mlp.astype(F32).reshape(DEPTH, 1, D_MODEL)
    g_next = jnp.concatenate([g_mix, norm_final.astype(F32).reshape(1, 1, D_MODEL)], axis=0)
    nmh = norm_mh.astype(F32).reshape(DEPTH, 1, D_MODEL)
    sinks = attn_sinks.astype(F32)
    sink8 = jnp.broadcast_to(sinks[:, :, None], (DEPTH, N_HEADS, HD))
    bias8 = jnp.broadcast_to(jnp.concatenate([b_igate, b_fgate], axis=1).astype(F32)[:, :, None],
                             (DEPTH, 2 * NH_M, 128))
    bi_rep = _rep(b_igate.astype(F32)).reshape(DEPTH, 1, NH_M * 128)
    bf_rep = _rep(b_fgate.astype(F32)).reshape(DEPTH, 1, NH_M * 128)
    m_rep = jnp.repeat(state_m.astype(F32), 128, axis=2)
    n_all = state_n.astype(F32).reshape(DEPTH, DB, NH_M * DQK_M)
    rows_view = lambda a: a.reshape(DEPTH, DB, -1, HD)
    mk_all, mv_all = rows_view(cache_meta_k), rows_view(cache_meta_v)
    wk_all, wv_all = rows_view(cache_win_k), rows_view(cache_win_v)

    tabs_p = _rope_tables(N_META + jnp.arange(S, dtype=jnp.int32))
    pos_small = jnp.concatenate([jnp.full((DB,), PAST_LEN, jnp.int32), jnp.arange(N_META, dtype=jnp.int32),
                                 jnp.zeros((SMALL_ROWS - DB - N_META,), jnp.int32)])
    tabs_s = _rope_tables(pos_small)

    xp = x_prompt.reshape(B * S, D_MODEL)
    xs = jnp.concatenate([x_sample.reshape(DB, D_MODEL), meta_tokens.astype(F32),
                          jnp.zeros((SMALL_ROWS - DB - N_META, D_MODEL), F32)], axis=0)
    zero_state = (jnp.zeros((1, NH_M, DQK_M, DV_M), F32), jnp.zeros((1, NH_M, DQK_M, 128), F32),
                  jnp.zeros((1, NH_M, 8, 128), F32))

    outs = {k: [] for k in ("ct", "n", "m", "meta", "win", "sn", "sm")}
    sC = swk = swv = None
    hs_ = _norm(0, xs, g_mix, SMALL_ROWS)
    hp = _norm(0, xp, g_mix, big_tm)
    for l in range(DEPTH):
        pa_s = _inproj_a(l, hs_, w_a, SMALL_ROWS, F32)
        qa_s, misc_s, kt_s, gt_s, km_s, grep_s = _inproj_b(l, hs_, w_bs, tabs_s, SMALL_ROWS, small=True)
        y_meta, ct_m, n_m, m_m = _mix(l, sinks, qa_s, misc_s, pa_s, kt_s, gt_s, bias8, nmh, *zero_state, misc_s,
                                      nb=1, nblk=1, sub=1, row0=META_ROW0, valid_len=N_META, has_meta=False)
        ys_m, sC, sn, sm = _mlstm_step(l, misc_s, km_s, pa_s, grep_s, m_rep, n_all, state_C, sC, bi_rep, bf_rep, nmh)
        ys_a, swk, swv = _swa_step(
            l, qa_s[:DB].reshape(DB, N_HEADS, HD), pa_s[:DB, A_GA:A_GA + D_MODEL].reshape(DB, N_HEADS, HD),
            misc_s, mk_all, mv_all, wk_all, wv_all, swk, swv, sink8)
        y_s = jnp.concatenate([(ys_m + ys_a.reshape(DB, D_MODEL)).astype(BF16), y_meta], axis=0)
        res = _tail(l, y_s, xs, w_out_b, g_mlp, w_up_b, w_down_b, g_next, tm=SMALL_ROWS, tf=512)
        xs, hs_ = res[0], res[-1]

        pa = _inproj_a(l, hp, w_a, big_tm, BF16)
        qa, misc, kt, gt = _inproj_b(l, hp, w_b, tabs_p, big_tm, small=False)
        y, ct_f, n_f, m_f = _mix(l, sinks, qa, misc, pa, kt, gt, bias8, nmh, ct_m, n_m, m_m, misc_s,
                                 nb=B, nblk=nchunk, sub=MIX_SUB, row0=0, valid_len=CHUNK, has_meta=True)
        res = _tail(l, y, xp, w_out_b, g_mlp, w_up_b, w_down_b, g_next, tm=big_tm, tf=TAIL_TF)
        xp, hp = res[0], res[-1]

        for key, val in (("ct", ct_f), ("n", n_f), ("m", m_f), ("sn", sn), ("sm", sm),
                         ("meta", misc_s[META_ROW0:META_ROW0 + N_META, M_KA:M_VA + kw]),
                         ("win", misc.reshape(B, S, M_TOT)[:, S - WINDOW:, M_KA:M_VA + kw])):
            outs[key].append(val)

    y_prompt = xp.reshape(B, S, D_MODEL)
    y_sample = xs[:DB].reshape(DB, 1, D_MODEL)
    st = lambda k: jnp.stack(outs[k])
    p_c = jnp.swapaxes(st("ct"), 3, 4).astype(state_C.dtype)
    p_n = st("n")[..., 0].astype(state_n.dtype)
    p_m = st("m")[..., 0, 0].astype(state_m.dtype)
    meta = jnp.broadcast_to(st("meta").astype(F32)[:, None], (DEPTH, B, N_META, 2 * kw))
    win = st("win").astype(F32)
    kv_of = lambda a, lo: a[..., lo:lo + kw].reshape(a.shape[:-1] + (N_KV, HD))
    s_n = st("sn").reshape(DEPTH, DB, NH_M, DQK_M).astype(state_n.dtype)
    s_m = st("sm").reshape(DEPTH, DB, NH_M, 128)[..., 0].astype(state_m.dtype)
    return (y_prompt, y_sample, p_c, p_n, p_m, kv_of(meta, 0), kv_of(meta, kw), kv_of(win, 0), kv_of(win, kw),
            sC, s_n, s_m, swk.reshape(cache_win_k.shape), swv.reshape(cache_win_v.shape))
```

```python
import functools

import jax
import jax.numpy as jnp
import numpy as np
from jax import lax
from jax.experimental import pallas as pl
from jax.experimental.pallas import tpu as pltpu

F32 = jnp.float32
BF16 = jnp.bfloat16

D_MODEL = 1024
DEPTH = 4
PAST_LEN = 8192
N_META = 16
NH_M = 4
DV_M = 256
DQK_M = 128
CHUNK = 128
N_HEADS = 8
N_KV = 2
GQ = N_HEADS // N_KV
HD = 128
WINDOW = 128
ROT_DIM = 32
ROPE_THETA = 500000.0
D_FF = 4 * D_MODEL
EPS = 1e-6
IN_WIDTHS = (NH_M * DQK_M, NH_M * DQK_M, NH_M * DV_M, NH_M, NH_M, NH_M * DV_M,
             N_HEADS * HD, N_KV * HD, N_KV * HD, D_MODEL, D_MODEL)

A_VM, A_OM, A_GM, A_GA, A_TOT = 0, 1024, 2048, 3072, 4096
A_TN = 2048
B_QA, B_KA, B_VA, B_QM, B_KM, B_G, B_TOT = 0, 1024, 1280, 1536, 2048, 2560, 2688
M_KA, M_VA, M_QM, M_TOT = 0, 256, 512, 1024
TAIL_TF = 1024
MIX_SUB = 1
SMALL_ROWS = 256
META_ROW0 = 128
DEC_BLK = 8
VMEM_LIMIT = 56 * 1024 * 1024


def _cparams(n_axes):
    return pltpu.CompilerParams(dimension_semantics=("arbitrary",) * n_axes, vmem_limit_bytes=VMEM_LIMIT)


def _rms(xf, gain):
    ms = jnp.mean(xf * xf, axis=-1, keepdims=True)
    return xf * lax.rsqrt(ms + EPS) * gain


def _dot(a, b):
    return jnp.dot(a, b, preferred_element_type=F32)


def _dot_nt(a, b):
    return lax.dot_general(a, b, (((1,), (1,)), ((), ())), preferred_element_type=F32)


def _log_sigmoid(x):
    return jnp.minimum(x, 0.0) - jnp.log1p(jnp.exp(-jnp.abs(x)))


def _sigmoid_bf16(x):
    half = jnp.asarray(0.5, x.dtype)
    return jnp.tanh(x * half) * half + half


def _norm_kernel(x_ref, gain_ref, o_ref):
    o_ref[...] = _rms(x_ref[...], gain_ref[...]).astype(BF16)


def _norm(l, x, gains, tm):
    rows = x.shape[0]
    return pl.pallas_call(
        _norm_kernel,
        grid=(rows // tm,),
        in_specs=[pl.BlockSpec((tm, D_MODEL), lambda i: (i, 0)),
                  pl.BlockSpec((None, 1, D_MODEL), lambda i: (l, 0, 0))],
        out_specs=pl.BlockSpec((tm, D_MODEL), lambda i: (i, 0)),
        out_shape=jax.ShapeDtypeStruct((rows, D_MODEL), BF16),
        compiler_params=_cparams(1),
        name="norm",
    )(x, gains)


def _inproj_a_kernel(h_ref, w_ref, o_ref):
    o_ref[...] = _dot(h_ref[...], w_ref[...]).astype(o_ref.dtype)


def _inproj_a(l, h, w_a, tm, out_dtype):
    rows = h.shape[0]
    return pl.pallas_call(
        _inproj_a_kernel,
        grid=(rows // tm, A_TOT // A_TN),
        in_specs=[pl.BlockSpec((tm, D_MODEL), lambda i, j: (i, 0)),
                  pl.BlockSpec((None, D_MODEL, A_TN), lambda i, j: (l, 0, j))],
        out_specs=pl.BlockSpec((tm, A_TN), lambda i, j: (i, j)),
        out_shape=jax.ShapeDtypeStruct((rows, A_TOT), out_dtype),
        compiler_params=_cparams(2),
        name="inproj_a",
    )(h, w_a)


def _inproj_b_kernel(h_ref, w_ref, cs_ref, s1_ref, s2_ref, qa_ref, misc_ref, kt_ref, gt_ref, *row_form):
    h = h_ref[...]

    def rope(seg):
        return (seg * cs_ref[...] + pltpu.roll(seg, 16, 1) * s1_ref[...]
                + pltpu.roll(seg, HD - 16, 1) * s2_ref[...])

    acc = _dot(h, w_ref[:, B_QA:B_KA])
    for c in range(N_HEADS):
        cs = slice(c * HD, (c + 1) * HD)
        qa_ref[:, cs] = (rope(acc[:, cs]) * (HD ** -0.5)).astype(qa_ref.dtype)

    acc = _dot(h, w_ref[:, B_KA:B_KM])
    for c in range(N_KV):
        cs = slice(M_KA + c * HD, M_KA + (c + 1) * HD)
        misc_ref[:, cs] = rope(acc[:, cs]).astype(misc_ref.dtype)
    misc_ref[:, M_VA:M_QM] = acc[:, M_VA:M_QM].astype(misc_ref.dtype)
    misc_ref[:, M_QM:M_TOT] = (acc[:, M_QM:M_TOT] * (DQK_M ** -0.5)).astype(misc_ref.dtype)

    acc = _dot(h, w_ref[:, B_KM:B_TOT])
    kt_ref[...] = acc[:, 0:NH_M * DQK_M].T.astype(BF16)
    gt_ref[...] = acc[:, NH_M * DQK_M:B_TOT - B_KM].T[0:2 * NH_M]

    if row_form:
        km_ref, grep_ref = row_form
        km_ref[...] = acc[:, 0:NH_M * DQK_M]
        for c in range(2 * NH_M):
            col = NH_M * DQK_M + c
            grep_ref[:, c * 128:(c + 1) * 128] = jnp.broadcast_to(acc[:, col:col + 1], (acc.shape[0], 128))


G_REP = 2 * NH_M * 128


def _inproj_b(l, h, w_b, tabs, tm, small):
    rows = h.shape[0]
    tab_blocks = tabs[0].shape[0] // tm
    tab_spec = pl.BlockSpec((tm, HD), lambda i: (i % tab_blocks, 0))
    dt = F32 if small else BF16
    row_spec = lambda width: pl.BlockSpec((tm, width), lambda i: (i, 0))
    out_specs = [row_spec(D_MODEL), row_spec(M_TOT),
                 pl.BlockSpec((NH_M * DQK_M, tm), lambda i: (0, i)),
                 pl.BlockSpec((2 * NH_M, tm), lambda i: (0, i))]
    out_shape = [jax.ShapeDtypeStruct((rows, D_MODEL), dt),
                 jax.ShapeDtypeStruct((rows, M_TOT), dt),
                 jax.ShapeDtypeStruct((NH_M * DQK_M, rows), BF16),
                 jax.ShapeDtypeStruct((2 * NH_M, rows), F32)]
    if small:
        out_specs += [row_spec(NH_M * DQK_M), row_spec(G_REP)]
        out_shape += [jax.ShapeDtypeStruct((rows, NH_M * DQK_M), F32), jax.ShapeDtypeStruct((rows, G_REP), F32)]
    return pl.pallas_call(
        _inproj_b_kernel,
        grid=(rows // tm,),
        in_specs=[pl.BlockSpec((tm, D_MODEL), lambda i: (i, 0)),
                  pl.BlockSpec((None, D_MODEL, w_b.shape[2]), lambda i: (l, 0, 0)),
                  tab_spec, tab_spec, tab_spec],
        out_specs=out_specs,
        out_shape=out_shape,
        compiler_params=_cparams(1),
        name="inproj_b",
    )(h, w_b, *tabs)


def _mix_kernel(sink_ref, misc_ref, miscp_ref, pa_ref, qa_ref, kt_ref, g_ref, bias_ref, nmh_ref,
                ct0_ref, n0_ref, m0_ref, mk_ref, mv_ref,
                y_ref, ctf_ref, nf_ref, mf_ref, ct_s, n_s, m_s, *, l, nc, sub, valid_len, has_meta):
    c = pl.program_id(1)
    L = CHUNK

    @pl.when(c == 0)
    def _():
        ct_s[...] = ct0_ref[...]
        n_s[...] = n0_ref[...]
        m_s[...] = m0_ref[...]

    row = lax.broadcasted_iota(jnp.int32, (L, L), 0)
    col = lax.broadcasted_iota(jnp.int32, (L, L), 1)
    tril = col <= row
    eye = col == row

    gates = g_ref[...] + bias_ref[:, 0:1]
    ig_sub = gates[0:NH_M]
    lf_sub = _log_sigmoid(gates[NH_M:2 * NH_M])
    if valid_len < L:
        ok = lax.broadcasted_iota(jnp.int32, (NH_M, sub * L), 1) < valid_len
        ig_sub = jnp.where(ok, ig_sub, -jnp.inf)
        lf_sub = jnp.where(ok, lf_sub, 0.0)

    assert sub == 1
    ym, gates_out = {}, {}

    def mlstm_head(h):
        ig = ig_sub[h:h + 1]
        lf = lf_sub[h:h + 1]
        m_prev = m_s[h, 0:1, 0:1]
        b_col = jnp.sum(jnp.where(tril, lf, 0.0), axis=1, keepdims=True)
        b_row = jnp.sum(jnp.where(eye, b_col, 0.0), axis=0, keepdims=True)
        yield
        a_col = b_col + m_prev
        dmat = jnp.where(tril, b_col - b_row + ig, -jnp.inf)
        m_col = jnp.maximum(a_col, jnp.max(dmat, axis=1, keepdims=True))
        w_inter = jnp.exp(a_col - m_col)
        wmat = jnp.exp(dmat - m_col)
        q = misc_ref[:, M_QM + h * DQK_M:M_QM + (h + 1) * DQK_M].astype(BF16)
        kt = kt_ref[h * DQK_M:(h + 1) * DQK_M, :]
        v = pa_ref[:, A_VM + h * DV_M:A_VM + (h + 1) * DV_M].astype(BF16)
        ct = ct_s[h]
        nm = n_s[h]
        yield
        qk = _dot(q, kt)
        inter = _dot(q, ct.astype(BF16))
        qn = _dot(q, nm.astype(BF16))[:, 0:1]
        yield
        s = qk * wmat
        s_b = s.astype(BF16)
        yield
        m_new = m_col[L - 1:L]
        decay = w_inter[L - 1:L]
        g_row = jnp.exp(b_col[L - 1:L] - b_row + ig - m_new)
        ktg = kt.astype(F32) * g_row
        ktg_b = ktg.astype(BF16)
        yield
        intra = _dot(s_b, v)
        upd = _dot(ktg_b, v)
        yield
        num = w_inter * inter + intra
        den = w_inter * qn + jnp.sum(s, axis=1, keepdims=True)
        hh = num / jnp.maximum(jnp.abs(den), jnp.exp(-m_col))
        ct_s[h] = decay * ct + upd
        n_s[h] = decay * nm + jnp.sum(ktg, axis=1, keepdims=True)
        m_s[h] = jnp.broadcast_to(m_new, (8, 128))
        yield
        hs = slice(h * DV_M, (h + 1) * DV_M)
        hn = hh * lax.rsqrt(jnp.mean(hh * hh, axis=1, keepdims=True) + EPS) * nmh_ref[:, hs]
        ym[h] = gates_out[h] * hn

    def gate_chain():
        yield
        for h in range(NH_M):
            hs = slice(h * DV_M, (h + 1) * DV_M)
            gates_out[h] = (_sigmoid_bf16(pa_ref[:, A_GM + h * DV_M:A_GM + (h + 1) * DV_M])
                            * _sigmoid_bf16(pa_ref[:, A_OM + h * DV_M:A_OM + (h + 1) * DV_M])
                            ).astype(F32)
            if h == 1:
                yield
        yield
        gates_out["a"] = _sigmoid_bf16(pa_ref[:, A_GA:A_GA + D_MODEL]).astype(F32)

    chains = [mlstm_head(h) for h in range(NH_M)] + [gate_chain()]
    chains += [_swa_chain(c, kv, sink_ref.at[l], qa_ref, misc_ref, miscp_ref, gates_out, mk_ref, mv_ref, ym, y_ref,
                          has_meta) for kv in range(N_KV)]
    while chains:
        alive = []
        for chain in chains:
            try:
                next(chain)
                alive.append(chain)
            except StopIteration:
                pass
        chains = alive

    @pl.when(c == nc - 1)
    def _():
        ctf_ref[0] = ct_s[...]
        nf_ref[0] = n_s[...]
        mf_ref[0] = m_s[...]


def _mix(l, sinks, qa, misc, pa, kt, gt, bias8, nmh, ct0, n0, m0, misc_small, nb, nblk, sub, row0, valid_len,
         has_meta):
    nc = nblk // sub
    rows = sub * CHUNK
    r0 = row0 // rows
    rb = lambda b, c: r0 + b * nc + c
    rp = lambda b, c: (r0 + b * nc + c) * sub - jnp.minimum(c, 1)
    kw = N_KV * HD
    first4 = lambda b, c: (0, 0, 0, 0)
    st4 = lambda b, c: (b, 0, 0, 0)
    return pl.pallas_call(
        functools.partial(_mix_kernel, l=l, nc=nc, sub=sub, valid_len=valid_len, has_meta=has_meta),
        grid=(nb, nc),
        in_specs=[pl.BlockSpec(memory_space=pltpu.SMEM),
                  pl.BlockSpec((rows, M_TOT), lambda b, c: (rb(b, c), 0)),
                  pl.BlockSpec((CHUNK, 2 * kw), lambda b, c: (rp(b, c), 0)),
                  pl.BlockSpec((rows, A_TOT), lambda b, c: (rb(b, c), 0)),
                  pl.BlockSpec((rows, D_MODEL), lambda b, c: (rb(b, c), 0)),
                  pl.BlockSpec((NH_M * DQK_M, rows), lambda b, c: (0, rb(b, c))),
                  pl.BlockSpec((2 * NH_M, rows), lambda b, c: (0, rb(b, c))),
                  pl.BlockSpec((None, 2 * NH_M, 128), lambda b, c: (l, 0, 0)),
                  pl.BlockSpec((None, 1, D_MODEL), lambda b, c: (l, 0, 0)),
                  pl.BlockSpec((None, NH_M, DQK_M, DV_M), first4),
                  pl.BlockSpec((None, NH_M, DQK_M, 128), first4),
                  pl.BlockSpec((None, NH_M, 8, 128), first4),
                  pl.BlockSpec((N_META, kw), lambda b, c: (META_ROW0 // N_META, M_KA // kw)),
                  pl.BlockSpec((N_META, kw), lambda b, c: (META_ROW0 // N_META, M_VA // kw))],
        out_specs=[pl.BlockSpec((rows, D_MODEL), lambda b, c: (b * nc + c, 0)),
                   pl.BlockSpec((1, NH_M, DQK_M, DV_M), st4),
                   pl.BlockSpec((1, NH_M, DQK_M, 128), st4),
                   pl.BlockSpec((1, NH_M, 8, 128), st4)],
        out_shape=[jax.ShapeDtypeStruct((nb * nc * rows, D_MODEL), BF16),
                   jax.ShapeDtypeStruct((nb, NH_M, DQK_M, DV_M), F32),
                   jax.ShapeDtypeStruct((nb, NH_M, DQK_M, 128), F32),
                   jax.ShapeDtypeStruct((nb, NH_M, 8, 128), F32)],
        scratch_shapes=[pltpu.VMEM((NH_M, DQK_M, DV_M), F32),
                        pltpu.VMEM((NH_M, DQK_M, 128), F32),
                        pltpu.VMEM((NH_M, 8, 128), F32)],
        compiler_params=_cparams(2),
        name="mix",
    )(sinks, misc, misc, pa, qa, kt, gt, bias8, nmh, ct0, n0, m0, misc_small, misc_small)


def _swa_chain(c, kv, sink_ref, q_ref, misc_ref, miscp_ref, gates_out, mk_ref, mv_ref, ym, y_ref, has_meta):
    L = WINDOW
    R = GQ * L
    row = lax.broadcasted_iota(jnp.int32, (R, L), 0)
    col = lax.broadcasted_iota(jnp.int32, (R, L), 1)
    tril = col <= (row & (L - 1))
    head_of_row = lax.shift_right_logical(lax.broadcasted_iota(jnp.int32, (R, 1), 0), L.bit_length() - 1)
    with_ones = lambda v: jnp.concatenate([v, jnp.ones((v.shape[0], HD), BF16)], axis=1)
    prev_ok = jnp.full((R, L), c, jnp.int32) > 0
    ka = slice(M_KA + kv * HD, M_KA + (kv + 1) * HD)
    va = slice(M_VA + kv * HD, M_VA + (kv + 1) * HD)
    ks = slice(kv * HD, (kv + 1) * HD)
    heads = [kv * GQ + g for g in range(GQ)]
    q = jnp.concatenate([q_ref[:, hd * HD:(hd + 1) * HD].astype(BF16) for hd in heads], axis=0)
    sink = jnp.zeros((R, 1), F32)
    for g, hd in enumerate(heads):
        sink = jnp.where(head_of_row == g, sink_ref[hd], sink)
    yield
    yield
    s_cur =_dot_nt(q, misc_ref[:, ka].astype(BF16))
    s_prev = _dot_nt(q, miscp_ref[:, ka].astype(BF16))
    if has_meta:
        s_m = _dot_nt(q, mk_ref[:, ks].astype(BF16))
    yield
    band = jnp.where(tril, s_cur, jnp.where(prev_ok, s_prev, -jnp.inf))
    mx = jnp.maximum(jnp.max(band, axis=1, keepdims=True), sink)
    if has_meta:
        mx = jnp.maximum(mx, jnp.max(s_m, axis=1, keepdims=True))
        p_m = jnp.exp(s_m - mx).astype(BF16)
    yield
    p = jnp.exp(band - mx)
    p_cur = jnp.where(tril, p, 0.0).astype(BF16)
    p_prev = jnp.where(tril, 0.0, p).astype(BF16)
    yield
    oe = (_dot(p_cur, with_ones(misc_ref[:, va].astype(BF16)))
          + _dot(p_prev, with_ones(miscp_ref[:, va].astype(BF16))))
    if has_meta:
        oe = oe + _dot(p_m, with_ones(mv_ref[:, ks].astype(BF16)))
    yield
    denom = oe[:, HD:HD + 1] + jnp.exp(sink - mx)
    o = oe[:, 0:HD] / denom
    yield
    for g, hd in enumerate(heads):
        hs = slice(hd * HD, (hd + 1) * HD)
        ym_h = ym[(hd * HD) // DV_M][:, (hd * HD) % DV_M:(hd * HD) % DV_M + HD]
        y_ref[:, hs] = (ym_h + o[g * L:(g + 1) * L] * gates_out["a"][:, hs]).astype(BF16)


def _tail_kernel(y_ref, x_ref, wo_ref, gn_ref, wu_ref, wd_ref, gnext_ref, o_ref, *rest, nj, last_layer):
    hf_ref = rest[-1]
    j = pl.program_id(1)

    @pl.when(j == 0)
    def _():
        x1 = x_ref[...] + _dot(y_ref[...], wo_ref[...])
        o_ref[...] = x1
        hf_ref[...] = _rms(x1, gn_ref[...]).astype(BF16)

    a = jnp.square(jnp.maximum(_dot(hf_ref[...], wu_ref[...]), 0.0)).astype(BF16)
    o_ref[...] += _dot(a, wd_ref[...])

    @pl.when(j == nj - 1)
    def _():
        if last_layer:
            o_ref[...] = _rms(o_ref[...], gnext_ref[...])
        else:
            rest[0][...] = _rms(o_ref[...], gnext_ref[...]).astype(BF16)


def _tail(l, y, x, w_out, g_mlp, w_up, w_down, g_next, tm, tf):
    rows = x.shape[0]
    nj = D_FF // tf
    last_layer = l == DEPTH - 1
    row_spec = pl.BlockSpec((tm, D_MODEL), lambda i, j: (i, 0))
    out_specs, out_shape = [row_spec], [jax.ShapeDtypeStruct((rows, D_MODEL), F32)]
    if not last_layer:
        out_specs.append(row_spec)
        out_shape.append(jax.ShapeDtypeStruct((rows, D_MODEL), BF16))
    return pl.pallas_call(
        functools.partial(_tail_kernel, nj=nj, last_layer=last_layer),
        grid=(rows // tm, nj),
        in_specs=[row_spec, row_spec,
                  pl.BlockSpec((None, D_MODEL, D_MODEL), lambda i, j: (l, 0, 0)),
                  pl.BlockSpec((None, 1, D_MODEL), lambda i, j: (l, 0, 0)),
                  pl.BlockSpec((None, D_MODEL, tf), lambda i, j: (l, 0, j)),
                  pl.BlockSpec((None, tf, D_MODEL), lambda i, j: (l, j, 0)),
                  pl.BlockSpec((None, 1, D_MODEL), lambda i, j: (l + 1, 0, 0))],
        out_specs=out_specs,
        out_shape=out_shape,
        scratch_shapes=[pltpu.VMEM((tm, D_MODEL), BF16)],
        compiler_params=_cparams(2),
        name="tail",
    )(y, x, w_out, g_mlp, w_up, w_down, g_next)


def _mlstm_step_kernel(q_ref, k_ref, v_ref, o_ref, gm_ref, ig_ref, fg_ref, m_ref, n_ref, c_ref,
                       kall_ref, vall_ref, bi_ref, bf_ref, nmh_ref, *rest):
    y_ref, co_ref, no_ref, mo_ref = rest[-4:]
    i = pl.program_id(0)
    nb = DEC_BLK
    row8 = lax.broadcasted_iota(jnp.int32, (nb, DV_M), 0)
    rowb = lax.broadcasted_iota(jnp.int32, (kall_ref.shape[0], DQK_M), 0)
    for h in range(NH_M):
        ds = slice(h * DQK_M, (h + 1) * DQK_M)
        vs = slice(h * DV_M, (h + 1) * DV_M)
        q, k, v, nrm = q_ref[:, ds], k_ref[:, ds], v_ref[:, vs], n_ref[:, ds]
        ig = ig_ref[:, ds] + bi_ref[:, ds]
        lf = _log_sigmoid(fg_ref[:, ds] + bf_ref[:, ds])
        m_prev = m_ref[:, ds]
        m_t = jnp.maximum(lf + m_prev, ig)
        w_inter = jnp.exp(lf + m_prev - m_t)
        w_in = jnp.exp(ig - m_t)
        s = jnp.sum(q * k, axis=1, keepdims=True) * w_in
        qb = q.astype(BF16)
        kall = kall_ref[:, ds].astype(BF16)
        vt = vall_ref[:, vs].T.astype(BF16)
        cq = jnp.zeros((nb, DV_M), F32)
        for bb in range(nb):
            c_old = c_ref[bb, h]
            cq = jnp.where(row8 == bb, _dot_nt(qb, c_old.astype(BF16)), cq)
            outer = _dot(vt, jnp.where(rowb == i * nb + bb, kall, jnp.zeros_like(kall)))
            co_ref[bb, h] = w_inter[bb:bb + 1, :] * c_old + w_in[bb:bb + 1, :] * outer
        num = w_inter[:, 0:1] * cq + s[:, 0:1] * v
        den = w_inter * jnp.sum(nrm * q, axis=1, keepdims=True) + s
        hh = num / jnp.maximum(jnp.abs(den), jnp.exp(-m_t))[:, 0:1]
        no_ref[:, ds] = w_inter * nrm + w_in * k
        mo_ref[:, ds] = m_t
        hn = hh * lax.rsqrt(jnp.mean(hh * hh, axis=1, keepdims=True) + EPS) * nmh_ref[:, vs]
        y_ref[:, vs] = jax.nn.sigmoid(gm_ref[:, vs]) * (jax.nn.sigmoid(o_ref[:, vs]) * hn)


def _mlstm_step(l, misc, km, pa, grep, m_all, n_all, c_all, c_out_prev, bi, bf, nmh):
    nbatch = c_all.shape[1]
    qd = NH_M * DQK_M
    rowq = lambda j: pl.BlockSpec((DEC_BLK, qd), lambda i: (i, j))
    rowv = lambda j: pl.BlockSpec((DEC_BLK, D_MODEL), lambda i: (i, j))
    lrow = pl.BlockSpec((None, DEC_BLK, qd), lambda i: (l, i, 0))
    lconst = lambda width: pl.BlockSpec((None, 1, width), lambda i: (l, 0, 0))
    cspec = pl.BlockSpec((None, DEC_BLK, NH_M, DV_M, DQK_M), lambda i: (l, i, 0, 0, 0))
    in_specs = [rowq(M_QM // qd), rowq(0), rowv(A_VM // D_MODEL), rowv(A_OM // D_MODEL), rowv(A_GM // D_MODEL),
                rowq(0), rowq(1), lrow, lrow, cspec,
                pl.BlockSpec((nbatch, qd), lambda i: (0, 0)),
                pl.BlockSpec((nbatch, D_MODEL), lambda i: (0, A_VM // D_MODEL)),
                lconst(qd), lconst(qd), lconst(D_MODEL)]
    args = [misc, km, pa, pa, pa, grep, grep, m_all, n_all, c_all, km, pa, bi, bf, nmh]
    aliases = {}
    if c_out_prev is not None:
        aliases = {len(args): 1}
        in_specs.append(pl.BlockSpec(memory_space=pl.ANY))
        args.append(c_out_prev)
    return pl.pallas_call(
        _mlstm_step_kernel,
        grid=(nbatch // DEC_BLK,),
        in_specs=in_specs,
        out_specs=[rowv(0), cspec, rowq(0), rowq(0)],
        out_shape=[jax.ShapeDtypeStruct((nbatch, D_MODEL), F32),
                   jax.ShapeDtypeStruct(c_all.shape, F32),
                   jax.ShapeDtypeStruct((nbatch, qd), F32),
                   jax.ShapeDtypeStruct((nbatch, qd), F32)],
        input_output_aliases=aliases,
        compiler_params=_cparams(1),
        name="mlstm_step",
    )(*args)


def _swa_step_kernel(q_ref, ga_ref, kn_ref, vn_ref, mk_ref, mv_ref, wk_ref, wv_ref, sink_ref, *rest):
    y_ref, ok_ref, ov_ref = rest[-3:]
    W = WINDOW
    hrow = lax.broadcasted_iota(jnp.int32, (N_HEADS, HD), 0)
    first = hrow < GQ
    wcol = lax.broadcasted_iota(jnp.int32, (N_HEADS, W), 1)
    wrow = lax.broadcasted_iota(jnp.int32, (N_KV * W, HD), 0)
    sink = sink_ref[...][:, 0:1]

    def per_kv(fn):
        a0, a1 = fn(0), fn(1)
        return jnp.where(first[:, 0:a0.shape[1]], a0, a1)

    for bb in range(DEC_BLK):
        q = q_ref[bb]
        qb = q.astype(BF16)
        rows_of = lambda ref, kv, n: ref[bb, pl.ds(kv, n, stride=N_KV), :].astype(BF16)
        kn, vn = kn_ref[bb:bb + 1, :], vn_ref[bb:bb + 1, :]
        kvs = lambda a, kv: a[:, kv * HD:(kv + 1) * HD]
        s_w = jnp.where(wcol >= 1, per_kv(lambda kv: _dot_nt(qb, rows_of(wk_ref, kv, W))), -jnp.inf)
        s_m = per_kv(lambda kv: _dot_nt(qb, rows_of(mk_ref, kv, N_META)))
        kn8 = per_kv(lambda kv: jnp.broadcast_to(kvs(kn, kv), (N_HEADS, HD)))
        vn8 = per_kv(lambda kv: jnp.broadcast_to(kvs(vn, kv), (N_HEADS, HD)))
        s_n = jnp.sum(q * kn8, axis=1, keepdims=True)
        mx = jnp.maximum(jnp.maximum(jnp.max(s_w, axis=1, keepdims=True), jnp.max(s_m, axis=1, keepdims=True)),
                         jnp.maximum(s_n, sink))
        p_w, p_m, p_n = jnp.exp(s_w - mx), jnp.exp(s_m - mx), jnp.exp(s_n - mx)
        denom = (jnp.sum(p_w, axis=1, keepdims=True) + jnp.sum(p_m, axis=1, keepdims=True) + p_n
                 + jnp.exp(sink - mx))
        o = (per_kv(lambda kv: _dot(p_w.astype(BF16), rows_of(wv_ref, kv, W)))
             + per_kv(lambda kv: _dot(p_m.astype(BF16), rows_of(mv_ref, kv, N_META)))
             + p_n * vn8)
        y_ref[bb] = (o / denom) * jax.nn.sigmoid(ga_ref[bb])
        for src_ref, new, dst_ref in ((wk_ref, kn, ok_ref), (wv_ref, vn, ov_ref)):
            shifted = pltpu.roll(src_ref[bb], N_KV * (W - 1), 0)
            for kv in range(N_KV):
                shifted = jnp.where(wrow == N_KV * (W - 1) + kv, kvs(new, kv), shifted)
            dst_ref[bb] = shifted


def _swa_step(l, q, ga, misc, mk_all, mv_all, wk_all, wv_all, ok_prev, ov_prev, sink8):
    nbatch = q.shape[0]
    kw = N_KV * HD
    hspec = pl.BlockSpec((DEC_BLK, N_HEADS, HD), lambda i: (i, 0, 0))
    mspec = pl.BlockSpec((None, DEC_BLK, N_KV * N_META, HD), lambda i: (l, i, 0, 0))
    wspec = pl.BlockSpec((None, DEC_BLK, N_KV * WINDOW, HD), lambda i: (l, i, 0, 0))
    in_specs = [hspec, hspec,
                pl.BlockSpec((DEC_BLK, kw), lambda i: (i, M_KA // kw)),
                pl.BlockSpec((DEC_BLK, kw), lambda i: (i, M_VA // kw)),
                mspec, mspec, wspec, wspec,
                pl.BlockSpec((None, N_HEADS, HD), lambda i: (l, 0, 0))]
    args = [q, ga, misc, misc, mk_all, mv_all, wk_all, wv_all, sink8]
    aliases = {}
    if ok_prev is not None:
        aliases = {len(args): 1, len(args) + 1: 2}
        in_specs += [pl.BlockSpec(memory_space=pl.ANY)] * 2
        args += [ok_prev, ov_prev]
    return pl.pallas_call(
        _swa_step_kernel,
        grid=(nbatch // DEC_BLK,),
        in_specs=in_specs,
        out_specs=[hspec, wspec, wspec],
        out_shape=[jax.ShapeDtypeStruct((nbatch, N_HEADS, HD), F32),
                   jax.ShapeDtypeStruct(wk_all.shape, F32),
                   jax.ShapeDtypeStruct(wv_all.shape, F32)],
        input_output_aliases=aliases,
        compiler_params=_cparams(1),
        name="swa_step",
    )(*args)


def _rope_tables(pos):
    half = ROT_DIM // 2
    inv = jnp.float32(ROPE_THETA) ** (-jnp.arange(half, dtype=F32) * 2.0 / ROT_DIM)
    ang = pos.astype(F32)[:, None] * inv[None, :]
    cos, sin = jnp.cos(ang), jnp.sin(ang)
    t = pos.shape[0]
    cs = jnp.concatenate([cos, cos, jnp.ones((t, HD - ROT_DIM), F32)], axis=1)
    s1 = jnp.concatenate([jnp.zeros((t, half), F32), sin, jnp.zeros((t, HD - ROT_DIM), F32)], axis=1)
    s2 = jnp.concatenate([-sin, jnp.zeros((t, HD - half), F32)], axis=1)
    return cs, s1, s2


PACK_ROWS = 128


def _pack_kernel(w_ref, wa_ref, wb_ref):
    edges = np.concatenate([[0], np.cumsum(IN_WIDTHS)])
    q_m, k_m, v_m, ig, fg, o_m, q_a, k_a, v_a, g_m, g_a = [(int(edges[i]), int(edges[i + 1])) for i in range(11)]
    x = w_ref[...]

    def put(dst_ref, dst0, seg):
        lo, hi = seg
        dst_ref[:, dst0:dst0 + hi - lo] = x[:, lo:hi].astype(BF16)

    for dst0, seg in ((A_VM, v_m), (A_OM, o_m), (A_GM, g_m), (A_GA, g_a)):
        put(wa_ref, dst0, seg)
    for dst0, seg in ((B_QA, q_a), (B_KA, k_a), (B_VA, v_a), (B_QM, q_m), (B_KM, k_m)):
        put(wb_ref, dst0, seg)
    tile = x[:, ig[0]:ig[0] + 128]
    lane = lax.broadcasted_iota(jnp.int32, tile.shape, 1)
    wb_ref[:, B_G:B_TOT] = jnp.where(lane < 2 * NH_M, tile, 0.0).astype(BF16)


def _pack_in_weights(w_in):
    assert IN_WIDTHS[3] == IN_WIDTHS[4] == NH_M and sum(IN_WIDTHS[:3]) % 128 == 0
    depth, rows, cols = w_in.shape
    return pl.pallas_call(
        _pack_kernel,
        grid=(depth, rows // PACK_ROWS),
        in_specs=[pl.BlockSpec((None, PACK_ROWS, cols), lambda l, i: (l, i, 0))],
        out_specs=[pl.BlockSpec((None, PACK_ROWS, A_TOT), lambda l, i: (l, i, 0)),
                   pl.BlockSpec((None, PACK_ROWS, B_TOT), lambda l, i: (l, i, 0))],
        out_shape=[jax.ShapeDtypeStruct((depth, rows, A_TOT), BF16),
                   jax.ShapeDtypeStruct((depth, rows, B_TOT), BF16)],
        compiler_params=_cparams(2),
        name="pack_w_in",
    )(w_in)


def _rep(a, width=128):
    return jnp.repeat(a, width, axis=1)


def kernel(x_prompt, x_sample, state_C, state_n, state_m, cache_meta_k, cache_meta_v, cache_win_k, cache_win_v,
           meta_tokens, norm_mix, w_in, b_igate, b_fgate, norm_mh, attn_sinks, w_out, norm_mlp, w_up, w_down,
           norm_final):
    B, S, _ = x_prompt.shape
    DB = x_sample.shape[0]
    W = cache_win_k.shape[2]
    assert (S % CHUNK, W, x_sample.shape[1], DB % DEC_BLK) == (0, WINDOW, 1, 0)
    assert (state_C.dtype, cache_win_k.dtype, cache_win_v.dtype) == (F32, F32, F32)
    nchunk = S // CHUNK
    kw = N_KV * HD
    big_tm = 1024

    w_a, w_b = _pack_in_weights(w_in)
    w_out_b, w_up_b, w_down_b = w_out.astype(BF16), w_up.astype(BF16), w_down.astype(BF16)
    g_mix = norm_mix.astype(F32).reshape(DEPTH, 1, D_MODEL)
    g_mlp = norm_mlp.astype(F32).reshape(DEPTH, 1, D_MODEL)
    g_next = jnp.concatenate([g_mix, norm_final.astype(F32).reshape(1, 1, D_MODEL)], axis=0)
    nmh = norm_mh.astype(F32).reshape(DEPTH, 1, D_MODEL)
    sinks = attn_sinks.astype(F32)
    sink8 = jnp.broadcast_to(sinks[:, :, None], (DEPTH, N_HEADS, HD))
    bias8 = jnp.broadcast_to(jnp.concatenate([b_igate, b_fgate], axis=1).astype(F32)[:, :, None],
                             (DEPTH, 2 * NH_M, 128))
    bi_rep = _rep(b_igate.astype(F32)).reshape(DEPTH, 1, NH_M * 128)
    bf_rep = _rep(b_fgate.astype(F32)).reshape(DEPTH, 1, NH_M * 128)
    m_rep = jnp.repeat(state_m.astype(F32), 128, axis=2)
    n_all = state_n.astype(F32).reshape(DEPTH, DB, NH_M * DQK_M)
    rows_view = lambda a: a.reshape(DEPTH, DB, -1, HD)
    mk_all, mv_all = rows_view(cache_meta_k), rows_view(cache_meta_v)
    wk_all, wv_all = rows_view(cache_win_k), rows_view(cache_win_v)

    tabs_p = _rope_tables(N_META + jnp.arange(S, dtype=jnp.int32))
    pos_small = jnp.concatenate([jnp.full((DB,), PAST_LEN, jnp.int32), jnp.arange(N_META, dtype=jnp.int32),
                                 jnp.zeros((SMALL_ROWS - DB - N_META,), jnp.int32)])
    tabs_s = _rope_tables(pos_small)

    xp = x_prompt.reshape(B * S, D_MODEL)
    xs = jnp.concatenate([x_sample.reshape(DB, D_MODEL), meta_tokens.astype(F32),
                          jnp.zeros((SMALL_ROWS - DB - N_META, D_MODEL), F32)], axis=0)
    zero_state = (jnp.zeros((1, NH_M, DQK_M, DV_M), F32), jnp.zeros((1, NH_M, DQK_M, 128), F32),
                  jnp.zeros((1, NH_M, 8, 128), F32))

    outs = {k: [] for k in ("ct", "n", "m", "meta", "win", "sn", "sm")}
    sC = swk = swv = None
    hs_ = _norm(0, xs, g_mix, SMALL_ROWS)
    hp = _norm(0, xp, g_mix, big_tm)
    for l in range(DEPTH):
        pa_s = _inproj_a(l, hs_, w_a, SMALL_ROWS, F32)
        qa_s, misc_s, kt_s, gt_s, km_s, grep_s = _inproj_b(l, hs_, w_b, tabs_s, SMALL_ROWS, small=True)
        y_meta, ct_m, n_m, m_m = _mix(l, sinks, qa_s, misc_s, pa_s, kt_s, gt_s, bias8, nmh, *zero_state, misc_s,
                                      nb=1, nblk=1, sub=1, row0=META_ROW0, valid_len=N_META, has_meta=False)
        ys_m, sC, sn, sm = _mlstm_step(l, misc_s, km_s, pa_s, grep_s, m_rep, n_all, state_C, sC, bi_rep, bf_rep, nmh)
        ys_a, swk, swv = _swa_step(
            l, qa_s[:DB].reshape(DB, N_HEADS, HD), pa_s[:DB, A_GA:A_GA + D_MODEL].reshape(DB, N_HEADS, HD),
            misc_s, mk_all, mv_all, wk_all, wv_all, swk, swv, sink8)
        y_s = jnp.concatenate([(ys_m + ys_a.reshape(DB, D_MODEL)).astype(BF16), y_meta], axis=0)
        res = _tail(l, y_s, xs, w_out_b, g_mlp, w_up_b, w_down_b, g_next, tm=SMALL_ROWS, tf=512)
        xs, hs_ = res[0], res[-1]

        pa = _inproj_a(l, hp, w_a, big_tm, BF16)
        qa, misc, kt, gt = _inproj_b(l, hp, w_b, tabs_p, big_tm, small=False)
        y, ct_f, n_f, m_f = _mix(l, sinks, qa, misc, pa, kt, gt, bias8, nmh, ct_m, n_m, m_m, misc_s,
                                 nb=B, nblk=nchunk, sub=MIX_SUB, row0=0, valid_len=CHUNK, has_meta=True)
        res = _tail(l, y, xp, w_out_b, g_mlp, w_up_b, w_down_b, g_next, tm=big_tm, tf=TAIL_TF)
        xp, hp = res[0], res[-1]

        for key, val in (("ct", ct_f), ("n", n_f), ("m", m_f), ("sn", sn), ("sm", sm),
                         ("meta", misc_s[META_ROW0:META_ROW0 + N_META, M_KA:M_VA + kw]),
                         ("win", misc.reshape(B, S, M_TOT)[:, S - WINDOW:, M_KA:M_VA + kw])):
            outs[key].append(val)

    y_prompt = xp.reshape(B, S, D_MODEL)
    y_sample = xs[:DB].reshape(DB, 1, D_MODEL)
    st = lambda k: jnp.stack(outs[k])
    p_c = jnp.swapaxes(st("ct"), 3, 4).astype(state_C.dtype)
    p_n = st("n")[..., 0].astype(state_n.dtype)
    p_m = st("m")[..., 0, 0].astype(state_m.dtype)
    meta = jnp.broadcast_to(st("meta").astype(F32)[:, None], (DEPTH, B, N_META, 2 * kw))
    win = st("win").astype(F32)
    kv_of = lambda a, lo: a[..., lo:lo + kw].reshape(a.shape[:-1] + (N_KV, HD))
    s_n = st("sn").reshape(DEPTH, DB, NH_M, DQK_M).astype(state_n.dtype)
    s_m = st("sm").reshape(DEPTH, DB, NH_M, 128)[..., 0].astype(state_m.dtype)
    return (y_prompt, y_sample, p_c, p_n, p_m, kv_of(meta, 0), kv_of(meta, kw), kv_of(win, 0), kv_of(win, kw),
            sC, s_n, s_m, swk.reshape(cache_win_k.shape), swv.reshape(cache_win_v.shape))
```

```python
import functools

import jax
import jax.numpy as jnp
import numpy as np
from jax import lax
from jax.experimental import pallas as pl
from jax.experimental.pallas import tpu as pltpu

F32 = jnp.float32
BF16 = jnp.bfloat16

D_MODEL = 1024
DEPTH = 4
PAST_LEN = 8192
N_META = 16
NH_M = 4
DV_M = 256
DQK_M = 128
CHUNK = 128
N_HEADS = 8
N_KV = 2
GQ = N_HEADS // N_KV
HD = 128
WINDOW = 128
ROT_DIM = 32
ROPE_THETA = 500000.0
D_FF = 4 * D_MODEL
EPS = 1e-6
IN_WIDTHS = (NH_M * DQK_M, NH_M * DQK_M, NH_M * DV_M, NH_M, NH_M, NH_M * DV_M,
             N_HEADS * HD, N_KV * HD, N_KV * HD, D_MODEL, D_MODEL)

A_VM, A_OM, A_GM, A_GA, A_TOT = 0, 1024, 2048, 3072, 4096
A_TN = 2048
B_QA, B_KA, B_VA, B_QM, B_KM, B_G, B_TOT = 0, 1024, 1280, 1536, 2048, 2560, 2688
M_KA, M_VA, M_QM, M_TOT = 0, 256, 512, 1024
TAIL_TF = 1024
MIX_SUB = 4
BLOCK_LAG = 2
MLSTM_STAGES = 8
SMALL_ROWS = 256
META_ROW0 = 128
DEC_BLK = 8
VMEM_LIMIT = 56 * 1024 * 1024


def _cparams(n_axes):
    return pltpu.CompilerParams(dimension_semantics=("arbitrary",) * n_axes, vmem_limit_bytes=VMEM_LIMIT)


def _rms(xf, gain):
    ms = jnp.mean(xf * xf, axis=-1, keepdims=True)
    return xf * lax.rsqrt(ms + EPS) * gain


def _dot(a, b):
    return jnp.dot(a, b, preferred_element_type=F32)


def _dot_nt(a, b):
    return lax.dot_general(a, b, (((1,), (1,)), ((), ())), preferred_element_type=F32)


def _mm(a, b, nt=False):
    return (_dot_nt if nt else _dot)(a.astype(BF16), b.astype(BF16))


def _log_sigmoid(x):
    return jnp.minimum(x, 0.0) - jnp.log1p(jnp.exp(-jnp.abs(x)))


def _run_interleaved(chains):
    chains = list(chains)
    while chains:
        alive = []
        for chain in chains:
            try:
                next(chain)
                alive.append(chain)
            except StopIteration:
                pass
        chains = alive


def _sigmoid_bf16(x):
    half = jnp.asarray(0.5, x.dtype)
    return jnp.tanh(x * half) * half + half


def _norm_kernel(x_ref, gain_ref, o_ref):
    o_ref[...] = _rms(x_ref[...], gain_ref[...]).astype(o_ref.dtype)


def _norm(l, x, gains, tm, out_dtype):
    rows = x.shape[0]
    return pl.pallas_call(
        _norm_kernel,
        grid=(rows // tm,),
        in_specs=[pl.BlockSpec((tm, D_MODEL), lambda i: (i, 0)),
                  pl.BlockSpec((None, 1, D_MODEL), lambda i: (l, 0, 0))],
        out_specs=pl.BlockSpec((tm, D_MODEL), lambda i: (i, 0)),
        out_shape=jax.ShapeDtypeStruct((rows, D_MODEL), out_dtype),
        compiler_params=_cparams(1),
        name="norm",
    )(x, gains)


def _inproj_a_kernel(h_ref, w_ref, o_ref):
    o_ref[...] = _dot(h_ref[...], w_ref[...]).astype(o_ref.dtype)


def _inproj_a(l, h, w_a, tm, out_dtype):
    rows = h.shape[0]
    return pl.pallas_call(
        _inproj_a_kernel,
        grid=(rows // tm, A_TOT // A_TN),
        in_specs=[pl.BlockSpec((tm, D_MODEL), lambda i, j: (i, 0)),
                  pl.BlockSpec((None, D_MODEL, A_TN), lambda i, j: (l, 0, j))],
        out_specs=pl.BlockSpec((tm, A_TN), lambda i, j: (i, j)),
        out_shape=jax.ShapeDtypeStruct((rows, A_TOT), out_dtype),
        compiler_params=_cparams(2),
        name="inproj_a",
    )(h, w_a)


def _inproj_b_kernel(h_ref, w_ref, cs_ref, s1_ref, s2_ref, qa_ref, misc_ref, kt_ref, gt_ref, *row_form):
    small = bool(row_form)
    mm = lambda lo, hi: _dot(h_ref[...], w_ref[:, lo:hi])

    def rope(seg):
        return (seg * cs_ref[...] + pltpu.roll(seg, 16, 1) * s1_ref[...]
                + pltpu.roll(seg, HD - 16, 1) * s2_ref[...])

    acc = mm(B_QA, B_KA)
    for c in range(N_HEADS):
        cs = slice(c * HD, (c + 1) * HD)
        qa_ref[:, cs] = (rope(acc[:, cs]) * (HD ** -0.5)).astype(qa_ref.dtype)

    acc = mm(B_KA, B_KM)
    for c in range(N_KV):
        cs = slice(M_KA + c * HD, M_KA + (c + 1) * HD)
        misc_ref[:, cs] = rope(acc[:, cs]).astype(misc_ref.dtype)
    misc_ref[:, M_VA:M_QM] = acc[:, M_VA:M_QM].astype(misc_ref.dtype)
    misc_ref[:, M_QM:M_TOT] = (acc[:, M_QM:M_TOT] * (DQK_M ** -0.5)).astype(misc_ref.dtype)

    acc = mm(B_KM, B_TOT)
    kt_ref[...] = acc[:, 0:NH_M * DQK_M].T.astype(kt_ref.dtype)
    gt_ref[...] = acc[:, NH_M * DQK_M:B_TOT - B_KM].T[0:2 * NH_M]

    if small:
        km_ref, grep_ref = row_form
        km_ref[...] = acc[:, 0:NH_M * DQK_M]
        for c in range(2 * NH_M):
            col = NH_M * DQK_M + c
            grep_ref[:, c * 128:(c + 1) * 128] = jnp.broadcast_to(acc[:, col:col + 1], (acc.shape[0], 128))


G_REP = 2 * NH_M * 128


def _inproj_b(l, h, w_b, tabs, tm, small):
    rows = h.shape[0]
    tab_blocks = tabs[0].shape[0] // tm
    tab_spec = pl.BlockSpec((tm, HD), lambda i: (i % tab_blocks, 0))
    dt = F32 if small else BF16
    row_spec = lambda width: pl.BlockSpec((tm, width), lambda i: (i, 0))
    out_specs = [row_spec(D_MODEL), row_spec(M_TOT),
                 pl.BlockSpec((NH_M * DQK_M, tm), lambda i: (0, i)),
                 pl.BlockSpec((2 * NH_M, tm), lambda i: (0, i))]
    out_shape = [jax.ShapeDtypeStruct((rows, D_MODEL), dt),
                 jax.ShapeDtypeStruct((rows, M_TOT), dt),
                 jax.ShapeDtypeStruct((NH_M * DQK_M, rows), dt),
                 jax.ShapeDtypeStruct((2 * NH_M, rows), F32)]
    if small:
        out_specs += [row_spec(NH_M * DQK_M), row_spec(G_REP)]
        out_shape += [jax.ShapeDtypeStruct((rows, NH_M * DQK_M), F32), jax.ShapeDtypeStruct((rows, G_REP), F32)]
    return pl.pallas_call(
        _inproj_b_kernel,
        grid=(rows // tm,),
        in_specs=[pl.BlockSpec((tm, D_MODEL), lambda i: (i, 0)),
                  pl.BlockSpec((None, D_MODEL, B_TOT), lambda i: (l, 0, 0)),
                  tab_spec, tab_spec, tab_spec],
        out_specs=out_specs,
        out_shape=out_shape,
        compiler_params=_cparams(1),
        name="inproj_b",
    )(h, w_b, *tabs)


def _mix_kernel(sink_ref, misc_ref, miscp_ref, pa_ref, qa_ref, kt_ref, g_ref, bias_ref, nmh_ref,
                ct0_ref, n0_ref, m0_ref, mk_ref, mv_ref,
                y_ref, ctf_ref, nf_ref, mf_ref, ct_s, n_s, m_s, *, l, nc, sub, valid_len, has_meta):
    c = pl.program_id(1)
    L = CHUNK
    mm = _mm
    cast = lambda a: a.astype(BF16)

    @pl.when(c == 0)
    def _():
        ct_s[...] = ct0_ref[...]
        n_s[...] = n0_ref[...]
        m_s[...] = m0_ref[...]

    row = lax.broadcasted_iota(jnp.int32, (L, L), 0)
    col = lax.broadcasted_iota(jnp.int32, (L, L), 1)
    tril = col <= row
    eye = col == row

    gates = g_ref[...] + bias_ref[:, 0:1]
    ig_sub = gates[0:NH_M]
    lf_sub = _log_sigmoid(gates[NH_M:2 * NH_M])
    if valid_len < L:
        ok = lax.broadcasted_iota(jnp.int32, (NH_M, sub * L), 1) < valid_len
        ig_sub = jnp.where(ok, ig_sub, -jnp.inf)
        lf_sub = jnp.where(ok, lf_sub, 0.0)

    ym, ya, gates_out = {}, {}, {}

    def mlstm_head(k, h):
        rs = slice(k * L, (k + 1) * L)
        ig = ig_sub[h:h + 1, rs]
        lf = lf_sub[h:h + 1, rs]
        b_col = jnp.sum(jnp.where(tril, lf, 0.0), axis=1, keepdims=True)
        b_row = jnp.sum(jnp.where(eye, b_col, 0.0), axis=0, keepdims=True)
        yield
        m_prev = m_s[h, 0:1, 0:1]
        a_col = b_col + m_prev
        dmat = jnp.where(tril, b_col - b_row + ig, -jnp.inf)
        m_col = jnp.maximum(a_col, jnp.max(dmat, axis=1, keepdims=True))
        w_inter = jnp.exp(a_col - m_col)
        wmat = jnp.exp(dmat - m_col)
        m_new = m_col[L - 1:L]
        m_s[h] = jnp.broadcast_to(m_new, (8, 128))
        q = cast(misc_ref[rs, M_QM + h * DQK_M:M_QM + (h + 1) * DQK_M])
        kt = kt_ref[h * DQK_M:(h + 1) * DQK_M, rs]
        v = cast(pa_ref[rs, A_VM + h * DV_M:A_VM + (h + 1) * DV_M])
        yield
        qk = mm(q, kt)
        yield
        s = qk * wmat
        s_b = cast(s)
        yield
        decay = w_inter[L - 1:L]
        g_row = jnp.exp(b_col[L - 1:L] - b_row + ig - m_new)
        ktg = kt.astype(F32) * g_row
        ktg_b = cast(ktg)
        yield
        intra = mm(s_b, v)
        upd = mm(ktg_b, v)
        yield
        ct = ct_s[h]
        nm = n_s[h]
        inter = mm(q, ct)
        qn = mm(q, nm)[:, 0:1]
        yield
        num = w_inter * inter + intra
        den = w_inter * qn + jnp.sum(s, axis=1, keepdims=True)
        hh = num / jnp.maximum(jnp.abs(den), jnp.exp(-m_col))
        ct_s[h] = decay * ct + upd
        n_s[h] = decay * nm + jnp.sum(ktg, axis=1, keepdims=True)
        yield
        hs = slice(h * DV_M, (h + 1) * DV_M)
        hn = hh * lax.rsqrt(jnp.mean(hh * hh, axis=1, keepdims=True) + EPS) * nmh_ref[:, hs]
        ym[k, h] = gates_out[k, h] * hn

    def gate_chain(k):
        rs = slice(k * L, (k + 1) * L)
        yield
        for h in range(NH_M):
            gates_out[k, h] = (_sigmoid_bf16(pa_ref[rs, A_GM + h * DV_M:A_GM + (h + 1) * DV_M])
                               * _sigmoid_bf16(pa_ref[rs, A_OM + h * DV_M:A_OM + (h + 1) * DV_M])
                               ).astype(F32)
            if h == 1:
                yield
        yield
        gates_out[k, "a"] = _sigmoid_bf16(pa_ref[rs, A_GA:A_GA + D_MODEL]).astype(F32)

    def store_chain(k):
        rs = slice(k * L, (k + 1) * L)
        for _ in range(MLSTM_STAGES):
            yield
        for hd in range(N_HEADS):
            ym_h = ym[k, (hd * HD) // DV_M][:, (hd * HD) % DV_M:(hd * HD) % DV_M + HD]
            y_ref[rs, hd * HD:(hd + 1) * HD] = (ym_h + ya[k, hd]).astype(y_ref.dtype)

    def delayed(chain, rounds):
        for _ in range(rounds):
            yield
        yield from chain

    chains = []
    for k in range(sub):
        block = [mlstm_head(k, h) for h in range(NH_M)] + [gate_chain(k)]
        block += [_swa_chain(c, k, kv, sink_ref.at[l], qa_ref, misc_ref, miscp_ref, gates_out, mk_ref, mv_ref, ya,
                             has_meta) for kv in range(N_KV)]
        block.append(store_chain(k))
        chains += [delayed(chain, k * BLOCK_LAG) for chain in block]
    _run_interleaved(chains)

    @pl.when(c == nc - 1)
    def _():
        ctf_ref[0] = ct_s[...]
        nf_ref[0] = n_s[...]
        mf_ref[0] = m_s[...]


def _mix(l, sinks, qa, misc, pa, kt, gt, bias8, nmh, ct0, n0, m0, misc_small, nb, nblk, sub, row0, valid_len,
         has_meta):
    nc = nblk // sub
    rows = sub * CHUNK
    r0 = row0 // rows
    rb = lambda b, c: r0 + b * nc + c
    rp = lambda b, c: (r0 + b * nc + c) * sub - jnp.minimum(c, 1)
    kw = N_KV * HD
    first4 = lambda b, c: (0, 0, 0, 0)
    st4 = lambda b, c: (b, 0, 0, 0)
    return pl.pallas_call(
        functools.partial(_mix_kernel, l=l, nc=nc, sub=sub, valid_len=valid_len, has_meta=has_meta),
        grid=(nb, nc),
        in_specs=[pl.BlockSpec(memory_space=pltpu.SMEM),
                  pl.BlockSpec((rows, M_TOT), lambda b, c: (rb(b, c), 0)),
                  pl.BlockSpec((CHUNK, 2 * kw), lambda b, c: (rp(b, c), 0)),
                  pl.BlockSpec((rows, A_TOT), lambda b, c: (rb(b, c), 0)),
                  pl.BlockSpec((rows, D_MODEL), lambda b, c: (rb(b, c), 0)),
                  pl.BlockSpec((NH_M * DQK_M, rows), lambda b, c: (0, rb(b, c))),
                  pl.BlockSpec((2 * NH_M, rows), lambda b, c: (0, rb(b, c))),
                  pl.BlockSpec((None, 2 * NH_M, 128), lambda b, c: (l, 0, 0)),
                  pl.BlockSpec((None, 1, D_MODEL), lambda b, c: (l, 0, 0)),
                  pl.BlockSpec((None, NH_M, DQK_M, DV_M), first4),
                  pl.BlockSpec((None, NH_M, DQK_M, 128), first4),
                  pl.BlockSpec((None, NH_M, 8, 128), first4),
                  pl.BlockSpec((N_META, kw), lambda b, c: (META_ROW0 // N_META, M_KA // kw)),
                  pl.BlockSpec((N_META, kw), lambda b, c: (META_ROW0 // N_META, M_VA // kw))],
        out_specs=[pl.BlockSpec((rows, D_MODEL), lambda b, c: (b * nc + c, 0)),
                   pl.BlockSpec((1, NH_M, DQK_M, DV_M), st4),
                   pl.BlockSpec((1, NH_M, DQK_M, 128), st4),
                   pl.BlockSpec((1, NH_M, 8, 128), st4)],
        out_shape=[jax.ShapeDtypeStruct((nb * nc * rows, D_MODEL), BF16),
                   jax.ShapeDtypeStruct((nb, NH_M, DQK_M, DV_M), F32),
                   jax.ShapeDtypeStruct((nb, NH_M, DQK_M, 128), F32),
                   jax.ShapeDtypeStruct((nb, NH_M, 8, 128), F32)],
        scratch_shapes=[pltpu.VMEM((NH_M, DQK_M, DV_M), F32),
                        pltpu.VMEM((NH_M, DQK_M, 128), F32),
                        pltpu.VMEM((NH_M, 8, 128), F32)],
        compiler_params=_cparams(2),
        name="mix",
    )(sinks, misc, misc, pa, qa, kt, gt, bias8, nmh, ct0, n0, m0, misc_small, misc_small)


def _swa_chain(c, k, kv, sink_ref, q_ref, misc_ref, miscp_ref, gates_out, mk_ref, mv_ref, ya, has_meta):
    L = WINDOW
    R = GQ * L
    rs = slice(k * L, (k + 1) * L)
    row = lax.broadcasted_iota(jnp.int32, (R, L), 0)
    col = lax.broadcasted_iota(jnp.int32, (R, L), 1)
    tril = col <= (row & (L - 1))
    head_of_row = lax.shift_right_logical(lax.broadcasted_iota(jnp.int32, (R, 1), 0), L.bit_length() - 1)
    mm = _mm
    cast = lambda a: a.astype(BF16)
    with_ones = lambda v: jnp.concatenate([v, jnp.ones((v.shape[0], HD), v.dtype)], axis=1)
    if k == 0:
        prev = lambda cols: cast(miscp_ref[:, cols])
        prev_ok = jnp.full((R, L), c, jnp.int32) > 0
    else:
        prev = lambda cols: cast(misc_ref[(k - 1) * L:k * L, cols])
        prev_ok = None
    ka = slice(M_KA + kv * HD, M_KA + (kv + 1) * HD)
    va = slice(M_VA + kv * HD, M_VA + (kv + 1) * HD)
    ks = slice(kv * HD, (kv + 1) * HD)
    heads = [kv * GQ + g for g in range(GQ)]
    q = jnp.concatenate([cast(q_ref[rs, hd * HD:(hd + 1) * HD]) for hd in heads], axis=0)
    sink = jnp.zeros((R, 1), F32)
    for g, hd in enumerate(heads):
        sink = jnp.where(head_of_row == g, sink_ref[hd], sink)
    yield
    yield
    s_cur = mm(q, cast(misc_ref[rs, ka]), nt=True)
    s_prev = mm(q, prev(ka), nt=True)
    if has_meta:
        s_m = mm(q, cast(mk_ref[:, ks]), nt=True)
    yield
    if prev_ok is not None:
        s_prev = jnp.where(prev_ok, s_prev, -jnp.inf)
    band = jnp.where(tril, s_cur, s_prev)
    mx = jnp.maximum(jnp.max(band, axis=1, keepdims=True), sink)
    if has_meta:
        mx = jnp.maximum(mx, jnp.max(s_m, axis=1, keepdims=True))
        p_m = cast(jnp.exp(s_m - mx))
    yield
    p = jnp.exp(band - mx)
    p_cur = cast(jnp.where(tril, p, 0.0))
    p_prev = cast(jnp.where(tril, 0.0, p))
    yield
    oe = (mm(p_cur, with_ones(cast(misc_ref[rs, va])))
          + mm(p_prev, with_ones(prev(va))))
    if has_meta:
        oe = oe + mm(p_m, with_ones(cast(mv_ref[:, ks])))
    yield
    denom = oe[:, HD:HD + 1] + jnp.exp(sink - mx)
    o = oe[:, 0:HD] / denom
    yield
    for g, hd in enumerate(heads):
        ya[k, hd] = o[g * L:(g + 1) * L] * gates_out[k, "a"][:, hd * HD:(hd + 1) * HD]


def _tail_kernel(y_ref, x_ref, gn_ref, gnext_ref, wo_ref, wu_ref, wd_ref, o_ref, *rest, nj, last_layer):
    hf_ref = rest[-1]
    j = pl.program_id(1)

    @pl.when(j == 0)
    def _():
        x1 = x_ref[...] + _dot(y_ref[...], wo_ref[...])
        o_ref[...] = x1
        hf_ref[...] = _rms(x1, gn_ref[...]).astype(BF16)

    a = jnp.square(jnp.maximum(_dot(hf_ref[...], wu_ref[...]), 0.0)).astype(BF16)
    o_ref[...] += _dot(a, wd_ref[...])

    @pl.when(j == nj - 1)
    def _():
        if last_layer:
            o_ref[...] = _rms(o_ref[...], gnext_ref[...])
        else:
            rest[0][...] = _rms(o_ref[...], gnext_ref[...]).astype(BF16)


def _tail(l, y, x, g_mlp, g_next, weights, tm, tf):
    rows = x.shape[0]
    nj = D_FF // tf
    last_layer = l == DEPTH - 1
    row_spec = pl.BlockSpec((tm, D_MODEL), lambda i, j: (i, 0))
    out_specs, out_shape = [row_spec], [jax.ShapeDtypeStruct((rows, D_MODEL), F32)]
    if not last_layer:
        out_specs.append(row_spec)
        out_shape.append(jax.ShapeDtypeStruct((rows, D_MODEL), BF16))
    return pl.pallas_call(
        functools.partial(_tail_kernel, nj=nj, last_layer=last_layer),
        grid=(rows // tm, nj),
        in_specs=[row_spec, row_spec,
                  pl.BlockSpec((None, 1, D_MODEL), lambda i, j: (l, 0, 0)),
                  pl.BlockSpec((None, 1, D_MODEL), lambda i, j: (l + 1, 0, 0)),
                  pl.BlockSpec((None, D_MODEL, D_MODEL), lambda i, j: (l, 0, 0)),
                  pl.BlockSpec((None, D_MODEL, tf), lambda i, j: (l, 0, j)),
                  pl.BlockSpec((None, tf, D_MODEL), lambda i, j: (l, j, 0))],
        out_specs=out_specs,
        out_shape=out_shape,
        scratch_shapes=[pltpu.VMEM((tm, D_MODEL), BF16)],
        compiler_params=_cparams(2),
        name="tail",
    )(y, x, g_mlp, g_next, *weights)


def _mlstm_step_kernel(q_ref, k_ref, v_ref, o_ref, gm_ref, ig_ref, fg_ref, m_ref, n_ref, c_ref,
                       kall_ref, vall_ref, bi_ref, bf_ref, nmh_ref, *rest):
    y_ref, co_ref, no_ref, mo_ref = rest[-4:]
    i = pl.program_id(0)
    nb = DEC_BLK
    row8 = lax.broadcasted_iota(jnp.int32, (nb, DV_M), 0)
    rowb = lax.broadcasted_iota(jnp.int32, (kall_ref.shape[0], DQK_M), 0)
    for h in range(NH_M):
        ds = slice(h * DQK_M, (h + 1) * DQK_M)
        vs = slice(h * DV_M, (h + 1) * DV_M)
        q, k, v, nrm = q_ref[:, ds], k_ref[:, ds], v_ref[:, vs], n_ref[:, ds]
        ig = ig_ref[:, ds] + bi_ref[:, ds]
        lf = _log_sigmoid(fg_ref[:, ds] + bf_ref[:, ds])
        m_prev = m_ref[:, ds]
        m_t = jnp.maximum(lf + m_prev, ig)
        w_inter = jnp.exp(lf + m_prev - m_t)
        w_in = jnp.exp(ig - m_t)
        s = jnp.sum(q * k, axis=1, keepdims=True) * w_in
        qb = q.astype(BF16)
        kall = kall_ref[:, ds].astype(BF16)
        vt = vall_ref[:, vs].T.astype(BF16)
        cq = jnp.zeros((nb, DV_M), F32)
        for bb in range(nb):
            c_old = c_ref[bb, h]
            cq = jnp.where(row8 == bb, _dot_nt(qb, c_old.astype(BF16)), cq)
            outer = _dot(vt, jnp.where(rowb == i * nb + bb, kall, jnp.zeros_like(kall)))
            co_ref[bb, h] = w_inter[bb:bb + 1, :] * c_old + w_in[bb:bb + 1, :] * outer
        num = w_inter[:, 0:1] * cq + s[:, 0:1] * v
        den = w_inter * jnp.sum(nrm * q, axis=1, keepdims=True) + s
        hh = num / jnp.maximum(jnp.abs(den), jnp.exp(-m_t))[:, 0:1]
        no_ref[:, ds] = w_inter * nrm + w_in * k
        mo_ref[:, ds] = m_t
        hn = hh * lax.rsqrt(jnp.mean(hh * hh, axis=1, keepdims=True) + EPS) * nmh_ref[:, vs]
        y_ref[:, vs] = jax.nn.sigmoid(gm_ref[:, vs]) * (jax.nn.sigmoid(o_ref[:, vs]) * hn)


def _mlstm_step(l, misc, km, pa, grep, m_all, n_all, c_all, c_out_prev, bi, bf, nmh):
    nbatch = c_all.shape[1]
    qd = NH_M * DQK_M
    rowq = lambda j: pl.BlockSpec((DEC_BLK, qd), lambda i: (i, j))
    rowv = lambda j: pl.BlockSpec((DEC_BLK, D_MODEL), lambda i: (i, j))
    lrow = pl.BlockSpec((None, DEC_BLK, qd), lambda i: (l, i, 0))
    lconst = lambda width: pl.BlockSpec((None, 1, width), lambda i: (l, 0, 0))
    cspec = pl.BlockSpec((None, DEC_BLK, NH_M, DV_M, DQK_M), lambda i: (l, i, 0, 0, 0))
    in_specs = [rowq(M_QM // qd), rowq(0), rowv(A_VM // D_MODEL), rowv(A_OM // D_MODEL), rowv(A_GM // D_MODEL),
                rowq(0), rowq(1), lrow, lrow, cspec,
                pl.BlockSpec((nbatch, qd), lambda i: (0, 0)),
                pl.BlockSpec((nbatch, D_MODEL), lambda i: (0, A_VM // D_MODEL)),
                lconst(qd), lconst(qd), lconst(D_MODEL)]
    args = [misc, km, pa, pa, pa, grep, grep, m_all, n_all, c_all, km, pa, bi, bf, nmh]
    aliases = {}
    if c_out_prev is not None:
        aliases = {len(args): 1}
        in_specs.append(pl.BlockSpec(memory_space=pl.ANY))
        args.append(c_out_prev)
    return pl.pallas_call(
        _mlstm_step_kernel,
        grid=(nbatch // DEC_BLK,),
        in_specs=in_specs,
        out_specs=[rowv(0), cspec, rowq(0), rowq(0)],
        out_shape=[jax.ShapeDtypeStruct((nbatch, D_MODEL), F32),
                   jax.ShapeDtypeStruct(c_all.shape, F32),
                   jax.ShapeDtypeStruct((nbatch, qd), F32),
                   jax.ShapeDtypeStruct((nbatch, qd), F32)],
        input_output_aliases=aliases,
        compiler_params=_cparams(1),
        name="mlstm_step",
    )(*args)


def _swa_step_kernel(q_ref, ga_ref, kn_ref, vn_ref, mk_ref, mv_ref, wk_ref, wv_ref, sink_ref, *rest):
    y_ref, ok_ref, ov_ref = rest[-3:]
    W = WINDOW
    hrow = lax.broadcasted_iota(jnp.int32, (N_HEADS, HD), 0)
    first = hrow < GQ
    wcol = lax.broadcasted_iota(jnp.int32, (N_HEADS, W), 1)
    wrow = lax.broadcasted_iota(jnp.int32, (N_KV * W, HD), 0)
    sink = sink_ref[...][:, 0:1]

    def per_kv(fn):
        a0, a1 = fn(0), fn(1)
        return jnp.where(first[:, 0:a0.shape[1]], a0, a1)

    def batch_element(bb):
        q = q_ref[bb]
        qb = q.astype(BF16)
        rows_of = lambda ref, kv, n: ref[bb, pl.ds(kv, n, stride=N_KV), :].astype(BF16)
        kn, vn = kn_ref[bb:bb + 1, :], vn_ref[bb:bb + 1, :]
        kvs = lambda a, kv: a[:, kv * HD:(kv + 1) * HD]
        s_w = jnp.where(wcol >= 1, per_kv(lambda kv: _dot_nt(qb, rows_of(wk_ref, kv, W))), -jnp.inf)
        s_m = per_kv(lambda kv: _dot_nt(qb, rows_of(mk_ref, kv, N_META)))
        kn8 = per_kv(lambda kv: jnp.broadcast_to(kvs(kn, kv), (N_HEADS, HD)))
        vn8 = per_kv(lambda kv: jnp.broadcast_to(kvs(vn, kv), (N_HEADS, HD)))
        s_n = jnp.sum(q * kn8, axis=1, keepdims=True)
        yield
        mx = jnp.maximum(jnp.maximum(jnp.max(s_w, axis=1, keepdims=True), jnp.max(s_m, axis=1, keepdims=True)),
                         jnp.maximum(s_n, sink))
        p_w, p_m, p_n = jnp.exp(s_w - mx), jnp.exp(s_m - mx), jnp.exp(s_n - mx)
        denom = (jnp.sum(p_w, axis=1, keepdims=True) + jnp.sum(p_m, axis=1, keepdims=True) + p_n
                 + jnp.exp(sink - mx))
        yield
        o = (per_kv(lambda kv: _dot(p_w.astype(BF16), rows_of(wv_ref, kv, W)))
             + per_kv(lambda kv: _dot(p_m.astype(BF16), rows_of(mv_ref, kv, N_META)))
             + p_n * vn8)
        yield
        y_ref[bb] = (o / denom) * jax.nn.sigmoid(ga_ref[bb])
        for src_ref, new, dst_ref in ((wk_ref, kn, ok_ref), (wv_ref, vn, ov_ref)):
            shifted = pltpu.roll(src_ref[bb], N_KV * (W - 1), 0)
            for kv in range(N_KV):
                shifted = jnp.where(wrow == N_KV * (W - 1) + kv, kvs(new, kv), shifted)
            dst_ref[bb] = shifted

    _run_interleaved([batch_element(bb) for bb in range(DEC_BLK)])


def _swa_step(l, q, ga, misc, mk_all, mv_all, wk_all, wv_all, ok_prev, ov_prev, sink8):
    nbatch = q.shape[0]
    kw = N_KV * HD
    hspec = pl.BlockSpec((DEC_BLK, N_HEADS, HD), lambda i: (i, 0, 0))
    mspec = pl.BlockSpec((None, DEC_BLK, N_KV * N_META, HD), lambda i: (l, i, 0, 0))
    wspec = pl.BlockSpec((None, DEC_BLK, N_KV * WINDOW, HD), lambda i: (l, i, 0, 0))
    in_specs = [hspec, hspec,
                pl.BlockSpec((DEC_BLK, kw), lambda i: (i, M_KA // kw)),
                pl.BlockSpec((DEC_BLK, kw), lambda i: (i, M_VA // kw)),
                mspec, mspec, wspec, wspec,
                pl.BlockSpec((None, N_HEADS, HD), lambda i: (l, 0, 0))]
    args = [q, ga, misc, misc, mk_all, mv_all, wk_all, wv_all, sink8]
    aliases = {}
    if ok_prev is not None:
        aliases = {len(args): 1, len(args) + 1: 2}
        in_specs += [pl.BlockSpec(memory_space=pl.ANY)] * 2
        args += [ok_prev, ov_prev]
    return pl.pallas_call(
        _swa_step_kernel,
        grid=(nbatch // DEC_BLK,),
        in_specs=in_specs,
        out_specs=[hspec, wspec, wspec],
        out_shape=[jax.ShapeDtypeStruct((nbatch, N_HEADS, HD), F32),
                   jax.ShapeDtypeStruct(wk_all.shape, F32),
                   jax.ShapeDtypeStruct(wv_all.shape, F32)],
        input_output_aliases=aliases,
        compiler_params=_cparams(1),
        name="swa_step",
    )(*args)


def _rope_tables(pos):
    half = ROT_DIM // 2
    inv = jnp.float32(ROPE_THETA) ** (-jnp.arange(half, dtype=F32) * 2.0 / ROT_DIM)
    ang = pos.astype(F32)[:, None] * inv[None, :]
    cos, sin = jnp.cos(ang), jnp.sin(ang)
    t = pos.shape[0]
    cs = jnp.concatenate([cos, cos, jnp.ones((t, HD - ROT_DIM), F32)], axis=1)
    s1 = jnp.concatenate([jnp.zeros((t, half), F32), sin, jnp.zeros((t, HD - ROT_DIM), F32)], axis=1)
    s2 = jnp.concatenate([-sin, jnp.zeros((t, HD - half), F32)], axis=1)
    return cs, s1, s2


PACK_ROWS = 128


def _pack_kernel(w_ref, wa_ref, wb_ref):
    edges = np.concatenate([[0], np.cumsum(IN_WIDTHS)])
    q_m, k_m, v_m, ig, fg, o_m, q_a, k_a, v_a, g_m, g_a = [(int(edges[i]), int(edges[i + 1])) for i in range(11)]
    x = w_ref[...]

    def put(dst_ref, dst0, seg):
        lo, hi = seg
        dst_ref[:, dst0:dst0 + hi - lo] = x[:, lo:hi].astype(BF16)

    for dst0, seg in ((A_VM, v_m), (A_OM, o_m), (A_GM, g_m), (A_GA, g_a)):
        put(wa_ref, dst0, seg)
    for dst0, seg in ((B_QA, q_a), (B_KA, k_a), (B_VA, v_a), (B_QM, q_m), (B_KM, k_m)):
        put(wb_ref, dst0, seg)
    tile = x[:, ig[0]:ig[0] + 128]
    lane = lax.broadcasted_iota(jnp.int32, tile.shape, 1)
    wb_ref[:, B_G:B_TOT] = jnp.where(lane < 2 * NH_M, tile, 0.0).astype(BF16)


def _pack_in_weights(w_in):
    assert IN_WIDTHS[3] == IN_WIDTHS[4] == NH_M and sum(IN_WIDTHS[:3]) % 128 == 0
    depth, rows, cols = w_in.shape
    return pl.pallas_call(
        _pack_kernel,
        grid=(depth, rows // PACK_ROWS),
        in_specs=[pl.BlockSpec((None, PACK_ROWS, cols), lambda l, i: (l, i, 0))],
        out_specs=[pl.BlockSpec((None, PACK_ROWS, A_TOT), lambda l, i: (l, i, 0)),
                   pl.BlockSpec((None, PACK_ROWS, B_TOT), lambda l, i: (l, i, 0))],
        out_shape=[jax.ShapeDtypeStruct((depth, rows, A_TOT), BF16),
                   jax.ShapeDtypeStruct((depth, rows, B_TOT), BF16)],
        compiler_params=_cparams(2),
        name="pack_w_in",
    )(w_in)


def _rep(a, width=128):
    return jnp.repeat(a, width, axis=1)


def kernel(x_prompt, x_sample, state_C, state_n, state_m, cache_meta_k, cache_meta_v, cache_win_k, cache_win_v,
           meta_tokens, norm_mix, w_in, b_igate, b_fgate, norm_mh, attn_sinks, w_out, norm_mlp, w_up, w_down,
           norm_final):
    B, S, _ = x_prompt.shape
    DB = x_sample.shape[0]
    W = cache_win_k.shape[2]
    assert (S % CHUNK, W, x_sample.shape[1], DB % DEC_BLK) == (0, WINDOW, 1, 0)
    assert (state_C.dtype, cache_win_k.dtype, cache_win_v.dtype) == (F32, F32, F32)
    nchunk = S // CHUNK
    kw = N_KV * HD
    big_tm = 1024

    w_a, w_b = _pack_in_weights(w_in)
    tail_w = (w_out.astype(BF16), w_up.astype(BF16), w_down.astype(BF16))
    g_mix = norm_mix.astype(F32).reshape(DEPTH, 1, D_MODEL)
    g_mlp = norm_mlp.astype(F32).reshape(DEPTH, 1, D_MODEL)
    g_next = jnp.concatenate([g_mix, norm_final.astype(F32).reshape(1, 1, D_MODEL)], axis=0)
    nmh = norm_mh.astype(F32).reshape(DEPTH, 1, D_MODEL)
    sinks = attn_sinks.astype(F32)
    sink8 = jnp.broadcast_to(sinks[:, :, None], (DEPTH, N_HEADS, HD))
    bias8 = jnp.broadcast_to(jnp.concatenate([b_igate, b_fgate], axis=1).astype(F32)[:, :, None],
                             (DEPTH, 2 * NH_M, 128))
    bi_rep = _rep(b_igate.astype(F32)).reshape(DEPTH, 1, NH_M * 128)
    bf_rep = _rep(b_fgate.astype(F32)).reshape(DEPTH, 1, NH_M * 128)
    m_rep = jnp.repeat(state_m.astype(F32), 128, axis=2)
    n_all = state_n.astype(F32).reshape(DEPTH, DB, NH_M * DQK_M)
    rows_view = lambda a: a.reshape(DEPTH, DB, -1, HD)
    mk_all, mv_all = rows_view(cache_meta_k), rows_view(cache_meta_v)
    wk_all, wv_all = rows_view(cache_win_k), rows_view(cache_win_v)

    tabs_p = _rope_tables(N_META + jnp.arange(S, dtype=jnp.int32))
    pos_small = jnp.concatenate([jnp.full((DB,), PAST_LEN, jnp.int32), jnp.arange(N_META, dtype=jnp.int32),
                                 jnp.zeros((SMALL_ROWS - DB - N_META,), jnp.int32)])
    tabs_s = _rope_tables(pos_small)

    xp = x_prompt.reshape(B * S, D_MODEL)
    xs = jnp.concatenate([x_sample.reshape(DB, D_MODEL), meta_tokens.astype(F32),
                          jnp.zeros((SMALL_ROWS - DB - N_META, D_MODEL), F32)], axis=0)
    zero_state = (jnp.zeros((1, NH_M, DQK_M, DV_M), F32), jnp.zeros((1, NH_M, DQK_M, 128), F32),
                  jnp.zeros((1, NH_M, 8, 128), F32))

    outs = {k: [] for k in ("ct", "n", "m", "meta", "win", "sn", "sm")}
    sC = swk = swv = None
    hs_ = _norm(0, xs, g_mix, SMALL_ROWS, BF16)
    hp = _norm(0, xp, g_mix, big_tm, BF16)
    for l in range(DEPTH):
        pa_s = _inproj_a(l, hs_, w_a, SMALL_ROWS, F32)
        qa_s, misc_s, kt_s, gt_s, km_s, grep_s = _inproj_b(l, hs_, w_b, tabs_s, SMALL_ROWS, small=True)
        y_meta, ct_m, n_m, m_m = _mix(l, sinks, qa_s, misc_s, pa_s, kt_s, gt_s, bias8, nmh, *zero_state, misc_s,
                                      nb=1, nblk=1, sub=1, row0=META_ROW0, valid_len=N_META, has_meta=False)
        ys_m, sC, sn, sm = _mlstm_step(l, misc_s, km_s, pa_s, grep_s, m_rep, n_all, state_C, sC, bi_rep, bf_rep, nmh)
        ys_a, swk, swv = _swa_step(
            l, qa_s[:DB].reshape(DB, N_HEADS, HD), pa_s[:DB, A_GA:A_GA + D_MODEL].reshape(DB, N_HEADS, HD),
            misc_s, mk_all, mv_all, wk_all, wv_all, swk, swv, sink8)
        y_s = jnp.concatenate([(ys_m + ys_a.reshape(DB, D_MODEL)).astype(BF16), y_meta], axis=0)
        res = _tail(l, y_s, xs, g_mlp, g_next, tail_w, tm=SMALL_ROWS, tf=512)
        xs, hs_ = res[0], res[-1]

        pa = _inproj_a(l, hp, w_a, big_tm, BF16)
        qa, misc, kt, gt = _inproj_b(l, hp, w_b, tabs_p, big_tm, small=False)
        y, ct_f, n_f, m_f = _mix(l, sinks, qa, misc, pa, kt, gt, bias8, nmh, ct_m, n_m, m_m, misc_s,
                                 nb=B, nblk=nchunk, sub=MIX_SUB, row0=0, valid_len=CHUNK, has_meta=True)
        res = _tail(l, y, xp, g_mlp, g_next, tail_w, tm=big_tm, tf=TAIL_TF)
        xp, hp = res[0], res[-1]

        for key, val in (("ct", ct_f), ("n", n_f), ("m", m_f), ("sn", sn), ("sm", sm),
                         ("meta", misc_s[META_ROW0:META_ROW0 + N_META, M_KA:M_VA + kw]),
                         ("win", misc.reshape(B, S, M_TOT)[:, S - WINDOW:, M_KA:M_VA + kw])):
            outs[key].append(val)

    y_prompt = xp.reshape(B, S, D_MODEL)
    y_sample = xs[:DB].reshape(DB, 1, D_MODEL)
    st = lambda k: jnp.stack(outs[k])
    p_c = jnp.swapaxes(st("ct"), 3, 4).astype(state_C.dtype)
    p_n = st("n")[..., 0].astype(state_n.dtype)
    p_m = st("m")[..., 0, 0].astype(state_m.dtype)
    meta = jnp.broadcast_to(st("meta").astype(F32)[:, None], (DEPTH, B, N_META, 2 * kw))
    win = st("win").astype(F32)
    kv_of = lambda a, lo: a[..., lo:lo + kw].reshape(a.shape[:-1] + (N_KV, HD))
    s_n = st("sn").reshape(DEPTH, DB, NH_M, DQK_M).astype(state_n.dtype)
    s_m = st("sm").reshape(DEPTH, DB, NH_M, 128)[..., 0].astype(state_m.dtype)
    return (y_prompt, y_sample, p_c, p_n, p_m, kv_of(meta, 0), kv_of(meta, kw), kv_of(win, 0), kv_of(win, kw),
            sC, s_n, s_m, swk.reshape(cache_win_k.shape), swv.reshape(cache_win_v.shape))
```

```python
import functools

import jax
import jax.numpy as jnp
import numpy as np
from jax import lax
from jax.experimental import pallas as pl
from jax.experimental.pallas import tpu as pltpu

F32 = jnp.float32
BF16 = jnp.bfloat16

D_MODEL = 1024
DEPTH = 4
PAST_LEN = 8192
N_META = 16
NH_M = 4
DV_M = 256
DQK_M = 128
CHUNK = 128
N_HEADS = 8
N_KV = 2
GQ = N_HEADS // N_KV
HD = 128
WINDOW = 128
ROT_DIM = 32
ROPE_THETA = 500000.0
D_FF = 4 * D_MODEL
EPS = 1e-6
IN_WIDTHS = (NH_M * DQK_M, NH_M * DQK_M, NH_M * DV_M, NH_M, NH_M, NH_M * DV_M,
             N_HEADS * HD, N_KV * HD, N_KV * HD, D_MODEL, D_MODEL)

A_VM, A_OM, A_GM, A_GA, A_TOT = 0, 1024, 2048, 3072, 4096
A_TN = 2048
B_QA, B_KA, B_VA, B_QM, B_KM, B_G, B_TOT = 0, 1024, 1280, 1536, 2048, 2560, 2688
M_KA, M_VA, M_QM, M_TOT = 0, 256, 512, 1024
TAIL_TM = 512
TAIL_TF = 1024
MIX_SUB = 4
BLOCK_LAG = 2
MLSTM_STAGES = 8
SMALL_ROWS = 256
META_ROW0 = 128
DEC_BLK = 8
VMEM_LIMIT = 56 * 1024 * 1024


def _cparams(n_axes):
    return pltpu.CompilerParams(dimension_semantics=("arbitrary",) * n_axes, vmem_limit_bytes=VMEM_LIMIT)


def _rms(xf, gain):
    ms = jnp.mean(xf * xf, axis=-1, keepdims=True)
    return xf * lax.rsqrt(ms + EPS) * gain


def _dot(a, b):
    return jnp.dot(a, b, preferred_element_type=F32)


def _dot_nt(a, b):
    return lax.dot_general(a, b, (((1,), (1,)), ((), ())), preferred_element_type=F32)


def _mm(a, b, nt=False):
    return (_dot_nt if nt else _dot)(a.astype(BF16), b.astype(BF16))


def _log_sigmoid(x):
    return jnp.minimum(x, 0.0) - jnp.log1p(jnp.exp(-jnp.abs(x)))


def _run_interleaved(chains):
    chains = list(chains)
    while chains:
        alive = []
        for chain in chains:
            try:
                next(chain)
                alive.append(chain)
            except StopIteration:
                pass
        chains = alive


def _sigmoid_bf16(x):
    half = jnp.asarray(0.5, x.dtype)
    return jnp.tanh(x * half) * half + half


def _norm_kernel(x_ref, gain_ref, o_ref):
    o_ref[...] = _rms(x_ref[...], gain_ref[...]).astype(o_ref.dtype)


def _norm(l, x, gains, tm, out_dtype):
    rows = x.shape[0]
    return pl.pallas_call(
        _norm_kernel,
        grid=(rows // tm,),
        in_specs=[pl.BlockSpec((tm, D_MODEL), lambda i: (i, 0)),
                  pl.BlockSpec((None, 1, D_MODEL), lambda i: (l, 0, 0))],
        out_specs=pl.BlockSpec((tm, D_MODEL), lambda i: (i, 0)),
        out_shape=jax.ShapeDtypeStruct((rows, D_MODEL), out_dtype),
        compiler_params=_cparams(1),
        name="norm",
    )(x, gains)


def _inproj_a_kernel(h_ref, w_ref, o_ref):
    o_ref[...] = _dot(h_ref[...], w_ref[...]).astype(o_ref.dtype)


def _inproj_a(l, h, w_a, tm, out_dtype):
    rows = h.shape[0]
    return pl.pallas_call(
        _inproj_a_kernel,
        grid=(rows // tm, A_TOT // A_TN),
        in_specs=[pl.BlockSpec((tm, D_MODEL), lambda i, j: (i, 0)),
                  pl.BlockSpec((None, D_MODEL, A_TN), lambda i, j: (l, 0, j))],
        out_specs=pl.BlockSpec((tm, A_TN), lambda i, j: (i, j)),
        out_shape=jax.ShapeDtypeStruct((rows, A_TOT), out_dtype),
        compiler_params=_cparams(2),
        name="inproj_a",
    )(h, w_a)


def _inproj_b_kernel(h_ref, w_ref, cs_ref, s1_ref, s2_ref, qa_ref, misc_ref, kt_ref, gt_ref, *row_form):
    small = bool(row_form)
    mm = lambda lo, hi: _dot(h_ref[...], w_ref[:, lo:hi])

    def rope(seg):
        return (seg * cs_ref[...] + pltpu.roll(seg, 16, 1) * s1_ref[...]
                + pltpu.roll(seg, HD - 16, 1) * s2_ref[...])

    acc = mm(B_QA, B_KA)
    for c in range(N_HEADS):
        cs = slice(c * HD, (c + 1) * HD)
        qa_ref[:, cs] = (rope(acc[:, cs]) * (HD ** -0.5)).astype(qa_ref.dtype)

    acc = mm(B_KA, B_KM)
    for c in range(N_KV):
        cs = slice(M_KA + c * HD, M_KA + (c + 1) * HD)
        misc_ref[:, cs] = rope(acc[:, cs]).astype(misc_ref.dtype)
    misc_ref[:, M_VA:M_QM] = acc[:, M_VA:M_QM].astype(misc_ref.dtype)
    misc_ref[:, M_QM:M_TOT] = (acc[:, M_QM:M_TOT] * (DQK_M ** -0.5)).astype(misc_ref.dtype)

    acc = mm(B_KM, B_TOT)
    kt_ref[...] = acc[:, 0:NH_M * DQK_M].T.astype(kt_ref.dtype)
    gt_ref[...] = acc[:, NH_M * DQK_M:B_TOT - B_KM].T[0:2 * NH_M]

    if small:
        km_ref, grep_ref = row_form
        km_ref[...] = acc[:, 0:NH_M * DQK_M]
        for c in range(2 * NH_M):
            col = NH_M * DQK_M + c
            grep_ref[:, c * 128:(c + 1) * 128] = jnp.broadcast_to(acc[:, col:col + 1], (acc.shape[0], 128))


G_REP = 2 * NH_M * 128


def _inproj_b(l, h, w_b, tabs, tm, small):
    rows = h.shape[0]
    tab_blocks = tabs[0].shape[0] // tm
    tab_spec = pl.BlockSpec((tm, HD), lambda i: (i % tab_blocks, 0))
    dt = F32 if small else BF16
    row_spec = lambda width: pl.BlockSpec((tm, width), lambda i: (i, 0))
    out_specs = [row_spec(D_MODEL), row_spec(M_TOT),
                 pl.BlockSpec((NH_M * DQK_M, tm), lambda i: (0, i)),
                 pl.BlockSpec((2 * NH_M, tm), lambda i: (0, i))]
    out_shape = [jax.ShapeDtypeStruct((rows, D_MODEL), dt),
                 jax.ShapeDtypeStruct((rows, M_TOT), dt),
                 jax.ShapeDtypeStruct((NH_M * DQK_M, rows), dt),
                 jax.ShapeDtypeStruct((2 * NH_M, rows), F32)]
    if small:
        out_specs += [row_spec(NH_M * DQK_M), row_spec(G_REP)]
        out_shape += [jax.ShapeDtypeStruct((rows, NH_M * DQK_M), F32), jax.ShapeDtypeStruct((rows, G_REP), F32)]
    return pl.pallas_call(
        _inproj_b_kernel,
        grid=(rows // tm,),
        in_specs=[pl.BlockSpec((tm, D_MODEL), lambda i: (i, 0)),
                  pl.BlockSpec((None, D_MODEL, B_TOT), lambda i: (l, 0, 0)),
                  tab_spec, tab_spec, tab_spec],
        out_specs=out_specs,
        out_shape=out_shape,
        compiler_params=_cparams(1),
        name="inproj_b",
    )(h, w_b, *tabs)


def _mix_kernel(sink_ref, misc_ref, miscp_ref, pa_ref, qa_ref, kt_ref, g_ref, bias_ref, nmh_ref,
                ct0_ref, n0_ref, m0_ref, mk_ref, mv_ref,
                y_ref, ctf_ref, nf_ref, mf_ref, ct_s, n_s, m_s, *, l, nc, sub, valid_len, has_meta):
    c = pl.program_id(1)
    L = CHUNK
    mm = _mm
    cast = lambda a: a.astype(BF16)

    @pl.when(c == 0)
    def _():
        ct_s[...] = ct0_ref[...]
        n_s[...] = n0_ref[...]
        m_s[...] = m0_ref[...]

    row = lax.broadcasted_iota(jnp.int32, (L, L), 0)
    col = lax.broadcasted_iota(jnp.int32, (L, L), 1)
    tril = col <= row
    eye = col == row

    gates = g_ref[...] + bias_ref[:, 0:1]
    ig_sub = gates[0:NH_M]
    lf_sub = _log_sigmoid(gates[NH_M:2 * NH_M])
    if valid_len < L:
        ok = lax.broadcasted_iota(jnp.int32, (NH_M, sub * L), 1) < valid_len
        ig_sub = jnp.where(ok, ig_sub, -jnp.inf)
        lf_sub = jnp.where(ok, lf_sub, 0.0)

    ym, ya, gates_out = {}, {}, {}

    def mlstm_head(k, h):
        rs = slice(k * L, (k + 1) * L)
        ig = ig_sub[h:h + 1, rs]
        lf = lf_sub[h:h + 1, rs]
        b_col = jnp.sum(jnp.where(tril, lf, 0.0), axis=1, keepdims=True)
        b_row = jnp.sum(jnp.where(eye, b_col, 0.0), axis=0, keepdims=True)
        yield
        m_prev = m_s[h, 0:1, 0:1]
        a_col = b_col + m_prev
        dmat = jnp.where(tril, b_col - b_row + ig, -jnp.inf)
        m_col = jnp.maximum(a_col, jnp.max(dmat, axis=1, keepdims=True))
        w_inter = jnp.exp(a_col - m_col)
        wmat = jnp.exp(dmat - m_col)
        m_new = m_col[L - 1:L]
        m_s[h] = jnp.broadcast_to(m_new, (8, 128))
        q = cast(misc_ref[rs, M_QM + h * DQK_M:M_QM + (h + 1) * DQK_M])
        kt = kt_ref[h * DQK_M:(h + 1) * DQK_M, rs]
        v = cast(pa_ref[rs, A_VM + h * DV_M:A_VM + (h + 1) * DV_M])
        yield
        qk = mm(q, kt)
        yield
        s = qk * wmat
        s_b = cast(s)
        yield
        decay = w_inter[L - 1:L]
        g_row = jnp.exp(b_col[L - 1:L] - b_row + ig - m_new)
        ktg = kt.astype(F32) * g_row
        ktg_b = cast(ktg)
        yield
        intra = mm(s_b, v)
        upd = mm(ktg_b, v)
        yield
        ct = ct_s[h]
        nm = n_s[h]
        inter = mm(q, ct)
        qn = mm(q, nm)[:, 0:1]
        yield
        num = w_inter * inter + intra
        den = w_inter * qn + jnp.sum(s, axis=1, keepdims=True)
        hh = num / jnp.maximum(jnp.abs(den), jnp.exp(-m_col))
        ct_s[h] = decay * ct + upd
        n_s[h] = decay * nm + jnp.sum(ktg, axis=1, keepdims=True)
        yield
        hs = slice(h * DV_M, (h + 1) * DV_M)
        hn = hh * lax.rsqrt(jnp.mean(hh * hh, axis=1, keepdims=True) + EPS) * nmh_ref[:, hs]
        ym[k, h] = gates_out[k, h] * hn

    def gate_chain(k):
        rs = slice(k * L, (k + 1) * L)
        yield
        for h in range(NH_M):
            gates_out[k, h] = (_sigmoid_bf16(pa_ref[rs, A_GM + h * DV_M:A_GM + (h + 1) * DV_M])
                               * _sigmoid_bf16(pa_ref[rs, A_OM + h * DV_M:A_OM + (h + 1) * DV_M])
                               ).astype(F32)
            if h == 1:
                yield
        yield
        gates_out[k, "a"] = _sigmoid_bf16(pa_ref[rs, A_GA:A_GA + D_MODEL]).astype(F32)

    def store_chain(k):
        rs = slice(k * L, (k + 1) * L)
        for _ in range(MLSTM_STAGES):
            yield
        for hd in range(N_HEADS):
            ym_h = ym[k, (hd * HD) // DV_M][:, (hd * HD) % DV_M:(hd * HD) % DV_M + HD]
            y_ref[rs, hd * HD:(hd + 1) * HD] = (ym_h + ya[k, hd]).astype(y_ref.dtype)

    def delayed(chain, rounds):
        for _ in range(rounds):
            yield
        yield from chain

    chains = []
    for k in range(sub):
        block = [mlstm_head(k, h) for h in range(NH_M)] + [gate_chain(k)]
        block += [_swa_chain(c, k, kv, sink_ref.at[l], qa_ref, misc_ref, miscp_ref, gates_out, mk_ref, mv_ref, ya,
                             has_meta) for kv in range(N_KV)]
        block.append(store_chain(k))
        chains += [delayed(chain, k * BLOCK_LAG) for chain in block]
    _run_interleaved(chains)

    @pl.when(c == nc - 1)
    def _():
        ctf_ref[0] = ct_s[...]
        nf_ref[0] = n_s[...]
        mf_ref[0] = m_s[...]


def _mix(l, sinks, qa, misc, pa, kt, gt, bias8, nmh, ct0, n0, m0, misc_small, nb, nblk, sub, row0, valid_len,
         has_meta):
    nc = nblk // sub
    rows = sub * CHUNK
    r0 = row0 // rows
    rb = lambda b, c: r0 + b * nc + c
    rp = lambda b, c: (r0 + b * nc + c) * sub - jnp.minimum(c, 1)
    kw = N_KV * HD
    first4 = lambda b, c: (0, 0, 0, 0)
    st4 = lambda b, c: (b, 0, 0, 0)
    return pl.pallas_call(
        functools.partial(_mix_kernel, l=l, nc=nc, sub=sub, valid_len=valid_len, has_meta=has_meta),
        grid=(nb, nc),
        in_specs=[pl.BlockSpec(memory_space=pltpu.SMEM),
                  pl.BlockSpec((rows, M_TOT), lambda b, c: (rb(b, c), 0)),
                  pl.BlockSpec((CHUNK, 2 * kw), lambda b, c: (rp(b, c), 0)),
                  pl.BlockSpec((rows, A_TOT), lambda b, c: (rb(b, c), 0)),
                  pl.BlockSpec((rows, D_MODEL), lambda b, c: (rb(b, c), 0)),
                  pl.BlockSpec((NH_M * DQK_M, rows), lambda b, c: (0, rb(b, c))),
                  pl.BlockSpec((2 * NH_M, rows), lambda b, c: (0, rb(b, c))),
                  pl.BlockSpec((None, 2 * NH_M, 128), lambda b, c: (l, 0, 0)),
                  pl.BlockSpec((None, 1, D_MODEL), lambda b, c: (l, 0, 0)),
                  pl.BlockSpec((None, NH_M, DQK_M, DV_M), first4),
                  pl.BlockSpec((None, NH_M, DQK_M, 128), first4),
                  pl.BlockSpec((None, NH_M, 8, 128), first4),
                  pl.BlockSpec((N_META, kw), lambda b, c: (META_ROW0 // N_META, M_KA // kw)),
                  pl.BlockSpec((N_META, kw), lambda b, c: (META_ROW0 // N_META, M_VA // kw))],
        out_specs=[pl.BlockSpec((rows, D_MODEL), lambda b, c: (b * nc + c, 0)),
                   pl.BlockSpec((1, NH_M, DQK_M, DV_M), st4),
                   pl.BlockSpec((1, NH_M, DQK_M, 128), st4),
                   pl.BlockSpec((1, NH_M, 8, 128), st4)],
        out_shape=[jax.ShapeDtypeStruct((nb * nc * rows, D_MODEL), BF16),
                   jax.ShapeDtypeStruct((nb, NH_M, DQK_M, DV_M), F32),
                   jax.ShapeDtypeStruct((nb, NH_M, DQK_M, 128), F32),
                   jax.ShapeDtypeStruct((nb, NH_M, 8, 128), F32)],
        scratch_shapes=[pltpu.VMEM((NH_M, DQK_M, DV_M), F32),
                        pltpu.VMEM((NH_M, DQK_M, 128), F32),
                        pltpu.VMEM((NH_M, 8, 128), F32)],
        compiler_params=_cparams(2),
        name="mix",
    )(sinks, misc, misc, pa, qa, kt, gt, bias8, nmh, ct0, n0, m0, misc_small, misc_small)


def _swa_chain(c, k, kv, sink_ref, q_ref, misc_ref, miscp_ref, gates_out, mk_ref, mv_ref, ya, has_meta):
    L = WINDOW
    R = GQ * L
    rs = slice(k * L, (k + 1) * L)
    row = lax.broadcasted_iota(jnp.int32, (R, L), 0)
    col = lax.broadcasted_iota(jnp.int32, (R, L), 1)
    tril = col <= (row & (L - 1))
    head_of_row = lax.shift_right_logical(lax.broadcasted_iota(jnp.int32, (R, 1), 0), L.bit_length() - 1)
    mm = _mm
    cast = lambda a: a.astype(BF16)
    with_ones = lambda v: jnp.concatenate([v, jnp.ones((v.shape[0], HD), v.dtype)], axis=1)
    if k == 0:
        prev = lambda cols: cast(miscp_ref[:, cols])
        prev_ok = jnp.full((R, L), c, jnp.int32) > 0
    else:
        prev = lambda cols: cast(misc_ref[(k - 1) * L:k * L, cols])
        prev_ok = None
    ka = slice(M_KA + kv * HD, M_KA + (kv + 1) * HD)
    va = slice(M_VA + kv * HD, M_VA + (kv + 1) * HD)
    ks = slice(kv * HD, (kv + 1) * HD)
    heads = [kv * GQ + g for g in range(GQ)]
    q = jnp.concatenate([cast(q_ref[rs, hd * HD:(hd + 1) * HD]) for hd in heads], axis=0)
    sink = jnp.zeros((R, 1), F32)
    for g, hd in enumerate(heads):
        sink = jnp.where(head_of_row == g, sink_ref[hd], sink)
    yield
    yield
    s_cur = mm(q, cast(misc_ref[rs, ka]), nt=True)
    s_prev = mm(q, prev(ka), nt=True)
    if has_meta:
        s_m = mm(q, cast(mk_ref[:, ks]), nt=True)
    yield
    if prev_ok is not None:
        s_prev = jnp.where(prev_ok, s_prev, -jnp.inf)
    band = jnp.where(tril, s_cur, s_prev)
    mx = jnp.maximum(jnp.max(band, axis=1, keepdims=True), sink)
    if has_meta:
        mx = jnp.maximum(mx, jnp.max(s_m, axis=1, keepdims=True))
        p_m = cast(jnp.exp(s_m - mx))
    yield
    p = jnp.exp(band - mx)
    p_cur = cast(jnp.where(tril, p, 0.0))
    p_prev = cast(jnp.where(tril, 0.0, p))
    yield
    oe = (mm(p_cur, with_ones(cast(misc_ref[rs, va])))
          + mm(p_prev, with_ones(prev(va))))
    if has_meta:
        oe = oe + mm(p_m, with_ones(cast(mv_ref[:, ks])))
    yield
    denom = oe[:, HD:HD + 1] + jnp.exp(sink - mx)
    o = oe[:, 0:HD] / denom
    yield
    for g, hd in enumerate(heads):
        ya[k, hd] = o[g * L:(g + 1) * L] * gates_out[k, "a"][:, hd * HD:(hd + 1) * HD]


def _tail_kernel(y_ref, x_ref, gn_ref, gnext_ref, wo_ref, wu_ref, wd_ref, o_ref, *rest, tf, last_layer):
    x1 = x_ref[...] + _dot(y_ref[...], wo_ref[...])
    hf = _rms(x1, gn_ref[...]).astype(BF16)
    acc = x1
    for j in range(D_FF // tf):
        a = jnp.square(jnp.maximum(_dot(hf, wu_ref[:, j * tf:(j + 1) * tf]), 0.0)).astype(BF16)
        acc = acc + _dot(a, wd_ref[j * tf:(j + 1) * tf, :])
    if last_layer:
        o_ref[...] = _rms(acc, gnext_ref[...])
    else:
        o_ref[...] = acc
        rest[0][...] = _rms(acc, gnext_ref[...]).astype(BF16)


def _tail(l, y, x, g_mlp, g_next, weights, tm, tf):
    rows = x.shape[0]
    last_layer = l == DEPTH - 1
    row_spec = pl.BlockSpec((tm, D_MODEL), lambda i: (i, 0))
    resident = lambda r, c: pl.BlockSpec((None, r, c), lambda i: (l, 0, 0), pipeline_mode=pl.Buffered(1))
    out_specs, out_shape = [row_spec], [jax.ShapeDtypeStruct((rows, D_MODEL), F32)]
    if not last_layer:
        out_specs.append(row_spec)
        out_shape.append(jax.ShapeDtypeStruct((rows, D_MODEL), BF16))
    return pl.pallas_call(
        functools.partial(_tail_kernel, tf=tf, last_layer=last_layer),
        grid=(rows // tm,),
        in_specs=[row_spec, row_spec,
                  pl.BlockSpec((None, 1, D_MODEL), lambda i: (l, 0, 0)),
                  pl.BlockSpec((None, 1, D_MODEL), lambda i: (l + 1, 0, 0)),
                  resident(D_MODEL, D_MODEL), resident(D_MODEL, D_FF), resident(D_FF, D_MODEL)],
        out_specs=out_specs,
        out_shape=out_shape,
        compiler_params=_cparams(1),
        name="tail",
    )(y, x, g_mlp, g_next, *weights)


def _mlstm_step_kernel(q_ref, k_ref, v_ref, o_ref, gm_ref, ig_ref, fg_ref, m_ref, n_ref, c_ref,
                       kall_ref, vall_ref, bi_ref, bf_ref, nmh_ref, *rest):
    y_ref, co_ref, no_ref, mo_ref = rest[-4:]
    i = pl.program_id(0)
    nb = DEC_BLK
    row8 = lax.broadcasted_iota(jnp.int32, (nb, DV_M), 0)
    rowb = lax.broadcasted_iota(jnp.int32, (kall_ref.shape[0], DQK_M), 0)
    for h in range(NH_M):
        ds = slice(h * DQK_M, (h + 1) * DQK_M)
        vs = slice(h * DV_M, (h + 1) * DV_M)
        q, k, v, nrm = q_ref[:, ds], k_ref[:, ds], v_ref[:, vs], n_ref[:, ds]
        ig = ig_ref[:, ds] + bi_ref[:, ds]
        lf = _log_sigmoid(fg_ref[:, ds] + bf_ref[:, ds])
        m_prev = m_ref[:, ds]
        m_t = jnp.maximum(lf + m_prev, ig)
        w_inter = jnp.exp(lf + m_prev - m_t)
        w_in = jnp.exp(ig - m_t)
        s = jnp.sum(q * k, axis=1, keepdims=True) * w_in
        qb = q.astype(BF16)
        kall = kall_ref[:, ds].astype(BF16)
        vt = vall_ref[:, vs].T.astype(BF16)
        cq = jnp.zeros((nb, DV_M), F32)
        for bb in range(nb):
            c_old = c_ref[bb, h]
            cq = jnp.where(row8 == bb, _dot_nt(qb, c_old.astype(BF16)), cq)
            outer = _dot(vt, jnp.where(rowb == i * nb + bb, kall, jnp.zeros_like(kall)))
            co_ref[bb, h] = w_inter[bb:bb + 1, :] * c_old + w_in[bb:bb + 1, :] * outer
        num = w_inter[:, 0:1] * cq + s[:, 0:1] * v
        den = w_inter * jnp.sum(nrm * q, axis=1, keepdims=True) + s
        hh = num / jnp.maximum(jnp.abs(den), jnp.exp(-m_t))[:, 0:1]
        no_ref[:, ds] = w_inter * nrm + w_in * k
        mo_ref[:, ds] = m_t
        hn = hh * lax.rsqrt(jnp.mean(hh * hh, axis=1, keepdims=True) + EPS) * nmh_ref[:, vs]
        y_ref[:, vs] = jax.nn.sigmoid(gm_ref[:, vs]) * (jax.nn.sigmoid(o_ref[:, vs]) * hn)


def _mlstm_step(l, misc, km, pa, grep, m_all, n_all, c_all, c_out_prev, bi, bf, nmh):
    nbatch = c_all.shape[1]
    qd = NH_M * DQK_M
    rowq = lambda j: pl.BlockSpec((DEC_BLK, qd), lambda i: (i, j))
    rowv = lambda j: pl.BlockSpec((DEC_BLK, D_MODEL), lambda i: (i, j))
    lrow = pl.BlockSpec((None, DEC_BLK, qd), lambda i: (l, i, 0))
    lconst = lambda width: pl.BlockSpec((None, 1, width), lambda i: (l, 0, 0))
    cspec = pl.BlockSpec((None, DEC_BLK, NH_M, DV_M, DQK_M), lambda i: (l, i, 0, 0, 0))
    in_specs = [rowq(M_QM // qd), rowq(0), rowv(A_VM // D_MODEL), rowv(A_OM // D_MODEL), rowv(A_GM // D_MODEL),
                rowq(0), rowq(1), lrow, lrow, cspec,
                pl.BlockSpec((nbatch, qd), lambda i: (0, 0)),
                pl.BlockSpec((nbatch, D_MODEL), lambda i: (0, A_VM // D_MODEL)),
                lconst(qd), lconst(qd), lconst(D_MODEL)]
    args = [misc, km, pa, pa, pa, grep, grep, m_all, n_all, c_all, km, pa, bi, bf, nmh]
    aliases = {}
    if c_out_prev is not None:
        aliases = {len(args): 1}
        in_specs.append(pl.BlockSpec(memory_space=pl.ANY))
        args.append(c_out_prev)
    return pl.pallas_call(
        _mlstm_step_kernel,
        grid=(nbatch // DEC_BLK,),
        in_specs=in_specs,
        out_specs=[rowv(0), cspec, rowq(0), rowq(0)],
        out_shape=[jax.ShapeDtypeStruct((nbatch, D_MODEL), F32),
                   jax.ShapeDtypeStruct(c_all.shape, F32),
                   jax.ShapeDtypeStruct((nbatch, qd), F32),
                   jax.ShapeDtypeStruct((nbatch, qd), F32)],
        input_output_aliases=aliases,
        compiler_params=_cparams(1),
        name="mlstm_step",
    )(*args)


def _swa_step_kernel(q_ref, ga_ref, kn_ref, vn_ref, mk_ref, mv_ref, wk_ref, wv_ref, sink_ref, *rest):
    y_ref, ok_ref, ov_ref = rest[-3:]
    W = WINDOW
    hrow = lax.broadcasted_iota(jnp.int32, (N_HEADS, HD), 0)
    first = hrow < GQ
    wcol = lax.broadcasted_iota(jnp.int32, (N_HEADS, W), 1)
    wrow = lax.broadcasted_iota(jnp.int32, (N_KV * W, HD), 0)
    sink = sink_ref[...][:, 0:1]

    def per_kv(fn):
        a0, a1 = fn(0), fn(1)
        return jnp.where(first[:, 0:a0.shape[1]], a0, a1)

    def batch_element(bb):
        q = q_ref[bb]
        qb = q.astype(BF16)
        rows_of = lambda ref, kv, n: ref[bb, pl.ds(kv, n, stride=N_KV), :].astype(BF16)
        kn, vn = kn_ref[bb:bb + 1, :], vn_ref[bb:bb + 1, :]
        kvs = lambda a, kv: a[:, kv * HD:(kv + 1) * HD]
        s_w = jnp.where(wcol >= 1, per_kv(lambda kv: _dot_nt(qb, rows_of(wk_ref, kv, W))), -jnp.inf)
        s_m = per_kv(lambda kv: _dot_nt(qb, rows_of(mk_ref, kv, N_META)))
        kn8 = per_kv(lambda kv: jnp.broadcast_to(kvs(kn, kv), (N_HEADS, HD)))
        vn8 = per_kv(lambda kv: jnp.broadcast_to(kvs(vn, kv), (N_HEADS, HD)))
        s_n = jnp.sum(q * kn8, axis=1, keepdims=True)
        yield
        mx = jnp.maximum(jnp.maximum(jnp.max(s_w, axis=1, keepdims=True), jnp.max(s_m, axis=1, keepdims=True)),
                         jnp.maximum(s_n, sink))
        p_w, p_m, p_n = jnp.exp(s_w - mx), jnp.exp(s_m - mx), jnp.exp(s_n - mx)
        denom = (jnp.sum(p_w, axis=1, keepdims=True) + jnp.sum(p_m, axis=1, keepdims=True) + p_n
                 + jnp.exp(sink - mx))
        yield
        o = (per_kv(lambda kv: _dot(p_w.astype(BF16), rows_of(wv_ref, kv, W)))
             + per_kv(lambda kv: _dot(p_m.astype(BF16), rows_of(mv_ref, kv, N_META)))
             + p_n * vn8)
        yield
        y_ref[bb] = (o / denom) * jax.nn.sigmoid(ga_ref[bb])
        for src_ref, new, dst_ref in ((wk_ref, kn, ok_ref), (wv_ref, vn, ov_ref)):
            shifted = pltpu.roll(src_ref[bb], N_KV * (W - 1), 0)
            for kv in range(N_KV):
                shifted = jnp.where(wrow == N_KV * (W - 1) + kv, kvs(new, kv), shifted)
            dst_ref[bb] = shifted

    _run_interleaved([batch_element(bb) for bb in range(DEC_BLK)])


def _swa_step(l, q, ga, misc, mk_all, mv_all, wk_all, wv_all, ok_prev, ov_prev, sink8):
    nbatch = q.shape[0]
    kw = N_KV * HD
    hspec = pl.BlockSpec((DEC_BLK, N_HEADS, HD), lambda i: (i, 0, 0))
    mspec = pl.BlockSpec((None, DEC_BLK, N_KV * N_META, HD), lambda i: (l, i, 0, 0))
    wspec = pl.BlockSpec((None, DEC_BLK, N_KV * WINDOW, HD), lambda i: (l, i, 0, 0))
    in_specs = [hspec, hspec,
                pl.BlockSpec((DEC_BLK, kw), lambda i: (i, M_KA // kw)),
                pl.BlockSpec((DEC_BLK, kw), lambda i: (i, M_VA // kw)),
                mspec, mspec, wspec, wspec,
                pl.BlockSpec((None, N_HEADS, HD), lambda i: (l, 0, 0))]
    args = [q, ga, misc, misc, mk_all, mv_all, wk_all, wv_all, sink8]
    aliases = {}
    if ok_prev is not None:
        aliases = {len(args): 1, len(args) + 1: 2}
        in_specs += [pl.BlockSpec(memory_space=pl.ANY)] * 2
        args += [ok_prev, ov_prev]
    return pl.pallas_call(
        _swa_step_kernel,
        grid=(nbatch // DEC_BLK,),
        in_specs=in_specs,
        out_specs=[hspec, wspec, wspec],
        out_shape=[jax.ShapeDtypeStruct((nbatch, N_HEADS, HD), F32),
                   jax.ShapeDtypeStruct(wk_all.shape, F32),
                   jax.ShapeDtypeStruct(wv_all.shape, F32)],
        input_output_aliases=aliases,
        compiler_params=_cparams(1),
        name="swa_step",
    )(*args)


def _rope_tables(pos):
    half = ROT_DIM // 2
    inv = jnp.float32(ROPE_THETA) ** (-jnp.arange(half, dtype=F32) * 2.0 / ROT_DIM)
    ang = pos.astype(F32)[:, None] * inv[None, :]
    cos, sin = jnp.cos(ang), jnp.sin(ang)
    t = pos.shape[0]
    cs = jnp.concatenate([cos, cos, jnp.ones((t, HD - ROT_DIM), F32)], axis=1)
    s1 = jnp.concatenate([jnp.zeros((t, half), F32), sin, jnp.zeros((t, HD - ROT_DIM), F32)], axis=1)
    s2 = jnp.concatenate([-sin, jnp.zeros((t, HD - half), F32)], axis=1)
    return cs, s1, s2


PACK_ROWS = 128


def _pack_kernel(wt_ref, wa_ref, wb_ref):
    edges = np.concatenate([[0], np.cumsum(IN_WIDTHS)])
    q_m, k_m, v_m, ig, fg, o_m, q_a, k_a, v_a, g_m, g_a = [(int(edges[i]), int(edges[i + 1])) for i in range(11)]

    def put(dst_ref, dst0, seg):
        lo, hi = seg
        dst_ref[:, dst0:dst0 + hi - lo] = wt_ref[lo:hi, :].T.astype(BF16)

    for dst0, seg in ((A_VM, v_m), (A_OM, o_m), (A_GM, g_m), (A_GA, g_a)):
        put(wa_ref, dst0, seg)
    for dst0, seg in ((B_QA, q_a), (B_KA, k_a), (B_VA, v_a), (B_QM, q_m), (B_KM, k_m)):
        put(wb_ref, dst0, seg)
    tile = wt_ref[ig[0]:ig[0] + 128, :].T
    lane = lax.broadcasted_iota(jnp.int32, tile.shape, 1)
    wb_ref[:, B_G:B_TOT] = jnp.where(lane < 2 * NH_M, tile, 0.0).astype(BF16)


def _pack_in_weights(w_in):
    starts = np.concatenate([[0], np.cumsum(IN_WIDTHS)[:-1]])
    assert IN_WIDTHS[3] == IN_WIDTHS[4] == NH_M and all(s % 8 == 0 for i, s in enumerate(starts) if i != 4)
    depth, rows, cols = w_in.shape
    return pl.pallas_call(
        _pack_kernel,
        grid=(depth, rows // PACK_ROWS),
        in_specs=[pl.BlockSpec((None, cols, PACK_ROWS), lambda l, i: (l, 0, i))],
        out_specs=[pl.BlockSpec((None, PACK_ROWS, A_TOT), lambda l, i: (l, i, 0)),
                   pl.BlockSpec((None, PACK_ROWS, B_TOT), lambda l, i: (l, i, 0))],
        out_shape=[jax.ShapeDtypeStruct((depth, rows, A_TOT), BF16),
                   jax.ShapeDtypeStruct((depth, rows, B_TOT), BF16)],
        compiler_params=_cparams(2),
        name="pack_w_in",
    )(jnp.swapaxes(w_in, 1, 2))


def _rep(a, width=128):
    return jnp.repeat(a, width, axis=1)


def kernel(x_prompt, x_sample, state_C, state_n, state_m, cache_meta_k, cache_meta_v, cache_win_k, cache_win_v,
           meta_tokens, norm_mix, w_in, b_igate, b_fgate, norm_mh, attn_sinks, w_out, norm_mlp, w_up, w_down,
           norm_final):
    B, S, _ = x_prompt.shape
    DB = x_sample.shape[0]
    W = cache_win_k.shape[2]
    assert (S % CHUNK, W, x_sample.shape[1], DB % DEC_BLK) == (0, WINDOW, 1, 0)
    assert (state_C.dtype, cache_win_k.dtype, cache_win_v.dtype) == (F32, F32, F32)
    nchunk = S // CHUNK
    kw = N_KV * HD
    big_tm = 1024

    w_a, w_b = _pack_in_weights(w_in)
    tail_w = (w_out.astype(BF16), w_up.astype(BF16), w_down.astype(BF16))
    g_mix = norm_mix.astype(F32).reshape(DEPTH, 1, D_MODEL)
    g_mlp = norm_mlp.astype(F32).reshape(DEPTH, 1, D_MODEL)
    g_next = jnp.concatenate([g_mix, norm_final.astype(F32).reshape(1, 1, D_MODEL)], axis=0)
    nmh = norm_mh.astype(F32).reshape(DEPTH, 1, D_MODEL)
    sinks = attn_sinks.astype(F32)
    sink8 = jnp.broadcast_to(sinks[:, :, None], (DEPTH, N_HEADS, HD))
    bias8 = jnp.broadcast_to(jnp.concatenate([b_igate, b_fgate], axis=1).astype(F32)[:, :, None],
                             (DEPTH, 2 * NH_M, 128))
    bi_rep = _rep(b_igate.astype(F32)).reshape(DEPTH, 1, NH_M * 128)
    bf_rep = _rep(b_fgate.astype(F32)).reshape(DEPTH, 1, NH_M * 128)
    m_rep = jnp.repeat(state_m.astype(F32), 128, axis=2)
    n_all = state_n.astype(F32).reshape(DEPTH, DB, NH_M * DQK_M)
    rows_view = lambda a: a.reshape(DEPTH, DB, -1, HD)
    mk_all, mv_all = rows_view(cache_meta_k), rows_view(cache_meta_v)
    wk_all, wv_all = rows_view(cache_win_k), rows_view(cache_win_v)

    tabs_p = _rope_tables(N_META + jnp.arange(S, dtype=jnp.int32))
    pos_small = jnp.concatenate([jnp.full((DB,), PAST_LEN, jnp.int32), jnp.arange(N_META, dtype=jnp.int32),
                                 jnp.zeros((SMALL_ROWS - DB - N_META,), jnp.int32)])
    tabs_s = _rope_tables(pos_small)

    xp = x_prompt.reshape(B * S, D_MODEL)
    xs = jnp.concatenate([x_sample.reshape(DB, D_MODEL), meta_tokens.astype(F32),
                          jnp.zeros((SMALL_ROWS - DB - N_META, D_MODEL), F32)], axis=0)
    zero_state = (jnp.zeros((1, NH_M, DQK_M, DV_M), F32), jnp.zeros((1, NH_M, DQK_M, 128), F32),
                  jnp.zeros((1, NH_M, 8, 128), F32))

    outs = {k: [] for k in ("ct", "n", "m", "meta", "win", "sn", "sm")}
    sC = swk = swv = None
    hs_ = _norm(0, xs, g_mix, SMALL_ROWS, BF16)
    hp = _norm(0, xp, g_mix, big_tm, BF16)
    for l in range(DEPTH):
        pa_s = _inproj_a(l, hs_, w_a, SMALL_ROWS, F32)
        qa_s, misc_s, kt_s, gt_s, km_s, grep_s = _inproj_b(l, hs_, w_b, tabs_s, SMALL_ROWS, small=True)
        y_meta, ct_m, n_m, m_m = _mix(l, sinks, qa_s, misc_s, pa_s, kt_s, gt_s, bias8, nmh, *zero_state, misc_s,
                                      nb=1, nblk=1, sub=1, row0=META_ROW0, valid_len=N_META, has_meta=False)
        ys_m, sC, sn, sm = _mlstm_step(l, misc_s, km_s, pa_s, grep_s, m_rep, n_all, state_C, sC, bi_rep, bf_rep, nmh)
        ys_a, swk, swv = _swa_step(
            l, qa_s[:DB].reshape(DB, N_HEADS, HD), pa_s[:DB, A_GA:A_GA + D_MODEL].reshape(DB, N_HEADS, HD),
            misc_s, mk_all, mv_all, wk_all, wv_all, swk, swv, sink8)
        y_s = jnp.concatenate([(ys_m + ys_a.reshape(DB, D_MODEL)).astype(BF16), y_meta], axis=0)
        res = _tail(l, y_s, xs, g_mlp, g_next, tail_w, tm=SMALL_ROWS, tf=TAIL_TF)
        xs, hs_ = res[0], res[-1]

        pa = _inproj_a(l, hp, w_a, big_tm, BF16)
        qa, misc, kt, gt = _inproj_b(l, hp, w_b, tabs_p, big_tm, small=False)
        y, ct_f, n_f, m_f = _mix(l, sinks, qa, misc, pa, kt, gt, bias8, nmh, ct_m, n_m, m_m, misc_s,
                                 nb=B, nblk=nchunk, sub=MIX_SUB, row0=0, valid_len=CHUNK, has_meta=True)
        res = _tail(l, y, xp, g_mlp, g_next, tail_w, tm=TAIL_TM, tf=TAIL_TF)
        xp, hp = res[0], res[-1]

        for key, val in (("ct", ct_f), ("n", n_f), ("m", m_f), ("sn", sn), ("sm", sm),
                         ("meta", misc_s[META_ROW0:META_ROW0 + N_META, M_KA:M_VA + kw]),
                         ("win", misc.reshape(B, S, M_TOT)[:, S - WINDOW:, M_KA:M_VA + kw])):
            outs[key].append(val)

    y_prompt = xp.reshape(B, S, D_MODEL)
    y_sample = xs[:DB].reshape(DB, 1, D_MODEL)
    st = lambda k: jnp.stack(outs[k])
    p_c = jnp.swapaxes(st("ct"), 3, 4).astype(state_C.dtype)
    p_n = st("n")[..., 0].astype(state_n.dtype)
    p_m = st("m")[..., 0, 0].astype(state_m.dtype)
    meta = jnp.broadcast_to(st("meta").astype(F32)[:, None], (DEPTH, B, N_META, 2 * kw))
    win = st("win").astype(F32)
    kv_of = lambda a, lo: a[..., lo:lo + kw].reshape(a.shape[:-1] + (N_KV, HD))
    s_n = st("sn").reshape(DEPTH, DB, NH_M, DQK_M).astype(state_n.dtype)
    s_m = st("sm").reshape(DEPTH, DB, NH_M, 128)[..., 0].astype(state_m.dtype)
    return (y_prompt, y_sample, p_c, p_n, p_m, kv_of(meta, 0), kv_of(meta, kw), kv_of(win, 0), kv_of(win, kw),
            sC, s_n, s_m, swk.reshape(cache_win_k.shape), swv.reshape(cache_win_v.shape))
```

```python
import functools

import jax
import jax.numpy as jnp
import numpy as np
from jax import lax
from jax.experimental import pallas as pl
from jax.experimental.pallas import tpu as pltpu

F32 = jnp.float32
BF16 = jnp.bfloat16

D_MODEL = 1024
DEPTH = 4
PAST_LEN = 8192
N_META = 16
NH_M = 4
DV_M = 256
DQK_M = 128
CHUNK = 128
N_HEADS = 8
N_KV = 2
GQ = N_HEADS // N_KV
HD = 128
WINDOW = 128
ROT_DIM = 32
ROPE_THETA = 500000.0
D_FF = 4 * D_MODEL
EPS = 1e-6
IN_WIDTHS = (NH_M * DQK_M, NH_M * DQK_M, NH_M * DV_M, NH_M, NH_M, NH_M * DV_M,
             N_HEADS * HD, N_KV * HD, N_KV * HD, D_MODEL, D_MODEL)

A_VM, A_OM, A_GM, A_GA, A_TOT = 0, 1024, 2048, 3072, 4096
A_TN = 1024
INPROJ_TM = 512
B_QA, B_KA, B_VA, B_QM, B_KM, B_G, B_TOT = 0, 1024, 1280, 1536, 2048, 2560, 2688
M_KA, M_VA, M_QM, M_TOT = 0, 256, 512, 1024
TAIL_TM = 512
TAIL_TF = 1024
MIX_SUB = 8
BLOCK_LAG = 2
MLSTM_STAGES = 8
SMALL_ROWS = 256
META_ROW0 = 128
DEC_BLK = 8
VMEM_LIMIT = 56 * 1024 * 1024


def _cparams(n_axes):
    return pltpu.CompilerParams(dimension_semantics=("arbitrary",) * n_axes, vmem_limit_bytes=VMEM_LIMIT)


def _rms(xf, gain):
    ms = jnp.mean(xf * xf, axis=-1, keepdims=True)
    return xf * lax.rsqrt(ms + EPS) * gain


def _dot(a, b):
    return jnp.dot(a, b, preferred_element_type=F32)


def _dot_nt(a, b):
    return lax.dot_general(a, b, (((1,), (1,)), ((), ())), preferred_element_type=F32)


def _mm(a, b, nt=False):
    return (_dot_nt if nt else _dot)(a.astype(BF16), b.astype(BF16))


def _log_sigmoid(x):
    return jnp.minimum(x, 0.0) - jnp.log1p(jnp.exp(-jnp.abs(x)))


def _run_interleaved(chains):
    chains = list(chains)
    while chains:
        alive = []
        for chain in chains:
            try:
                next(chain)
                alive.append(chain)
            except StopIteration:
                pass
        chains = alive


def _sigmoid_bf16(x):
    half = jnp.asarray(0.5, x.dtype)
    return jnp.tanh(x * half) * half + half


def _norm_kernel(x_ref, gain_ref, o_ref):
    o_ref[...] = _rms(x_ref[...], gain_ref[...]).astype(o_ref.dtype)


def _norm(l, x, gains, tm, out_dtype):
    rows = x.shape[0]
    return pl.pallas_call(
        _norm_kernel,
        grid=(rows // tm,),
        in_specs=[pl.BlockSpec((tm, D_MODEL), lambda i: (i, 0)),
                  pl.BlockSpec((None, 1, D_MODEL), lambda i: (l, 0, 0))],
        out_specs=pl.BlockSpec((tm, D_MODEL), lambda i: (i, 0)),
        out_shape=jax.ShapeDtypeStruct((rows, D_MODEL), out_dtype),
        compiler_params=_cparams(1),
        name="norm",
    )(x, gains)


def _inproj_kernel(h_ref, wa_ref, w_ref, cs_ref, s1_ref, s2_ref, pa_ref, qa_ref, misc_ref, kt_ref, gt_ref,
                   *row_form):
    small = bool(row_form)
    mm = lambda lo, hi: _dot(h_ref[...], w_ref[:, lo:hi])

    for c in range(A_TOT // A_TN):
        cs = slice(c * A_TN, (c + 1) * A_TN)
        pa_ref[:, cs] = _dot(h_ref[...], wa_ref[:, cs]).astype(pa_ref.dtype)

    def rope(seg):
        return (seg * cs_ref[...] + pltpu.roll(seg, 16, 1) * s1_ref[...]
                + pltpu.roll(seg, HD - 16, 1) * s2_ref[...])

    acc = mm(B_QA, B_KA)
    for c in range(N_HEADS):
        cs = slice(c * HD, (c + 1) * HD)
        qa_ref[:, cs] = (rope(acc[:, cs]) * (HD ** -0.5)).astype(qa_ref.dtype)

    acc = mm(B_KA, B_KM)
    for c in range(N_KV):
        cs = slice(M_KA + c * HD, M_KA + (c + 1) * HD)
        misc_ref[:, cs] = rope(acc[:, cs]).astype(misc_ref.dtype)
    misc_ref[:, M_VA:M_QM] = acc[:, M_VA:M_QM].astype(misc_ref.dtype)
    misc_ref[:, M_QM:M_TOT] = (acc[:, M_QM:M_TOT] * (DQK_M ** -0.5)).astype(misc_ref.dtype)

    acc = mm(B_KM, B_TOT)
    kt_ref[...] = acc[:, 0:NH_M * DQK_M].T.astype(kt_ref.dtype)
    gt_ref[...] = acc[:, NH_M * DQK_M:B_TOT - B_KM].T[0:2 * NH_M]

    if small:
        km_ref, grep_ref = row_form
        km_ref[...] = acc[:, 0:NH_M * DQK_M]
        for c in range(2 * NH_M):
            col = NH_M * DQK_M + c
            grep_ref[:, c * 128:(c + 1) * 128] = jnp.broadcast_to(acc[:, col:col + 1], (acc.shape[0], 128))


G_REP = 2 * NH_M * 128


def _inproj(l, h, w_a, w_b, tabs, tm, small):
    rows = h.shape[0]
    tab_blocks = tabs[0].shape[0] // tm
    tab_spec = pl.BlockSpec((tm, HD), lambda i: (i % tab_blocks, 0))
    dt = F32 if small else BF16
    row_spec = lambda width: pl.BlockSpec((tm, width), lambda i: (i, 0))
    resident = lambda cols: pl.BlockSpec((None, D_MODEL, cols), lambda i: (l, 0, 0), pipeline_mode=pl.Buffered(1))
    out_specs = [row_spec(A_TOT), row_spec(D_MODEL), row_spec(M_TOT),
                 pl.BlockSpec((NH_M * DQK_M, tm), lambda i: (0, i)),
                 pl.BlockSpec((2 * NH_M, tm), lambda i: (0, i))]
    out_shape = [jax.ShapeDtypeStruct((rows, A_TOT), dt),
                 jax.ShapeDtypeStruct((rows, D_MODEL), dt),
                 jax.ShapeDtypeStruct((rows, M_TOT), dt),
                 jax.ShapeDtypeStruct((NH_M * DQK_M, rows), dt),
                 jax.ShapeDtypeStruct((2 * NH_M, rows), F32)]
    if small:
        out_specs += [row_spec(NH_M * DQK_M), row_spec(G_REP)]
        out_shape += [jax.ShapeDtypeStruct((rows, NH_M * DQK_M), F32), jax.ShapeDtypeStruct((rows, G_REP), F32)]
    return pl.pallas_call(
        _inproj_kernel,
        grid=(rows // tm,),
        in_specs=[pl.BlockSpec((tm, D_MODEL), lambda i: (i, 0)), resident(A_TOT), resident(B_TOT),
                  tab_spec, tab_spec, tab_spec],
        out_specs=out_specs,
        out_shape=out_shape,
        compiler_params=_cparams(1),
        name="inproj",
    )(h, w_a, w_b, *tabs)


def _mix_kernel(sink_ref, misc_ref, miscp_ref, pa_ref, qa_ref, kt_ref, g_ref, bias_ref, nmh_ref,
                ct0_ref, n0_ref, m0_ref, mk_ref, mv_ref,
                y_ref, ctf_ref, nf_ref, mf_ref, ct_s, n_s, m_s, *, l, nc, sub, valid_len, has_meta):
    c = pl.program_id(1)
    L = CHUNK
    mm = _mm
    cast = lambda a: a.astype(BF16)

    @pl.when(c == 0)
    def _():
        ct_s[...] = ct0_ref[...]
        n_s[...] = n0_ref[...]
        m_s[...] = m0_ref[...]

    row = lax.broadcasted_iota(jnp.int32, (L, L), 0)
    col = lax.broadcasted_iota(jnp.int32, (L, L), 1)
    tril = col <= row
    eye = col == row

    gates = g_ref[...] + bias_ref[:, 0:1]
    ig_sub = gates[0:NH_M]
    lf_sub = _log_sigmoid(gates[NH_M:2 * NH_M])
    if valid_len < L:
        ok = lax.broadcasted_iota(jnp.int32, (NH_M, sub * L), 1) < valid_len
        ig_sub = jnp.where(ok, ig_sub, -jnp.inf)
        lf_sub = jnp.where(ok, lf_sub, 0.0)

    ym, ya, gates_out = {}, {}, {}

    def mlstm_head(k, h):
        rs = slice(k * L, (k + 1) * L)
        ig = ig_sub[h:h + 1, rs]
        lf = lf_sub[h:h + 1, rs]
        b_col = jnp.sum(jnp.where(tril, lf, 0.0), axis=1, keepdims=True)
        b_row = jnp.sum(jnp.where(eye, b_col, 0.0), axis=0, keepdims=True)
        yield
        m_prev = m_s[h, 0:1, 0:1]
        a_col = b_col + m_prev
        dmat = jnp.where(tril, b_col - b_row + ig, -jnp.inf)
        m_col = jnp.maximum(a_col, jnp.max(dmat, axis=1, keepdims=True))
        w_inter = jnp.exp(a_col - m_col)
        wmat = jnp.exp(dmat - m_col)
        m_new = m_col[L - 1:L]
        m_s[h] = jnp.broadcast_to(m_new, (8, 128))
        q = cast(misc_ref[rs, M_QM + h * DQK_M:M_QM + (h + 1) * DQK_M])
        kt = kt_ref[h * DQK_M:(h + 1) * DQK_M, rs]
        v = cast(pa_ref[rs, A_VM + h * DV_M:A_VM + (h + 1) * DV_M])
        yield
        qk = mm(q, kt)
        yield
        s = qk * wmat
        s_b = cast(s)
        yield
        decay = w_inter[L - 1:L]
        g_row = jnp.exp(b_col[L - 1:L] - b_row + ig - m_new)
        ktg = kt.astype(F32) * g_row
        ktg_b = cast(ktg)
        yield
        intra = mm(s_b, v)
        upd = mm(ktg_b, v)
        yield
        ct = ct_s[h]
        nm = n_s[h]
        inter = mm(q, ct)
        qn = mm(q, nm)[:, 0:1]
        yield
        num = w_inter * inter + intra
        den = w_inter * qn + jnp.sum(s, axis=1, keepdims=True)
        hh = num / jnp.maximum(jnp.abs(den), jnp.exp(-m_col))
        ct_s[h] = decay * ct + upd
        n_s[h] = decay * nm + jnp.sum(ktg, axis=1, keepdims=True)
        yield
        hs = slice(h * DV_M, (h + 1) * DV_M)
        hn = hh * lax.rsqrt(jnp.mean(hh * hh, axis=1, keepdims=True) + EPS) * nmh_ref[:, hs]
        ym[k, h] = gates_out[k, h] * hn

    def gate_chain(k):
        rs = slice(k * L, (k + 1) * L)
        yield
        for h in range(NH_M):
            gates_out[k, h] = (_sigmoid_bf16(pa_ref[rs, A_GM + h * DV_M:A_GM + (h + 1) * DV_M])
                               * _sigmoid_bf16(pa_ref[rs, A_OM + h * DV_M:A_OM + (h + 1) * DV_M])
                               ).astype(F32)
            if h == 1:
                yield
        yield
        gates_out[k, "a"] = _sigmoid_bf16(pa_ref[rs, A_GA:A_GA + D_MODEL]).astype(F32)

    def store_chain(k):
        rs = slice(k * L, (k + 1) * L)
        for _ in range(MLSTM_STAGES):
            yield
        for hd in range(N_HEADS):
            ym_h = ym[k, (hd * HD) // DV_M][:, (hd * HD) % DV_M:(hd * HD) % DV_M + HD]
            y_ref[rs, hd * HD:(hd + 1) * HD] = (ym_h + ya[k, hd]).astype(y_ref.dtype)

    def delayed(chain, rounds):
        for _ in range(rounds):
            yield
        yield from chain

    chains = []
    for k in range(sub):
        block = [mlstm_head(k, h) for h in range(NH_M)] + [gate_chain(k)]
        block += [_swa_chain(c, k, kv, sink_ref.at[l], qa_ref, misc_ref, miscp_ref, gates_out, mk_ref, mv_ref, ya,
                             has_meta) for kv in range(N_KV)]
        block.append(store_chain(k))
        chains += [delayed(chain, k * BLOCK_LAG) for chain in block]
    _run_interleaved(chains)

    @pl.when(c == nc - 1)
    def _():
        ctf_ref[0] = ct_s[...]
        nf_ref[0] = n_s[...]
        mf_ref[0] = m_s[...]


def _mix(l, sinks, qa, misc, pa, kt, gt, bias8, nmh, ct0, n0, m0, misc_small, nb, nblk, sub, row0, valid_len,
         has_meta):
    nc = nblk // sub
    rows = sub * CHUNK
    r0 = row0 // rows
    rb = lambda b, c: r0 + b * nc + c
    rp = lambda b, c: (r0 + b * nc + c) * sub - jnp.minimum(c, 1)
    kw = N_KV * HD
    first4 = lambda b, c: (0, 0, 0, 0)
    st4 = lambda b, c: (b, 0, 0, 0)
    return pl.pallas_call(
        functools.partial(_mix_kernel, l=l, nc=nc, sub=sub, valid_len=valid_len, has_meta=has_meta),
        grid=(nb, nc),
        in_specs=[pl.BlockSpec(memory_space=pltpu.SMEM),
                  pl.BlockSpec((rows, M_TOT), lambda b, c: (rb(b, c), 0)),
                  pl.BlockSpec((CHUNK, 2 * kw), lambda b, c: (rp(b, c), 0)),
                  pl.BlockSpec((rows, A_TOT), lambda b, c: (rb(b, c), 0)),
                  pl.BlockSpec((rows, D_MODEL), lambda b, c: (rb(b, c), 0)),
                  pl.BlockSpec((NH_M * DQK_M, rows), lambda b, c: (0, rb(b, c))),
                  pl.BlockSpec((2 * NH_M, rows), lambda b, c: (0, rb(b, c))),
                  pl.BlockSpec((None, 2 * NH_M, 128), lambda b, c: (l, 0, 0)),
                  pl.BlockSpec((None, 1, D_MODEL), lambda b, c: (l, 0, 0)),
                  pl.BlockSpec((None, NH_M, DQK_M, DV_M), first4),
                  pl.BlockSpec((None, NH_M, DQK_M, 128), first4),
                  pl.BlockSpec((None, NH_M, 8, 128), first4),
                  pl.BlockSpec((N_META, kw), lambda b, c: (META_ROW0 // N_META, M_KA // kw)),
                  pl.BlockSpec((N_META, kw), lambda b, c: (META_ROW0 // N_META, M_VA // kw))],
        out_specs=[pl.BlockSpec((rows, D_MODEL), lambda b, c: (b * nc + c, 0)),
                   pl.BlockSpec((1, NH_M, DQK_M, DV_M), st4),
                   pl.BlockSpec((1, NH_M, DQK_M, 128), st4),
                   pl.BlockSpec((1, NH_M, 8, 128), st4)],
        out_shape=[jax.ShapeDtypeStruct((nb * nc * rows, D_MODEL), BF16),
                   jax.ShapeDtypeStruct((nb, NH_M, DQK_M, DV_M), F32),
                   jax.ShapeDtypeStruct((nb, NH_M, DQK_M, 128), F32),
                   jax.ShapeDtypeStruct((nb, NH_M, 8, 128), F32)],
        scratch_shapes=[pltpu.VMEM((NH_M, DQK_M, DV_M), F32),
                        pltpu.VMEM((NH_M, DQK_M, 128), F32),
                        pltpu.VMEM((NH_M, 8, 128), F32)],
        compiler_params=_cparams(2),
        name="mix",
    )(sinks, misc, misc, pa, qa, kt, gt, bias8, nmh, ct0, n0, m0, misc_small, misc_small)


def _swa_chain(c, k, kv, sink_ref, q_ref, misc_ref, miscp_ref, gates_out, mk_ref, mv_ref, ya, has_meta):
    L = WINDOW
    R = GQ * L
    rs = slice(k * L, (k + 1) * L)
    row = lax.broadcasted_iota(jnp.int32, (R, L), 0)
    col = lax.broadcasted_iota(jnp.int32, (R, L), 1)
    tril = col <= (row & (L - 1))
    head_of_row = lax.shift_right_logical(lax.broadcasted_iota(jnp.int32, (R, 1), 0), L.bit_length() - 1)
    mm = _mm
    cast = lambda a: a.astype(BF16)
    with_ones = lambda v: jnp.concatenate([v, jnp.ones((v.shape[0], HD), v.dtype)], axis=1)
    if k == 0:
        prev = lambda cols: cast(miscp_ref[:, cols])
        prev_ok = jnp.full((R, L), c, jnp.int32) > 0
    else:
        prev = lambda cols: cast(misc_ref[(k - 1) * L:k * L, cols])
        prev_ok = None
    ka = slice(M_KA + kv * HD, M_KA + (kv + 1) * HD)
    va = slice(M_VA + kv * HD, M_VA + (kv + 1) * HD)
    ks = slice(kv * HD, (kv + 1) * HD)
    heads = [kv * GQ + g for g in range(GQ)]
    q = jnp.concatenate([cast(q_ref[rs, hd * HD:(hd + 1) * HD]) for hd in heads], axis=0)
    sink = jnp.zeros((R, 1), F32)
    for g, hd in enumerate(heads):
        sink = jnp.where(head_of_row == g, sink_ref[hd], sink)
    yield
    yield
    s_cur = mm(q, cast(misc_ref[rs, ka]), nt=True)
    s_prev = mm(q, prev(ka), nt=True)
    if has_meta:
        s_m = mm(q, cast(mk_ref[:, ks]), nt=True)
    yield
    if prev_ok is not None:
        s_prev = jnp.where(prev_ok, s_prev, -jnp.inf)
    band = jnp.where(tril, s_cur, s_prev)
    mx = jnp.maximum(jnp.max(band, axis=1, keepdims=True), sink)
    if has_meta:
        mx = jnp.maximum(mx, jnp.max(s_m, axis=1, keepdims=True))
        p_m = cast(jnp.exp(s_m - mx))
    yield
    p = jnp.exp(band - mx)
    p_cur = cast(jnp.where(tril, p, 0.0))
    p_prev = cast(jnp.where(tril, 0.0, p))
    yield
    oe = (mm(p_cur, with_ones(cast(misc_ref[rs, va])))
          + mm(p_prev, with_ones(prev(va))))
    if has_meta:
        oe = oe + mm(p_m, with_ones(cast(mv_ref[:, ks])))
    yield
    denom = oe[:, HD:HD + 1] + jnp.exp(sink - mx)
    o = oe[:, 0:HD] / denom
    yield
    for g, hd in enumerate(heads):
        ya[k, hd] = o[g * L:(g + 1) * L] * gates_out[k, "a"][:, hd * HD:(hd + 1) * HD]


def _tail_kernel(y_ref, x_ref, gn_ref, gnext_ref, wo_ref, wu_ref, wd_ref, o_ref, *rest, tf, last_layer):
    x1 = x_ref[...] + _dot(y_ref[...], wo_ref[...])
    hf = _rms(x1, gn_ref[...]).astype(BF16)
    acc = x1
    for j in range(D_FF // tf):
        a = jnp.square(jnp.maximum(_dot(hf, wu_ref[:, j * tf:(j + 1) * tf]), 0.0)).astype(BF16)
        acc = acc + _dot(a, wd_ref[j * tf:(j + 1) * tf, :])
    if last_layer:
        o_ref[...] = _rms(acc, gnext_ref[...])
    else:
        o_ref[...] = acc
        rest[0][...] = _rms(acc, gnext_ref[...]).astype(BF16)


def _tail(l, y, x, g_mlp, g_next, weights, tm, tf):
    rows = x.shape[0]
    last_layer = l == DEPTH - 1
    row_spec = pl.BlockSpec((tm, D_MODEL), lambda i: (i, 0))
    resident = lambda r, c: pl.BlockSpec((None, r, c), lambda i: (l, 0, 0), pipeline_mode=pl.Buffered(1))
    out_specs, out_shape = [row_spec], [jax.ShapeDtypeStruct((rows, D_MODEL), F32)]
    if not last_layer:
        out_specs.append(row_spec)
        out_shape.append(jax.ShapeDtypeStruct((rows, D_MODEL), BF16))
    return pl.pallas_call(
        functools.partial(_tail_kernel, tf=tf, last_layer=last_layer),
        grid=(rows // tm,),
        in_specs=[row_spec, row_spec,
                  pl.BlockSpec((None, 1, D_MODEL), lambda i: (l, 0, 0)),
                  pl.BlockSpec((None, 1, D_MODEL), lambda i: (l + 1, 0, 0)),
                  resident(D_MODEL, D_MODEL), resident(D_MODEL, D_FF), resident(D_FF, D_MODEL)],
        out_specs=out_specs,
        out_shape=out_shape,
        compiler_params=_cparams(1),
        name="tail",
    )(y, x, g_mlp, g_next, *weights)


def _mlstm_step_kernel(q_ref, k_ref, v_ref, o_ref, gm_ref, ig_ref, fg_ref, m_ref, n_ref, c_ref,
                       kall_ref, vall_ref, bi_ref, bf_ref, nmh_ref, *rest):
    y_ref, co_ref, no_ref, mo_ref = rest[-4:]
    i = pl.program_id(0)
    nb = DEC_BLK
    row8 = lax.broadcasted_iota(jnp.int32, (nb, DV_M), 0)
    rowb = lax.broadcasted_iota(jnp.int32, (kall_ref.shape[0], DQK_M), 0)
    for h in range(NH_M):
        ds = slice(h * DQK_M, (h + 1) * DQK_M)
        vs = slice(h * DV_M, (h + 1) * DV_M)
        q, k, v, nrm = q_ref[:, ds], k_ref[:, ds], v_ref[:, vs], n_ref[:, ds]
        ig = ig_ref[:, ds] + bi_ref[:, ds]
        lf = _log_sigmoid(fg_ref[:, ds] + bf_ref[:, ds])
        m_prev = m_ref[:, ds]
        m_t = jnp.maximum(lf + m_prev, ig)
        w_inter = jnp.exp(lf + m_prev - m_t)
        w_in = jnp.exp(ig - m_t)
        s = jnp.sum(q * k, axis=1, keepdims=True) * w_in
        qb = q.astype(BF16)
        kall = kall_ref[:, ds].astype(BF16)
        vt = vall_ref[:, vs].T.astype(BF16)
        cq = jnp.zeros((nb, DV_M), F32)
        for bb in range(nb):
            c_old = c_ref[bb, h]
            cq = jnp.where(row8 == bb, _dot_nt(qb, c_old.astype(BF16)), cq)
            outer = _dot(vt, jnp.where(rowb == i * nb + bb, kall, jnp.zeros_like(kall)))
            co_ref[bb, h] = w_inter[bb:bb + 1, :] * c_old + w_in[bb:bb + 1, :] * outer
        num = w_inter[:, 0:1] * cq + s[:, 0:1] * v
        den = w_inter * jnp.sum(nrm * q, axis=1, keepdims=True) + s
        hh = num / jnp.maximum(jnp.abs(den), jnp.exp(-m_t))[:, 0:1]
        no_ref[:, ds] = w_inter * nrm + w_in * k
        mo_ref[:, ds] = m_t
        hn = hh * lax.rsqrt(jnp.mean(hh * hh, axis=1, keepdims=True) + EPS) * nmh_ref[:, vs]
        y_ref[:, vs] = jax.nn.sigmoid(gm_ref[:, vs]) * (jax.nn.sigmoid(o_ref[:, vs]) * hn)


def _mlstm_step(l, misc, km, pa, grep, m_all, n_all, c_all, c_out_prev, bi, bf, nmh):
    nbatch = c_all.shape[1]
    qd = NH_M * DQK_M
    rowq = lambda j: pl.BlockSpec((DEC_BLK, qd), lambda i: (i, j))
    rowv = lambda j: pl.BlockSpec((DEC_BLK, D_MODEL), lambda i: (i, j))
    lrow = pl.BlockSpec((None, DEC_BLK, qd), lambda i: (l, i, 0))
    lconst = lambda width: pl.BlockSpec((None, 1, width), lambda i: (l, 0, 0))
    cspec = pl.BlockSpec((None, DEC_BLK, NH_M, DV_M, DQK_M), lambda i: (l, i, 0, 0, 0))
    in_specs = [rowq(M_QM // qd), rowq(0), rowv(A_VM // D_MODEL), rowv(A_OM // D_MODEL), rowv(A_GM // D_MODEL),
                rowq(0), rowq(1), lrow, lrow, cspec,
                pl.BlockSpec((nbatch, qd), lambda i: (0, 0)),
                pl.BlockSpec((nbatch, D_MODEL), lambda i: (0, A_VM // D_MODEL)),
                lconst(qd), lconst(qd), lconst(D_MODEL)]
    args = [misc, km, pa, pa, pa, grep, grep, m_all, n_all, c_all, km, pa, bi, bf, nmh]
    aliases = {}
    if c_out_prev is not None:
        aliases = {len(args): 1}
        in_specs.append(pl.BlockSpec(memory_space=pl.ANY))
        args.append(c_out_prev)
    return pl.pallas_call(
        _mlstm_step_kernel,
        grid=(nbatch // DEC_BLK,),
        in_specs=in_specs,
        out_specs=[rowv(0), cspec, rowq(0), rowq(0)],
        out_shape=[jax.ShapeDtypeStruct((nbatch, D_MODEL), F32),
                   jax.ShapeDtypeStruct(c_all.shape, F32),
                   jax.ShapeDtypeStruct((nbatch, qd), F32),
                   jax.ShapeDtypeStruct((nbatch, qd), F32)],
        input_output_aliases=aliases,
        compiler_params=_cparams(1),
        name="mlstm_step",
    )(*args)


def _swa_step_kernel(q_ref, ga_ref, kn_ref, vn_ref, mk_ref, mv_ref, wk_ref, wv_ref, sink_ref, *rest):
    y_ref, ok_ref, ov_ref = rest[-3:]
    W = WINDOW
    hrow = lax.broadcasted_iota(jnp.int32, (N_HEADS, HD), 0)
    first = hrow < GQ
    wcol = lax.broadcasted_iota(jnp.int32, (N_HEADS, W), 1)
    wrow = lax.broadcasted_iota(jnp.int32, (N_KV * W, HD), 0)
    sink = sink_ref[...][:, 0:1]

    def per_kv(fn):
        a0, a1 = fn(0), fn(1)
        return jnp.where(first[:, 0:a0.shape[1]], a0, a1)

    def batch_element(bb):
        q = q_ref[bb]
        qb = q.astype(BF16)
        rows_of = lambda ref, kv, n: ref[bb, pl.ds(kv, n, stride=N_KV), :].astype(BF16)
        kn, vn = kn_ref[bb:bb + 1, :], vn_ref[bb:bb + 1, :]
        kvs = lambda a, kv: a[:, kv * HD:(kv + 1) * HD]
        s_w = jnp.where(wcol >= 1, per_kv(lambda kv: _dot_nt(qb, rows_of(wk_ref, kv, W))), -jnp.inf)
        s_m = per_kv(lambda kv: _dot_nt(qb, rows_of(mk_ref, kv, N_META)))
        kn8 = per_kv(lambda kv: jnp.broadcast_to(kvs(kn, kv), (N_HEADS, HD)))
        vn8 = per_kv(lambda kv: jnp.broadcast_to(kvs(vn, kv), (N_HEADS, HD)))
        s_n = jnp.sum(q * kn8, axis=1, keepdims=True)
        yield
        mx = jnp.maximum(jnp.maximum(jnp.max(s_w, axis=1, keepdims=True), jnp.max(s_m, axis=1, keepdims=True)),
                         jnp.maximum(s_n, sink))
        p_w, p_m, p_n = jnp.exp(s_w - mx), jnp.exp(s_m - mx), jnp.exp(s_n - mx)
        denom = (jnp.sum(p_w, axis=1, keepdims=True) + jnp.sum(p_m, axis=1, keepdims=True) + p_n
                 + jnp.exp(sink - mx))
        yield
        o = (per_kv(lambda kv: _dot(p_w.astype(BF16), rows_of(wv_ref, kv, W)))
             + per_kv(lambda kv: _dot(p_m.astype(BF16), rows_of(mv_ref, kv, N_META)))
             + p_n * vn8)
        yield
        y_ref[bb] = (o / denom) * jax.nn.sigmoid(ga_ref[bb])
        for src_ref, new, dst_ref in ((wk_ref, kn, ok_ref), (wv_ref, vn, ov_ref)):
            shifted = pltpu.roll(src_ref[bb], N_KV * (W - 1), 0)
            for kv in range(N_KV):
                shifted = jnp.where(wrow == N_KV * (W - 1) + kv, kvs(new, kv), shifted)
            dst_ref[bb] = shifted

    _run_interleaved([batch_element(bb) for bb in range(DEC_BLK)])


def _swa_step(l, q, ga, misc, mk_all, mv_all, wk_all, wv_all, ok_prev, ov_prev, sink8):
    nbatch = q.shape[0]
    kw = N_KV * HD
    hspec = pl.BlockSpec((DEC_BLK, N_HEADS, HD), lambda i: (i, 0, 0))
    mspec = pl.BlockSpec((None, DEC_BLK, N_KV * N_META, HD), lambda i: (l, i, 0, 0))
    wspec = pl.BlockSpec((None, DEC_BLK, N_KV * WINDOW, HD), lambda i: (l, i, 0, 0))
    in_specs = [hspec, hspec,
                pl.BlockSpec((DEC_BLK, kw), lambda i: (i, M_KA // kw)),
                pl.BlockSpec((DEC_BLK, kw), lambda i: (i, M_VA // kw)),
                mspec, mspec, wspec, wspec,
                pl.BlockSpec((None, N_HEADS, HD), lambda i: (l, 0, 0))]
    args = [q, ga, misc, misc, mk_all, mv_all, wk_all, wv_all, sink8]
    aliases = {}
    if ok_prev is not None:
        aliases = {len(args): 1, len(args) + 1: 2}
        in_specs += [pl.BlockSpec(memory_space=pl.ANY)] * 2
        args += [ok_prev, ov_prev]
    return pl.pallas_call(
        _swa_step_kernel,
        grid=(nbatch // DEC_BLK,),
        in_specs=in_specs,
        out_specs=[hspec, wspec, wspec],
        out_shape=[jax.ShapeDtypeStruct((nbatch, N_HEADS, HD), F32),
                   jax.ShapeDtypeStruct(wk_all.shape, F32),
                   jax.ShapeDtypeStruct(wv_all.shape, F32)],
        input_output_aliases=aliases,
        compiler_params=_cparams(1),
        name="swa_step",
    )(*args)


def _rope_tables(pos):
    half = ROT_DIM // 2
    inv = jnp.float32(ROPE_THETA) ** (-jnp.arange(half, dtype=F32) * 2.0 / ROT_DIM)
    ang = pos.astype(F32)[:, None] * inv[None, :]
    cos, sin = jnp.cos(ang), jnp.sin(ang)
    t = pos.shape[0]
    cs = jnp.concatenate([cos, cos, jnp.ones((t, HD - ROT_DIM), F32)], axis=1)
    s1 = jnp.concatenate([jnp.zeros((t, half), F32), sin, jnp.zeros((t, HD - ROT_DIM), F32)], axis=1)
    s2 = jnp.concatenate([-sin, jnp.zeros((t, HD - half), F32)], axis=1)
    return cs, s1, s2


PACK_ROWS = 128


def _pack_kernel(wt_ref, wa_ref, wb_ref):
    edges = np.concatenate([[0], np.cumsum(IN_WIDTHS)])
    q_m, k_m, v_m, ig, fg, o_m, q_a, k_a, v_a, g_m, g_a = [(int(edges[i]), int(edges[i + 1])) for i in range(11)]

    def put(dst_ref, dst0, seg):
        lo, hi = seg
        dst_ref[:, dst0:dst0 + hi - lo] = wt_ref[lo:hi, :].T.astype(BF16)

    for dst0, seg in ((A_VM, v_m), (A_OM, o_m), (A_GM, g_m), (A_GA, g_a)):
        put(wa_ref, dst0, seg)
    for dst0, seg in ((B_QA, q_a), (B_KA, k_a), (B_VA, v_a), (B_QM, q_m), (B_KM, k_m)):
        put(wb_ref, dst0, seg)
    tile = wt_ref[ig[0]:ig[0] + 128, :].T
    lane = lax.broadcasted_iota(jnp.int32, tile.shape, 1)
    wb_ref[:, B_G:B_TOT] = jnp.where(lane < 2 * NH_M, tile, 0.0).astype(BF16)


def _pack_in_weights(w_in):
    starts = np.concatenate([[0], np.cumsum(IN_WIDTHS)[:-1]])
    assert IN_WIDTHS[3] == IN_WIDTHS[4] == NH_M and all(s % 8 == 0 for i, s in enumerate(starts) if i != 4)
    depth, rows, cols = w_in.shape
    return pl.pallas_call(
        _pack_kernel,
        grid=(depth, rows // PACK_ROWS),
        in_specs=[pl.BlockSpec((None, cols, PACK_ROWS), lambda l, i: (l, 0, i))],
        out_specs=[pl.BlockSpec((None, PACK_ROWS, A_TOT), lambda l, i: (l, i, 0)),
                   pl.BlockSpec((None, PACK_ROWS, B_TOT), lambda l, i: (l, i, 0))],
        out_shape=[jax.ShapeDtypeStruct((depth, rows, A_TOT), BF16),
                   jax.ShapeDtypeStruct((depth, rows, B_TOT), BF16)],
        compiler_params=_cparams(2),
        name="pack_w_in",
    )(jnp.swapaxes(w_in, 1, 2))


def _rep(a, width=128):
    return jnp.repeat(a, width, axis=1)


def kernel(x_prompt, x_sample, state_C, state_n, state_m, cache_meta_k, cache_meta_v, cache_win_k, cache_win_v,
           meta_tokens, norm_mix, w_in, b_igate, b_fgate, norm_mh, attn_sinks, w_out, norm_mlp, w_up, w_down,
           norm_final):
    B, S, _ = x_prompt.shape
    DB = x_sample.shape[0]
    W = cache_win_k.shape[2]
    assert (S % CHUNK, W, x_sample.shape[1], DB % DEC_BLK) == (0, WINDOW, 1, 0)
    assert (state_C.dtype, cache_win_k.dtype, cache_win_v.dtype) == (F32, F32, F32)
    nchunk = S // CHUNK
    kw = N_KV * HD
    big_tm = 1024

    w_a, w_b = _pack_in_weights(w_in)
    tail_w = (w_out.astype(BF16), w_up.astype(BF16), w_down.astype(BF16))
    g_mix = norm_mix.astype(F32).reshape(DEPTH, 1, D_MODEL)
    g_mlp = norm_mlp.astype(F32).reshape(DEPTH, 1, D_MODEL)
    g_next = jnp.concatenate([g_mix, norm_final.astype(F32).reshape(1, 1, D_MODEL)], axis=0)
    nmh = norm_mh.astype(F32).reshape(DEPTH, 1, D_MODEL)
    sinks = attn_sinks.astype(F32)
    sink8 = jnp.broadcast_to(sinks[:, :, None], (DEPTH, N_HEADS, HD))
    bias8 = jnp.broadcast_to(jnp.concatenate([b_igate, b_fgate], axis=1).astype(F32)[:, :, None],
                             (DEPTH, 2 * NH_M, 128))
    bi_rep = _rep(b_igate.astype(F32)).reshape(DEPTH, 1, NH_M * 128)
    bf_rep = _rep(b_fgate.astype(F32)).reshape(DEPTH, 1, NH_M * 128)
    m_rep = jnp.repeat(state_m.astype(F32), 128, axis=2)
    n_all = state_n.astype(F32).reshape(DEPTH, DB, NH_M * DQK_M)
    rows_view = lambda a: a.reshape(DEPTH, DB, -1, HD)
    mk_all, mv_all = rows_view(cache_meta_k), rows_view(cache_meta_v)
    wk_all, wv_all = rows_view(cache_win_k), rows_view(cache_win_v)

    tabs_p = _rope_tables(N_META + jnp.arange(S, dtype=jnp.int32))
    pos_small = jnp.concatenate([jnp.full((DB,), PAST_LEN, jnp.int32), jnp.arange(N_META, dtype=jnp.int32),
                                 jnp.zeros((SMALL_ROWS - DB - N_META,), jnp.int32)])
    tabs_s = _rope_tables(pos_small)

    xp = x_prompt.reshape(B * S, D_MODEL)
    xs = jnp.concatenate([x_sample.reshape(DB, D_MODEL), meta_tokens.astype(F32),
                          jnp.zeros((SMALL_ROWS - DB - N_META, D_MODEL), F32)], axis=0)
    zero_state = (jnp.zeros((1, NH_M, DQK_M, DV_M), F32), jnp.zeros((1, NH_M, DQK_M, 128), F32),
                  jnp.zeros((1, NH_M, 8, 128), F32))

    outs = {k: [] for k in ("ct", "n", "m", "meta", "win", "sn", "sm")}
    sC = swk = swv = None
    hs_ = _norm(0, xs, g_mix, SMALL_ROWS, BF16)
    hp = _norm(0, xp, g_mix, big_tm, BF16)
    for l in range(DEPTH):
        pa_s, qa_s, misc_s, kt_s, gt_s, km_s, grep_s = _inproj(l, hs_, w_a, w_b, tabs_s, SMALL_ROWS, small=True)
        y_meta, ct_m, n_m, m_m = _mix(l, sinks, qa_s, misc_s, pa_s, kt_s, gt_s, bias8, nmh, *zero_state, misc_s,
                                      nb=1, nblk=1, sub=1, row0=META_ROW0, valid_len=N_META, has_meta=False)
        ys_m, sC, sn, sm = _mlstm_step(l, misc_s, km_s, pa_s, grep_s, m_rep, n_all, state_C, sC, bi_rep, bf_rep, nmh)
        ys_a, swk, swv = _swa_step(
            l, qa_s[:DB].reshape(DB, N_HEADS, HD), pa_s[:DB, A_GA:A_GA + D_MODEL].reshape(DB, N_HEADS, HD),
            misc_s, mk_all, mv_all, wk_all, wv_all, swk, swv, sink8)
        y_s = jnp.concatenate([(ys_m + ys_a.reshape(DB, D_MODEL)).astype(BF16), y_meta], axis=0)
        res = _tail(l, y_s, xs, g_mlp, g_next, tail_w, tm=SMALL_ROWS, tf=TAIL_TF)
        xs, hs_ = res[0], res[-1]

        pa, qa, misc, kt, gt = _inproj(l, hp, w_a, w_b, tabs_p, INPROJ_TM, small=False)
        y, ct_f, n_f, m_f = _mix(l, sinks, qa, misc, pa, kt, gt, bias8, nmh, ct_m, n_m, m_m, misc_s,
                                 nb=B, nblk=nchunk, sub=MIX_SUB, row0=0, valid_len=CHUNK, has_meta=True)
        res = _tail(l, y, xp, g_mlp, g_next, tail_w, tm=TAIL_TM, tf=TAIL_TF)
        xp, hp = res[0], res[-1]

        for key, val in (("ct", ct_f), ("n", n_f), ("m", m_f), ("sn", sn), ("sm", sm),
                         ("meta", misc_s[META_ROW0:META_ROW0 + N_META, M_KA:M_VA + kw]),
                         ("win", misc.reshape(B, S, M_TOT)[:, S - WINDOW:, M_KA:M_VA + kw])):
            outs[key].append(val)

    y_prompt = xp.reshape(B, S, D_MODEL)
    y_sample = xs[:DB].reshape(DB, 1, D_MODEL)
    st = lambda k: jnp.stack(outs[k])
    p_c = jnp.swapaxes(st("ct"), 3, 4).astype(state_C.dtype)
    p_n = st("n")[..., 0].astype(state_n.dtype)
    p_m = st("m")[..., 0, 0].astype(state_m.dtype)
    meta = jnp.broadcast_to(st("meta").astype(F32)[:, None], (DEPTH, B, N_META, 2 * kw))
    win = st("win").astype(F32)
    kv_of = lambda a, lo: a[..., lo:lo + kw].reshape(a.shape[:-1] + (N_KV, HD))
    s_n = st("sn").reshape(DEPTH, DB, NH_M, DQK_M).astype(state_n.dtype)
    s_m = st("sm").reshape(DEPTH, DB, NH_M, 128)[..., 0].astype(state_m.dtype)
    return (y_prompt, y_sample, p_c, p_n, p_m, kv_of(meta, 0), kv_of(meta, kw), kv_of(win, 0), kv_of(win, kw),
            sC, s_n, s_m, swk.reshape(cache_win_k.shape), swv.reshape(cache_win_v.shape))
```

```python
import functools

import jax
import jax.numpy as jnp
import numpy as np
from jax import lax
from jax.experimental import pallas as pl
from jax.experimental.pallas import tpu as pltpu

F32 = jnp.float32
BF16 = jnp.bfloat16

D_MODEL = 1024
DEPTH = 4
PAST_LEN = 8192
N_META = 16
NH_M = 4
DV_M = 256
DQK_M = 128
CHUNK = 128
N_HEADS = 8
N_KV = 2
GQ = N_HEADS // N_KV
HD = 128
WINDOW = 128
ROT_DIM = 32
ROPE_THETA = 500000.0
D_FF = 4 * D_MODEL
EPS = 1e-6
IN_WIDTHS = (NH_M * DQK_M, NH_M * DQK_M, NH_M * DV_M, NH_M, NH_M, NH_M * DV_M,
             N_HEADS * HD, N_KV * HD, N_KV * HD, D_MODEL, D_MODEL)

A_VM, A_OM, A_GM, A_GA, A_TOT = 0, 1024, 2048, 3072, 4096
A_TN = 1024
INPROJ_TM = 512
NORM_TM = 1024
B_QA, B_KA, B_VA, B_QM, B_KM, B_G, B_TOT = 0, 1024, 1280, 1536, 2048, 2560, 2688
M_KA, M_VA, M_QM, M_TOT = 0, 256, 512, 1024
TAIL_TM = 512
TAIL_TF = 1024
MIX_SUB = 8
BLOCK_LAG = 2
MLSTM_STAGES = 8
SMALL_ROWS = 256
META_ROW0 = 128
DEC_BLK = 16
VMEM_LIMIT = 56 * 1024 * 1024


def _cparams(n_axes):
    return pltpu.CompilerParams(dimension_semantics=("arbitrary",) * n_axes, vmem_limit_bytes=VMEM_LIMIT)


def _rms(xf, gain):
    ms = jnp.mean(xf * xf, axis=-1, keepdims=True)
    return xf * lax.rsqrt(ms + EPS) * gain


def _dot(a, b):
    return jnp.dot(a, b, preferred_element_type=F32)


def _dot_nt(a, b):
    return lax.dot_general(a, b, (((1,), (1,)), ((), ())), preferred_element_type=F32)


def _mm(a, b, nt=False):
    return (_dot_nt if nt else _dot)(a.astype(BF16), b.astype(BF16))


def _log_sigmoid(x):
    return jnp.minimum(x, 0.0) - jnp.log1p(jnp.exp(-jnp.abs(x)))


def _run_interleaved(chains):
    chains = list(chains)
    while chains:
        alive = []
        for chain in chains:
            try:
                next(chain)
                alive.append(chain)
            except StopIteration:
                pass
        chains = alive


def _sigmoid_bf16(x):
    half = jnp.asarray(0.5, x.dtype)
    return jnp.tanh(x * half) * half + half


def _norm_kernel(x_ref, gain_ref, o_ref):
    o_ref[...] = _rms(x_ref[...], gain_ref[...]).astype(o_ref.dtype)


def _norm(l, x, gains, tm, out_dtype):
    rows = x.shape[0]
    return pl.pallas_call(
        _norm_kernel,
        grid=(rows // tm,),
        in_specs=[pl.BlockSpec((tm, D_MODEL), lambda i: (i, 0)),
                  pl.BlockSpec((None, 1, D_MODEL), lambda i: (l, 0, 0))],
        out_specs=pl.BlockSpec((tm, D_MODEL), lambda i: (i, 0)),
        out_shape=jax.ShapeDtypeStruct((rows, D_MODEL), out_dtype),
        compiler_params=_cparams(1),
        name="norm",
    )(x, gains)


def _inproj_kernel(h_ref, wa_ref, w_ref, cs_ref, s1_ref, s2_ref, pa_ref, qa_ref, misc_ref, kt_ref, gt_ref,
                   *row_form):
    small = bool(row_form)
    mm = lambda lo, hi: _dot(h_ref[...], w_ref[:, lo:hi])

    for c in range(A_TOT // A_TN):
        cs = slice(c * A_TN, (c + 1) * A_TN)
        pa_ref[:, cs] = _dot(h_ref[...], wa_ref[:, cs]).astype(pa_ref.dtype)

    def rope(seg):
        return (seg * cs_ref[...] + pltpu.roll(seg, 16, 1) * s1_ref[...]
                + pltpu.roll(seg, HD - 16, 1) * s2_ref[...])

    acc = mm(B_QA, B_KA)
    for c in range(N_HEADS):
        cs = slice(c * HD, (c + 1) * HD)
        qa_ref[:, cs] = (rope(acc[:, cs]) * (HD ** -0.5)).astype(qa_ref.dtype)

    acc = mm(B_KA, B_KM)
    for c in range(N_KV):
        cs = slice(M_KA + c * HD, M_KA + (c + 1) * HD)
        misc_ref[:, cs] = rope(acc[:, cs]).astype(misc_ref.dtype)
    misc_ref[:, M_VA:M_QM] = acc[:, M_VA:M_QM].astype(misc_ref.dtype)
    misc_ref[:, M_QM:M_TOT] = (acc[:, M_QM:M_TOT] * (DQK_M ** -0.5)).astype(misc_ref.dtype)

    acc = mm(B_KM, B_TOT)
    kt_ref[...] = acc[:, 0:NH_M * DQK_M].T.astype(kt_ref.dtype)
    gt_ref[...] = acc[:, NH_M * DQK_M:B_TOT - B_KM].T[0:2 * NH_M]

    if small:
        km_ref, grep_ref = row_form
        km_ref[...] = acc[:, 0:NH_M * DQK_M]
        for c in range(2 * NH_M):
            col = NH_M * DQK_M + c
            grep_ref[:, c * 128:(c + 1) * 128] = jnp.broadcast_to(acc[:, col:col + 1], (acc.shape[0], 128))


G_REP = 2 * NH_M * 128


def _inproj(l, h, w_a, w_b, tabs, tm, small):
    rows = h.shape[0]
    tab_blocks = tabs[0].shape[0] // tm
    tab_spec = pl.BlockSpec((tm, HD), lambda i: (i % tab_blocks, 0))
    dt = F32 if small else BF16
    row_spec = lambda width: pl.BlockSpec((tm, width), lambda i: (i, 0))
    resident = lambda cols: pl.BlockSpec((None, D_MODEL, cols), lambda i: (l, 0, 0), pipeline_mode=pl.Buffered(1))
    out_specs = [row_spec(A_TOT), row_spec(D_MODEL), row_spec(M_TOT),
                 pl.BlockSpec((NH_M * DQK_M, tm), lambda i: (0, i)),
                 pl.BlockSpec((2 * NH_M, tm), lambda i: (0, i))]
    out_shape = [jax.ShapeDtypeStruct((rows, A_TOT), dt),
                 jax.ShapeDtypeStruct((rows, D_MODEL), dt),
                 jax.ShapeDtypeStruct((rows, M_TOT), dt),
                 jax.ShapeDtypeStruct((NH_M * DQK_M, rows), dt),
                 jax.ShapeDtypeStruct((2 * NH_M, rows), F32)]
    if small:
        out_specs += [row_spec(NH_M * DQK_M), row_spec(G_REP)]
        out_shape += [jax.ShapeDtypeStruct((rows, NH_M * DQK_M), F32), jax.ShapeDtypeStruct((rows, G_REP), F32)]
    return pl.pallas_call(
        _inproj_kernel,
        grid=(rows // tm,),
        in_specs=[pl.BlockSpec((tm, D_MODEL), lambda i: (i, 0)), resident(A_TOT), resident(B_TOT),
                  tab_spec, tab_spec, tab_spec],
        out_specs=out_specs,
        out_shape=out_shape,
        compiler_params=_cparams(1),
        name="inproj",
    )(h, w_a, w_b, *tabs)


def _mix_kernel(sink_ref, misc_ref, miscp_ref, pa_ref, qa_ref, kt_ref, g_ref, bias_ref, nmh_ref,
                ct0_ref, n0_ref, m0_ref, mk_ref, mv_ref,
                y_ref, ctf_ref, nf_ref, mf_ref, ct_s, n_s, m_s, *, l, nc, sub, valid_len, has_meta):
    c = pl.program_id(1)
    L = CHUNK
    mm = _mm
    cast = lambda a: a.astype(BF16)

    @pl.when(c == 0)
    def _():
        ct_s[...] = ct0_ref[...]
        n_s[...] = n0_ref[...]
        m_s[...] = m0_ref[...]

    row = lax.broadcasted_iota(jnp.int32, (L, L), 0)
    col = lax.broadcasted_iota(jnp.int32, (L, L), 1)
    tril = col <= row
    eye = col == row

    gates = g_ref[...] + bias_ref[:, 0:1]
    ig_sub = gates[0:NH_M]
    lf_sub = _log_sigmoid(gates[NH_M:2 * NH_M])
    if valid_len < L:
        ok = lax.broadcasted_iota(jnp.int32, (NH_M, sub * L), 1) < valid_len
        ig_sub = jnp.where(ok, ig_sub, -jnp.inf)
        lf_sub = jnp.where(ok, lf_sub, 0.0)

    ym, ya, gates_out = {}, {}, {}

    def mlstm_head(k, h):
        rs = slice(k * L, (k + 1) * L)
        ig = ig_sub[h:h + 1, rs]
        lf = lf_sub[h:h + 1, rs]
        b_col = jnp.sum(jnp.where(tril, lf, 0.0), axis=1, keepdims=True)
        b_row = jnp.sum(jnp.where(eye, b_col, 0.0), axis=0, keepdims=True)
        yield
        m_prev = m_s[h, 0:1, 0:1]
        a_col = b_col + m_prev
        dmat = jnp.where(tril, b_col - b_row + ig, -jnp.inf)
        m_col = jnp.maximum(a_col, jnp.max(dmat, axis=1, keepdims=True))
        w_inter = jnp.exp(a_col - m_col)
        wmat = jnp.exp(dmat - m_col)
        m_new = m_col[L - 1:L]
        m_s[h] = jnp.broadcast_to(m_new, (8, 128))
        q = cast(misc_ref[rs, M_QM + h * DQK_M:M_QM + (h + 1) * DQK_M])
        kt = kt_ref[h * DQK_M:(h + 1) * DQK_M, rs]
        v = cast(pa_ref[rs, A_VM + h * DV_M:A_VM + (h + 1) * DV_M])
        yield
        qk = mm(q, kt)
        yield
        s = qk * wmat
        s_b = cast(s)
        yield
        decay = w_inter[L - 1:L]
        g_row = jnp.exp(b_col[L - 1:L] - b_row + ig - m_new)
        ktg = kt.astype(F32) * g_row
        ktg_b = cast(ktg)
        yield
        intra = mm(s_b, v)
        upd = mm(ktg_b, v)
        yield
        ct = ct_s[h]
        nm = n_s[h]
        inter = mm(q, ct)
        qn = mm(q, nm)[:, 0:1]
        yield
        num = w_inter * inter + intra
        den = w_inter * qn + jnp.sum(s, axis=1, keepdims=True)
        hh = num / jnp.maximum(jnp.abs(den), jnp.exp(-m_col))
        ct_s[h] = decay * ct + upd
        n_s[h] = decay * nm + jnp.sum(ktg, axis=1, keepdims=True)
        yield
        hs = slice(h * DV_M, (h + 1) * DV_M)
        hn = hh * lax.rsqrt(jnp.mean(hh * hh, axis=1, keepdims=True) + EPS) * nmh_ref[:, hs]
        ym[k, h] = gates_out[k, h] * hn

    def gate_chain(k):
        rs = slice(k * L, (k + 1) * L)
        yield
        for h in range(NH_M):
            gates_out[k, h] = (_sigmoid_bf16(pa_ref[rs, A_GM + h * DV_M:A_GM + (h + 1) * DV_M])
                               * _sigmoid_bf16(pa_ref[rs, A_OM + h * DV_M:A_OM + (h + 1) * DV_M])
                               ).astype(F32)
            if h == 1:
                yield
        yield
        gates_out[k, "a"] = _sigmoid_bf16(pa_ref[rs, A_GA:A_GA + D_MODEL]).astype(F32)

    def store_chain(k):
        rs = slice(k * L, (k + 1) * L)
        for _ in range(MLSTM_STAGES):
            yield
        for hd in range(N_HEADS):
            ym_h = ym[k, (hd * HD) // DV_M][:, (hd * HD) % DV_M:(hd * HD) % DV_M + HD]
            y_ref[rs, hd * HD:(hd + 1) * HD] = (ym_h + ya[k, hd]).astype(y_ref.dtype)

    def delayed(chain, rounds):
        for _ in range(rounds):
            yield
        yield from chain

    chains = []
    for k in range(sub):
        block = [mlstm_head(k, h) for h in range(NH_M)] + [gate_chain(k)]
        block += [_swa_chain(c, k, kv, sink_ref.at[l], qa_ref, misc_ref, miscp_ref, gates_out, mk_ref, mv_ref, ya,
                             has_meta) for kv in range(N_KV)]
        block.append(store_chain(k))
        chains += [delayed(chain, k * BLOCK_LAG) for chain in block]
    _run_interleaved(chains)

    @pl.when(c == nc - 1)
    def _():
        ctf_ref[0] = ct_s[...]
        nf_ref[0] = n_s[...]
        mf_ref[0] = m_s[...]


def _mix(l, sinks, qa, misc, pa, kt, gt, bias8, nmh, ct0, n0, m0, misc_small, nb, nblk, sub, row0, valid_len,
         has_meta):
    nc = nblk // sub
    rows = sub * CHUNK
    r0 = row0 // rows
    rb = lambda b, c: r0 + b * nc + c
    rp = lambda b, c: (r0 + b * nc + c) * sub - jnp.minimum(c, 1)
    kw = N_KV * HD
    first4 = lambda b, c: (0, 0, 0, 0)
    st4 = lambda b, c: (b, 0, 0, 0)
    return pl.pallas_call(
        functools.partial(_mix_kernel, l=l, nc=nc, sub=sub, valid_len=valid_len, has_meta=has_meta),
        grid=(nb, nc),
        in_specs=[pl.BlockSpec(memory_space=pltpu.SMEM),
                  pl.BlockSpec((rows, M_TOT), lambda b, c: (rb(b, c), 0)),
                  pl.BlockSpec((CHUNK, 2 * kw), lambda b, c: (rp(b, c), 0)),
                  pl.BlockSpec((rows, A_TOT), lambda b, c: (rb(b, c), 0)),
                  pl.BlockSpec((rows, D_MODEL), lambda b, c: (rb(b, c), 0)),
                  pl.BlockSpec((NH_M * DQK_M, rows), lambda b, c: (0, rb(b, c))),
                  pl.BlockSpec((2 * NH_M, rows), lambda b, c: (0, rb(b, c))),
                  pl.BlockSpec((None, 2 * NH_M, 128), lambda b, c: (l, 0, 0)),
                  pl.BlockSpec((None, 1, D_MODEL), lambda b, c: (l, 0, 0)),
                  pl.BlockSpec((None, NH_M, DQK_M, DV_M), first4),
                  pl.BlockSpec((None, NH_M, DQK_M, 128), first4),
                  pl.BlockSpec((None, NH_M, 8, 128), first4),
                  pl.BlockSpec((N_META, kw), lambda b, c: (META_ROW0 // N_META, M_KA // kw)),
                  pl.BlockSpec((N_META, kw), lambda b, c: (META_ROW0 // N_META, M_VA // kw))],
        out_specs=[pl.BlockSpec((rows, D_MODEL), lambda b, c: (b * nc + c, 0)),
                   pl.BlockSpec((1, NH_M, DQK_M, DV_M), st4),
                   pl.BlockSpec((1, NH_M, DQK_M, 128), st4),
                   pl.BlockSpec((1, NH_M, 8, 128), st4)],
        out_shape=[jax.ShapeDtypeStruct((nb * nc * rows, D_MODEL), BF16),
                   jax.ShapeDtypeStruct((nb, NH_M, DQK_M, DV_M), F32),
                   jax.ShapeDtypeStruct((nb, NH_M, DQK_M, 128), F32),
                   jax.ShapeDtypeStruct((nb, NH_M, 8, 128), F32)],
        scratch_shapes=[pltpu.VMEM((NH_M, DQK_M, DV_M), F32),
                        pltpu.VMEM((NH_M, DQK_M, 128), F32),
                        pltpu.VMEM((NH_M, 8, 128), F32)],
        compiler_params=_cparams(2),
        name="mix",
    )(sinks, misc, misc, pa, qa, kt, gt, bias8, nmh, ct0, n0, m0, misc_small, misc_small)


def _swa_chain(c, k, kv, sink_ref, q_ref, misc_ref, miscp_ref, gates_out, mk_ref, mv_ref, ya, has_meta):
    L = WINDOW
    R = GQ * L
    rs = slice(k * L, (k + 1) * L)
    row = lax.broadcasted_iota(jnp.int32, (R, L), 0)
    col = lax.broadcasted_iota(jnp.int32, (R, L), 1)
    tril = col <= (row & (L - 1))
    head_of_row = lax.shift_right_logical(lax.broadcasted_iota(jnp.int32, (R, 1), 0), L.bit_length() - 1)
    mm = _mm
    cast = lambda a: a.astype(BF16)
    with_ones = lambda v: jnp.concatenate([v, jnp.ones((v.shape[0], HD), v.dtype)], axis=1)
    if k == 0:
        prev = lambda cols: cast(miscp_ref[:, cols])
        prev_ok = jnp.full((R, L), c, jnp.int32) > 0
    else:
        prev = lambda cols: cast(misc_ref[(k - 1) * L:k * L, cols])
        prev_ok = None
    ka = slice(M_KA + kv * HD, M_KA + (kv + 1) * HD)
    va = slice(M_VA + kv * HD, M_VA + (kv + 1) * HD)
    ks = slice(kv * HD, (kv + 1) * HD)
    heads = [kv * GQ + g for g in range(GQ)]
    q = jnp.concatenate([cast(q_ref[rs, hd * HD:(hd + 1) * HD]) for hd in heads], axis=0)
    sink = jnp.zeros((R, 1), F32)
    for g, hd in enumerate(heads):
        sink = jnp.where(head_of_row == g, sink_ref[hd], sink)
    yield
    yield
    s_cur = mm(q, cast(misc_ref[rs, ka]), nt=True)
    s_prev = mm(q, prev(ka), nt=True)
    if has_meta:
        s_m = mm(q, cast(mk_ref[:, ks]), nt=True)
    yield
    if prev_ok is not None:
        s_prev = jnp.where(prev_ok, s_prev, -jnp.inf)
    band = jnp.where(tril, s_cur, s_prev)
    mx = jnp.maximum(jnp.max(band, axis=1, keepdims=True), sink)
    if has_meta:
        mx = jnp.maximum(mx, jnp.max(s_m, axis=1, keepdims=True))
        p_m = cast(jnp.exp(s_m - mx))
    yield
    p = jnp.exp(band - mx)
    p_cur = cast(jnp.where(tril, p, 0.0))
    p_prev = cast(jnp.where(tril, 0.0, p))
    yield
    oe = (mm(p_cur, with_ones(cast(misc_ref[rs, va])))
          + mm(p_prev, with_ones(prev(va))))
    if has_meta:
        oe = oe + mm(p_m, with_ones(cast(mv_ref[:, ks])))
    yield
    denom = oe[:, HD:HD + 1] + jnp.exp(sink - mx)
    o = oe[:, 0:HD] / denom
    yield
    for g, hd in enumerate(heads):
        ya[k, hd] = o[g * L:(g + 1) * L] * gates_out[k, "a"][:, hd * HD:(hd + 1) * HD]


def _tail_kernel(y_ref, x_ref, gn_ref, gnext_ref, wo_ref, wu_ref, wd_ref, o_ref, *rest, tf, last_layer):
    x1 = x_ref[...] + _dot(y_ref[...], wo_ref[...])
    hf = _rms(x1, gn_ref[...]).astype(BF16)
    acc = x1
    for j in range(D_FF // tf):
        a = jnp.square(jnp.maximum(_dot(hf, wu_ref[:, j * tf:(j + 1) * tf]), 0.0)).astype(BF16)
        acc = acc + _dot(a, wd_ref[j * tf:(j + 1) * tf, :])
    if last_layer:
        o_ref[...] = _rms(acc, gnext_ref[...])
    else:
        o_ref[...] = acc
        rest[0][...] = _rms(acc, gnext_ref[...]).astype(BF16)


def _tail(l, y, x, g_mlp, g_next, weights, tm, tf):
    rows = x.shape[0]
    last_layer = l == DEPTH - 1
    row_spec = pl.BlockSpec((tm, D_MODEL), lambda i: (i, 0))
    resident = lambda r, c: pl.BlockSpec((None, r, c), lambda i: (l, 0, 0), pipeline_mode=pl.Buffered(1))
    out_specs, out_shape = [row_spec], [jax.ShapeDtypeStruct((rows, D_MODEL), F32)]
    if not last_layer:
        out_specs.append(row_spec)
        out_shape.append(jax.ShapeDtypeStruct((rows, D_MODEL), BF16))
    return pl.pallas_call(
        functools.partial(_tail_kernel, tf=tf, last_layer=last_layer),
        grid=(rows // tm,),
        in_specs=[row_spec, row_spec,
                  pl.BlockSpec((None, 1, D_MODEL), lambda i: (l, 0, 0)),
                  pl.BlockSpec((None, 1, D_MODEL), lambda i: (l + 1, 0, 0)),
                  resident(D_MODEL, D_MODEL), resident(D_MODEL, D_FF), resident(D_FF, D_MODEL)],
        out_specs=out_specs,
        out_shape=out_shape,
        compiler_params=_cparams(1),
        name="tail",
    )(y, x, g_mlp, g_next, *weights)


def _mlstm_step_kernel(q_ref, k_ref, v_ref, o_ref, gm_ref, ig_ref, fg_ref, m_ref, n_ref, c_ref,
                       kall_ref, vall_ref, bi_ref, bf_ref, nmh_ref, *rest):
    y_ref, co_ref, no_ref, mo_ref = rest[-4:]
    i = pl.program_id(0)
    nb = DEC_BLK
    row8 = lax.broadcasted_iota(jnp.int32, (nb, DV_M), 0)
    rowb = lax.broadcasted_iota(jnp.int32, (kall_ref.shape[0], DQK_M), 0)
    for h in range(NH_M):
        ds = slice(h * DQK_M, (h + 1) * DQK_M)
        vs = slice(h * DV_M, (h + 1) * DV_M)
        q, k, v, nrm = q_ref[:, ds], k_ref[:, ds], v_ref[:, vs], n_ref[:, ds]
        ig = ig_ref[:, ds] + bi_ref[:, ds]
        lf = _log_sigmoid(fg_ref[:, ds] + bf_ref[:, ds])
        m_prev = m_ref[:, ds]
        m_t = jnp.maximum(lf + m_prev, ig)
        w_inter = jnp.exp(lf + m_prev - m_t)
        w_in = jnp.exp(ig - m_t)
        s = jnp.sum(q * k, axis=1, keepdims=True) * w_in
        qb = q.astype(BF16)
        kall = kall_ref[:, ds].astype(BF16)
        vt = vall_ref[:, vs].T.astype(BF16)
        cq = jnp.zeros((nb, DV_M), F32)
        for bb in range(nb):
            c_old = c_ref[bb, h]
            cq = jnp.where(row8 == bb, _dot_nt(qb, c_old.astype(BF16)), cq)
            outer = _dot(vt, jnp.where(rowb == i * nb + bb, kall, jnp.zeros_like(kall)))
            co_ref[bb, h] = w_inter[bb:bb + 1, :] * c_old + w_in[bb:bb + 1, :] * outer
        num = w_inter[:, 0:1] * cq + s[:, 0:1] * v
        den = w_inter * jnp.sum(nrm * q, axis=1, keepdims=True) + s
        hh = num / jnp.maximum(jnp.abs(den), jnp.exp(-m_t))[:, 0:1]
        no_ref[:, ds] = w_inter * nrm + w_in * k
        mo_ref[:, ds] = m_t
        hn = hh * lax.rsqrt(jnp.mean(hh * hh, axis=1, keepdims=True) + EPS) * nmh_ref[:, vs]
        y_ref[:, vs] = jax.nn.sigmoid(gm_ref[:, vs]) * (jax.nn.sigmoid(o_ref[:, vs]) * hn)


def _mlstm_step(l, misc, km, pa, grep, m_all, n_all, c_all, c_out_prev, bi, bf, nmh):
    nbatch = c_all.shape[1]
    qd = NH_M * DQK_M
    rowq = lambda j: pl.BlockSpec((DEC_BLK, qd), lambda i: (i, j))
    rowv = lambda j: pl.BlockSpec((DEC_BLK, D_MODEL), lambda i: (i, j))
    lrow = pl.BlockSpec((None, DEC_BLK, qd), lambda i: (l, i, 0))
    lconst = lambda width: pl.BlockSpec((None, 1, width), lambda i: (l, 0, 0))
    cspec = pl.BlockSpec((None, DEC_BLK, NH_M, DV_M, DQK_M), lambda i: (l, i, 0, 0, 0))
    in_specs = [rowq(M_QM // qd), rowq(0), rowv(A_VM // D_MODEL), rowv(A_OM // D_MODEL), rowv(A_GM // D_MODEL),
                rowq(0), rowq(1), lrow, lrow, cspec,
                pl.BlockSpec((nbatch, qd), lambda i: (0, 0)),
                pl.BlockSpec((nbatch, D_MODEL), lambda i: (0, A_VM // D_MODEL)),
                lconst(qd), lconst(qd), lconst(D_MODEL)]
    args = [misc, km, pa, pa, pa, grep, grep, m_all, n_all, c_all, km, pa, bi, bf, nmh]
    aliases = {}
    if c_out_prev is not None:
        aliases = {len(args): 1}
        in_specs.append(pl.BlockSpec(memory_space=pl.ANY))
        args.append(c_out_prev)
    return pl.pallas_call(
        _mlstm_step_kernel,
        grid=(nbatch // DEC_BLK,),
        in_specs=in_specs,
        out_specs=[rowv(0), cspec, rowq(0), rowq(0)],
        out_shape=[jax.ShapeDtypeStruct((nbatch, D_MODEL), F32),
                   jax.ShapeDtypeStruct(c_all.shape, F32),
                   jax.ShapeDtypeStruct((nbatch, qd), F32),
                   jax.ShapeDtypeStruct((nbatch, qd), F32)],
        input_output_aliases=aliases,
        compiler_params=_cparams(1),
        name="mlstm_step",
    )(*args)


def _swa_step_kernel(q_ref, ga_ref, kn_ref, vn_ref, mk_ref, mv_ref, wk_ref, wv_ref, sink_ref, *rest):
    y_ref, ok_ref, ov_ref = rest[-3:]
    W = WINDOW
    hrow = lax.broadcasted_iota(jnp.int32, (N_HEADS, HD), 0)
    first = hrow < GQ
    wcol = lax.broadcasted_iota(jnp.int32, (N_HEADS, W), 1)
    wrow = lax.broadcasted_iota(jnp.int32, (N_KV * W, HD), 0)
    sink = sink_ref[...][:, 0:1]

    def per_kv(fn):
        a0, a1 = fn(0), fn(1)
        return jnp.where(first[:, 0:a0.shape[1]], a0, a1)

    def batch_element(bb):
        q = q_ref[bb]
        qb = q.astype(BF16)
        rows_of = lambda ref, kv, n: ref[bb, pl.ds(kv, n, stride=N_KV), :].astype(BF16)
        kn, vn = kn_ref[bb:bb + 1, :], vn_ref[bb:bb + 1, :]
        kvs = lambda a, kv: a[:, kv * HD:(kv + 1) * HD]
        s_w = jnp.where(wcol >= 1, per_kv(lambda kv: _dot_nt(qb, rows_of(wk_ref, kv, W))), -jnp.inf)
        s_m = per_kv(lambda kv: _dot_nt(qb, rows_of(mk_ref, kv, N_META)))
        kn8 = per_kv(lambda kv: jnp.broadcast_to(kvs(kn, kv), (N_HEADS, HD)))
        vn8 = per_kv(lambda kv: jnp.broadcast_to(kvs(vn, kv), (N_HEADS, HD)))
        s_n = jnp.sum(q * kn8, axis=1, keepdims=True)
        yield
        mx = jnp.maximum(jnp.maximum(jnp.max(s_w, axis=1, keepdims=True), jnp.max(s_m, axis=1, keepdims=True)),
                         jnp.maximum(s_n, sink))
        p_w, p_m, p_n = jnp.exp(s_w - mx), jnp.exp(s_m - mx), jnp.exp(s_n - mx)
        denom = (jnp.sum(p_w, axis=1, keepdims=True) + jnp.sum(p_m, axis=1, keepdims=True) + p_n
                 + jnp.exp(sink - mx))
        yield
        o = (per_kv(lambda kv: _dot(p_w.astype(BF16), rows_of(wv_ref, kv, W)))
             + per_kv(lambda kv: _dot(p_m.astype(BF16), rows_of(mv_ref, kv, N_META)))
             + p_n * vn8)
        yield
        y_ref[bb] = (o / denom) * jax.nn.sigmoid(ga_ref[bb])
        for src_ref, new, dst_ref in ((wk_ref, kn, ok_ref), (wv_ref, vn, ov_ref)):
            shifted = pltpu.roll(src_ref[bb], N_KV * (W - 1), 0)
            for kv in range(N_KV):
                shifted = jnp.where(wrow == N_KV * (W - 1) + kv, kvs(new, kv), shifted)
            dst_ref[bb] = shifted

    _run_interleaved([batch_element(bb) for bb in range(DEC_BLK)])


def _swa_step(l, q, ga, misc, mk_all, mv_all, wk_all, wv_all, ok_prev, ov_prev, sink8):
    nbatch = q.shape[0]
    kw = N_KV * HD
    hspec = pl.BlockSpec((DEC_BLK, N_HEADS, HD), lambda i: (i, 0, 0))
    mspec = pl.BlockSpec((None, DEC_BLK, N_KV * N_META, HD), lambda i: (l, i, 0, 0))
    wspec = pl.BlockSpec((None, DEC_BLK, N_KV * WINDOW, HD), lambda i: (l, i, 0, 0))
    in_specs = [hspec, hspec,
                pl.BlockSpec((DEC_BLK, kw), lambda i: (i, M_KA // kw)),
                pl.BlockSpec((DEC_BLK, kw), lambda i: (i, M_VA // kw)),
                mspec, mspec, wspec, wspec,
                pl.BlockSpec((None, N_HEADS, HD), lambda i: (l, 0, 0))]
    args = [q, ga, misc, misc, mk_all, mv_all, wk_all, wv_all, sink8]
    aliases = {}
    if ok_prev is not None:
        aliases = {len(args): 1, len(args) + 1: 2}
        in_specs += [pl.BlockSpec(memory_space=pl.ANY)] * 2
        args += [ok_prev, ov_prev]
    return pl.pallas_call(
        _swa_step_kernel,
        grid=(nbatch // DEC_BLK,),
        in_specs=in_specs,
        out_specs=[hspec, wspec, wspec],
        out_shape=[jax.ShapeDtypeStruct((nbatch, N_HEADS, HD), F32),
                   jax.ShapeDtypeStruct(wk_all.shape, F32),
                   jax.ShapeDtypeStruct(wv_all.shape, F32)],
        input_output_aliases=aliases,
        compiler_params=_cparams(1),
        name="swa_step",
    )(*args)


def _rope_tables(pos):
    half = ROT_DIM // 2
    inv = jnp.float32(ROPE_THETA) ** (-jnp.arange(half, dtype=F32) * 2.0 / ROT_DIM)
    ang = pos.astype(F32)[:, None] * inv[None, :]
    cos, sin = jnp.cos(ang), jnp.sin(ang)
    t = pos.shape[0]
    cs = jnp.concatenate([cos, cos, jnp.ones((t, HD - ROT_DIM), F32)], axis=1)
    s1 = jnp.concatenate([jnp.zeros((t, half), F32), sin, jnp.zeros((t, HD - ROT_DIM), F32)], axis=1)
    s2 = jnp.concatenate([-sin, jnp.zeros((t, HD - half), F32)], axis=1)
    return cs, s1, s2


PACK_ROWS = 128


def _pack_kernel(wt_ref, wa_ref, wb_ref):
    edges = np.concatenate([[0], np.cumsum(IN_WIDTHS)])
    q_m, k_m, v_m, ig, fg, o_m, q_a, k_a, v_a, g_m, g_a = [(int(edges[i]), int(edges[i + 1])) for i in range(11)]

    def put(dst_ref, dst0, seg):
        lo, hi = seg
        dst_ref[:, dst0:dst0 + hi - lo] = wt_ref[lo:hi, :].T.astype(BF16)

    for dst0, seg in ((A_VM, v_m), (A_OM, o_m), (A_GM, g_m), (A_GA, g_a)):
        put(wa_ref, dst0, seg)
    for dst0, seg in ((B_QA, q_a), (B_KA, k_a), (B_VA, v_a), (B_QM, q_m), (B_KM, k_m)):
        put(wb_ref, dst0, seg)
    tile = wt_ref[ig[0]:ig[0] + 128, :].T
    lane = lax.broadcasted_iota(jnp.int32, tile.shape, 1)
    wb_ref[:, B_G:B_TOT] = jnp.where(lane < 2 * NH_M, tile, 0.0).astype(BF16)


def _pack_in_weights(w_in):
    starts = np.concatenate([[0], np.cumsum(IN_WIDTHS)[:-1]])
    assert IN_WIDTHS[3] == IN_WIDTHS[4] == NH_M and all(s % 8 == 0 for i, s in enumerate(starts) if i != 4)
    depth, rows, cols = w_in.shape
    return pl.pallas_call(
        _pack_kernel,
        grid=(depth, rows // PACK_ROWS),
        in_specs=[pl.BlockSpec((None, cols, PACK_ROWS), lambda l, i: (l, 0, i))],
        out_specs=[pl.BlockSpec((None, PACK_ROWS, A_TOT), lambda l, i: (l, i, 0)),
                   pl.BlockSpec((None, PACK_ROWS, B_TOT), lambda l, i: (l, i, 0))],
        out_shape=[jax.ShapeDtypeStruct((depth, rows, A_TOT), BF16),
                   jax.ShapeDtypeStruct((depth, rows, B_TOT), BF16)],
        compiler_params=_cparams(2),
        name="pack_w_in",
    )(jnp.swapaxes(w_in, 1, 2))


def _rep(a, width=128):
    return jnp.repeat(a, width, axis=1)


def kernel(x_prompt, x_sample, state_C, state_n, state_m, cache_meta_k, cache_meta_v, cache_win_k, cache_win_v,
           meta_tokens, norm_mix, w_in, b_igate, b_fgate, norm_mh, attn_sinks, w_out, norm_mlp, w_up, w_down,
           norm_final):
    B, S, _ = x_prompt.shape
    DB = x_sample.shape[0]
    W = cache_win_k.shape[2]
    assert (S % CHUNK, W, x_sample.shape[1], DB % DEC_BLK) == (0, WINDOW, 1, 0)
    assert (state_C.dtype, cache_win_k.dtype, cache_win_v.dtype) == (F32, F32, F32)
    nchunk = S // CHUNK
    kw = N_KV * HD

    w_a, w_b = _pack_in_weights(w_in)
    tail_w = (w_out.astype(BF16), w_up.astype(BF16), w_down.astype(BF16))
    g_mix = norm_mix.astype(F32).reshape(DEPTH, 1, D_MODEL)
    g_mlp = norm_mlp.astype(F32).reshape(DEPTH, 1, D_MODEL)
    g_next = jnp.concatenate([g_mix, norm_final.astype(F32).reshape(1, 1, D_MODEL)], axis=0)
    nmh = norm_mh.astype(F32).reshape(DEPTH, 1, D_MODEL)
    sinks = attn_sinks.astype(F32)
    sink8 = jnp.broadcast_to(sinks[:, :, None], (DEPTH, N_HEADS, HD))
    bias8 = jnp.broadcast_to(jnp.concatenate([b_igate, b_fgate], axis=1).astype(F32)[:, :, None],
                             (DEPTH, 2 * NH_M, 128))
    bi_rep = _rep(b_igate.astype(F32)).reshape(DEPTH, 1, NH_M * 128)
    bf_rep = _rep(b_fgate.astype(F32)).reshape(DEPTH, 1, NH_M * 128)
    m_rep = jnp.repeat(state_m.astype(F32), 128, axis=2)
    n_all = state_n.astype(F32).reshape(DEPTH, DB, NH_M * DQK_M)
    rows_view = lambda a: a.reshape(DEPTH, DB, -1, HD)
    mk_all, mv_all = rows_view(cache_meta_k), rows_view(cache_meta_v)
    wk_all, wv_all = rows_view(cache_win_k), rows_view(cache_win_v)

    tabs_p = _rope_tables(N_META + jnp.arange(S, dtype=jnp.int32))
    pos_small = jnp.concatenate([jnp.full((DB,), PAST_LEN, jnp.int32), jnp.arange(N_META, dtype=jnp.int32),
                                 jnp.zeros((SMALL_ROWS - DB - N_META,), jnp.int32)])
    tabs_s = _rope_tables(pos_small)

    xp = x_prompt.reshape(B * S, D_MODEL)
    xs = jnp.concatenate([x_sample.reshape(DB, D_MODEL), meta_tokens.astype(F32),
                          jnp.zeros((SMALL_ROWS - DB - N_META, D_MODEL), F32)], axis=0)
    zero_state = (jnp.zeros((1, NH_M, DQK_M, DV_M), F32), jnp.zeros((1, NH_M, DQK_M, 128), F32),
                  jnp.zeros((1, NH_M, 8, 128), F32))

    outs = {k: [] for k in ("ct", "n", "m", "meta", "win", "sn", "sm")}
    sC = swk = swv = None
    hs_ = _norm(0, xs, g_mix, SMALL_ROWS, BF16)
    hp = _norm(0, xp, g_mix, NORM_TM, BF16)
    for l in range(DEPTH):
        pa_s, qa_s, misc_s, kt_s, gt_s, km_s, grep_s = _inproj(l, hs_, w_a, w_b, tabs_s, SMALL_ROWS, small=True)
        y_meta, ct_m, n_m, m_m = _mix(l, sinks, qa_s, misc_s, pa_s, kt_s, gt_s, bias8, nmh, *zero_state, misc_s,
                                      nb=1, nblk=1, sub=1, row0=META_ROW0, valid_len=N_META, has_meta=False)
        ys_m, sC, sn, sm = _mlstm_step(l, misc_s, km_s, pa_s, grep_s, m_rep, n_all, state_C, sC, bi_rep, bf_rep, nmh)
        ys_a, swk, swv = _swa_step(
            l, qa_s[:DB].reshape(DB, N_HEADS, HD), pa_s[:DB, A_GA:A_GA + D_MODEL].reshape(DB, N_HEADS, HD),
            misc_s, mk_all, mv_all, wk_all, wv_all, swk, swv, sink8)
        y_s = jnp.concatenate([(ys_m + ys_a.reshape(DB, D_MODEL)).astype(BF16), y_meta], axis=0)
        res = _tail(l, y_s, xs, g_mlp, g_next, tail_w, tm=SMALL_ROWS, tf=TAIL_TF)
        xs, hs_ = res[0], res[-1]

        pa, qa, misc, kt, gt = _inproj(l, hp, w_a, w_b, tabs_p, INPROJ_TM, small=False)
        y, ct_f, n_f, m_f = _mix(l, sinks, qa, misc, pa, kt, gt, bias8, nmh, ct_m, n_m, m_m, misc_s,
                                 nb=B, nblk=nchunk, sub=MIX_SUB, row0=0, valid_len=CHUNK, has_meta=True)
        res = _tail(l, y, xp, g_mlp, g_next, tail_w, tm=TAIL_TM, tf=TAIL_TF)
        xp, hp = res[0], res[-1]

        for key, val in (("ct", ct_f), ("n", n_f), ("m", m_f), ("sn", sn), ("sm", sm),
                         ("meta", misc_s[META_ROW0:META_ROW0 + N_META, M_KA:M_VA + kw]),
                         ("win", misc.reshape(B, S, M_TOT)[:, S - WINDOW:, M_KA:M_VA + kw])):
            outs[key].append(val)

    y_prompt = xp.reshape(B, S, D_MODEL)
    y_sample = xs[:DB].reshape(DB, 1, D_MODEL)
    st = lambda k: jnp.stack(outs[k])
    p_c = jnp.swapaxes(st("ct"), 3, 4).astype(state_C.dtype)
    p_n = st("n")[..., 0].astype(state_n.dtype)
    p_m = st("m")[..., 0, 0].astype(state_m.dtype)
    meta = jnp.broadcast_to(st("meta").astype(F32)[:, None], (DEPTH, B, N_META, 2 * kw))
    win = st("win").astype(F32)
    kv_of = lambda a, lo: a[..., lo:lo + kw].reshape(a.shape[:-1] + (N_KV, HD))
    s_n = st("sn").reshape(DEPTH, DB, NH_M, DQK_M).astype(state_n.dtype)
    s_m = st("sm").reshape(DEPTH, DB, NH_M, 128)[..., 0].astype(state_m.dtype)
    return (y_prompt, y_sample, p_c, p_n, p_m, kv_of(meta, 0), kv_of(meta, kw), kv_of(win, 0), kv_of(win, kw),
            sC, s_n, s_m, swk.reshape(cache_win_k.shape), swv.reshape(cache_win_v.shape))
```

```python
import functools

import jax
import jax.numpy as jnp
import numpy as np
from jax import lax
from jax.experimental import pallas as pl
from jax.experimental.pallas import tpu as pltpu

F32 = jnp.float32
BF16 = jnp.bfloat16

D_MODEL = 1024
DEPTH = 4
PAST_LEN = 8192
N_META = 16
NH_M = 4
DV_M = 256
DQK_M = 128
CHUNK = 128
N_HEADS = 8
N_KV = 2
GQ = N_HEADS // N_KV
HD = 128
WINDOW = 128
ROT_DIM = 32
ROPE_THETA = 500000.0
D_FF = 4 * D_MODEL
EPS = 1e-6
IN_WIDTHS = (NH_M * DQK_M, NH_M * DQK_M, NH_M * DV_M, NH_M, NH_M, NH_M * DV_M,
             N_HEADS * HD, N_KV * HD, N_KV * HD, D_MODEL, D_MODEL)

A_VM, A_OM, A_GM, A_GA, A_TOT = 0, 1024, 2048, 3072, 4096
A_TN = 1024
INPROJ_TM = 512
NORM_TM = 1024
B_QA, B_KA, B_VA, B_QM, B_KM, B_G, B_TOT = 0, 1024, 1280, 1536, 2048, 2560, 2688
M_KA, M_VA, M_QM, M_TOT = 0, 256, 512, 1024
TAIL_TM = 512
TAIL_TF = 1024
MIX_SUB = 8
BLOCK_LAG = 2
MLSTM_STAGES = 8
SMALL_ROWS = 256
META_ROW0 = 128
DEC_BLK = 16
VMEM_LIMIT = 56 * 1024 * 1024


def _cparams(n_axes):
    return pltpu.CompilerParams(dimension_semantics=("arbitrary",) * n_axes, vmem_limit_bytes=VMEM_LIMIT)


def _rms(xf, gain):
    ms = jnp.mean(xf * xf, axis=-1, keepdims=True)
    return xf * lax.rsqrt(ms + EPS) * gain


def _dot(a, b):
    return jnp.dot(a, b, preferred_element_type=F32)


def _dot_nt(a, b):
    return lax.dot_general(a, b, (((1,), (1,)), ((), ())), preferred_element_type=F32)


def _mm(a, b, nt=False):
    return (_dot_nt if nt else _dot)(a.astype(BF16), b.astype(BF16))


def _log_sigmoid(x):
    return jnp.minimum(x, 0.0) - jnp.log1p(jnp.exp(-jnp.abs(x)))


def _run_interleaved(chains):
    chains = list(chains)
    while chains:
        alive = []
        for chain in chains:
            try:
                next(chain)
                alive.append(chain)
            except StopIteration:
                pass
        chains = alive


def _sigmoid_bf16(x):
    half = jnp.asarray(0.5, x.dtype)
    return jnp.tanh(x * half) * half + half


def _norm_kernel(x_ref, gain_ref, o_ref):
    o_ref[...] = _rms(x_ref[...], gain_ref[...]).astype(o_ref.dtype)


def _norm(l, x, gains, tm, out_dtype):
    rows = x.shape[0]
    return pl.pallas_call(
        _norm_kernel,
        grid=(rows // tm,),
        in_specs=[pl.BlockSpec((tm, D_MODEL), lambda i: (i, 0)),
                  pl.BlockSpec((None, 1, D_MODEL), lambda i: (l, 0, 0))],
        out_specs=pl.BlockSpec((tm, D_MODEL), lambda i: (i, 0)),
        out_shape=jax.ShapeDtypeStruct((rows, D_MODEL), out_dtype),
        compiler_params=_cparams(1),
        name="norm",
    )(x, gains)


def _inproj_kernel(h_ref, wa_ref, w_ref, cs_ref, s1_ref, s2_ref, pa_ref, qa_ref, misc_ref, kt_ref, gt_ref,
                   *row_form):
    small = bool(row_form)
    mm = lambda lo, hi: _dot(h_ref[...], w_ref[:, lo:hi])

    for c in range(A_TOT // A_TN):
        cs = slice(c * A_TN, (c + 1) * A_TN)
        pa_ref[:, cs] = _dot(h_ref[...], wa_ref[:, cs]).astype(pa_ref.dtype)

    def rope(seg):
        return (seg * cs_ref[...] + pltpu.roll(seg, 16, 1) * s1_ref[...]
                + pltpu.roll(seg, HD - 16, 1) * s2_ref[...])

    acc = mm(B_QA, B_KA)
    for c in range(N_HEADS):
        cs = slice(c * HD, (c + 1) * HD)
        qa_ref[:, cs] = (rope(acc[:, cs]) * (HD ** -0.5)).astype(qa_ref.dtype)

    acc = mm(B_KA, B_KM)
    for c in range(N_KV):
        cs = slice(M_KA + c * HD, M_KA + (c + 1) * HD)
        misc_ref[:, cs] = rope(acc[:, cs]).astype(misc_ref.dtype)
    misc_ref[:, M_VA:M_QM] = acc[:, M_VA:M_QM].astype(misc_ref.dtype)
    misc_ref[:, M_QM:M_TOT] = (acc[:, M_QM:M_TOT] * (DQK_M ** -0.5)).astype(misc_ref.dtype)

    acc = mm(B_KM, B_TOT)
    kt_ref[...] = acc[:, 0:NH_M * DQK_M].T.astype(kt_ref.dtype)
    gt_ref[...] = acc[:, NH_M * DQK_M:B_TOT - B_KM].T[0:2 * NH_M]

    if small:
        km_ref, grep_ref = row_form
        km_ref[...] = acc[:, 0:NH_M * DQK_M]
        for c in range(2 * NH_M):
            col = NH_M * DQK_M + c
            grep_ref[:, c * 128:(c + 1) * 128] = jnp.broadcast_to(acc[:, col:col + 1], (acc.shape[0], 128))


G_REP = 2 * NH_M * 128


def _inproj(l, h, w_a, w_b, tabs, tm, small):
    rows = h.shape[0]
    tab_blocks = tabs[0].shape[0] // tm
    tab_spec = pl.BlockSpec((tm, HD), lambda i: (i % tab_blocks, 0))
    dt = F32 if small else BF16
    row_spec = lambda width: pl.BlockSpec((tm, width), lambda i: (i, 0))
    resident = lambda cols: pl.BlockSpec((None, D_MODEL, cols), lambda i: (l, 0, 0), pipeline_mode=pl.Buffered(1))
    out_specs = [row_spec(A_TOT), row_spec(D_MODEL), row_spec(M_TOT),
                 pl.BlockSpec((NH_M * DQK_M, tm), lambda i: (0, i)),
                 pl.BlockSpec((2 * NH_M, tm), lambda i: (0, i))]
    out_shape = [jax.ShapeDtypeStruct((rows, A_TOT), dt),
                 jax.ShapeDtypeStruct((rows, D_MODEL), dt),
                 jax.ShapeDtypeStruct((rows, M_TOT), dt),
                 jax.ShapeDtypeStruct((NH_M * DQK_M, rows), dt),
                 jax.ShapeDtypeStruct((2 * NH_M, rows), F32)]
    if small:
        out_specs += [row_spec(NH_M * DQK_M), row_spec(G_REP)]
        out_shape += [jax.ShapeDtypeStruct((rows, NH_M * DQK_M), F32), jax.ShapeDtypeStruct((rows, G_REP), F32)]
    return pl.pallas_call(
        _inproj_kernel,
        grid=(rows // tm,),
        in_specs=[pl.BlockSpec((tm, D_MODEL), lambda i: (i, 0)), resident(A_TOT), resident(B_TOT),
                  tab_spec, tab_spec, tab_spec],
        out_specs=out_specs,
        out_shape=out_shape,
        compiler_params=_cparams(1),
        name="inproj",
    )(h, w_a, w_b, *tabs)


def _mix_kernel(sink_ref, misc_ref, miscp_ref, pa_ref, qa_ref, kt_ref, g_ref, bias_ref, nmh_ref,
                ct0_ref, n0_ref, m0_ref, mk_ref, mv_ref,
                y_ref, ctf_ref, nf_ref, mf_ref, ct_s, n_s, m_s, *, l, nc, sub, valid_len, has_meta):
    c = pl.program_id(1)
    L = CHUNK
    mm = _mm
    cast = lambda a: a.astype(BF16)

    @pl.when(c == 0)
    def _():
        ct_s[...] = ct0_ref[...]
        n_s[...] = n0_ref[...]
        m_s[...] = m0_ref[...]

    row = lax.broadcasted_iota(jnp.int32, (L, L), 0)
    col = lax.broadcasted_iota(jnp.int32, (L, L), 1)
    tril = col <= row
    eye = col == row

    gates = g_ref[...] + bias_ref[:, 0:1]
    ig_sub = gates[0:NH_M]
    lf_sub = _log_sigmoid(gates[NH_M:2 * NH_M])
    if valid_len < L:
        ok = lax.broadcasted_iota(jnp.int32, (NH_M, sub * L), 1) < valid_len
        ig_sub = jnp.where(ok, ig_sub, -jnp.inf)
        lf_sub = jnp.where(ok, lf_sub, 0.0)

    ym, ya, gates_out = {}, {}, {}

    def mlstm_head(k, h):
        rs = slice(k * L, (k + 1) * L)
        ig = ig_sub[h:h + 1, rs]
        lf = lf_sub[h:h + 1, rs]
        b_col = jnp.sum(jnp.where(tril, lf, 0.0), axis=1, keepdims=True)
        b_row = jnp.sum(jnp.where(eye, b_col, 0.0), axis=0, keepdims=True)
        yield
        m_prev = m_s[h, 0:1, 0:1]
        a_col = b_col + m_prev
        dmat = jnp.where(tril, b_col - b_row + ig, -jnp.inf)
        m_col = jnp.maximum(a_col, jnp.max(dmat, axis=1, keepdims=True))
        w_inter = jnp.exp(a_col - m_col)
        wmat = jnp.exp(dmat - m_col)
        m_new = m_col[L - 1:L]
        m_s[h] = jnp.broadcast_to(m_new, (8, 128))
        q = cast(misc_ref[rs, M_QM + h * DQK_M:M_QM + (h + 1) * DQK_M])
        kt = kt_ref[h * DQK_M:(h + 1) * DQK_M, rs]
        v = cast(pa_ref[rs, A_VM + h * DV_M:A_VM + (h + 1) * DV_M])
        yield
        qk = mm(q, kt)
        yield
        s = qk * wmat
        s_b = cast(s)
        yield
        decay = w_inter[L - 1:L]
        g_row = jnp.exp(b_col[L - 1:L] - b_row + ig - m_new)
        ktg = kt.astype(F32) * g_row
        ktg_b = cast(ktg)
        yield
        intra = mm(s_b, v)
        upd = mm(ktg_b, v)
        yield
        ct = ct_s[h]
        nm = n_s[h]
        inter = mm(q, ct)
        qn = mm(q, nm)[:, 0:1]
        yield
        num = w_inter * inter + intra
        den = w_inter * qn + jnp.sum(s, axis=1, keepdims=True)
        hh = num / jnp.maximum(jnp.abs(den), jnp.exp(-m_col))
        ct_s[h] = decay * ct + upd
        n_s[h] = decay * nm + jnp.sum(ktg, axis=1, keepdims=True)
        yield
        hs = slice(h * DV_M, (h + 1) * DV_M)
        hn = hh * lax.rsqrt(jnp.mean(hh * hh, axis=1, keepdims=True) + EPS) * nmh_ref[:, hs]
        ym[k, h] = gates_out[k, h] * hn

    def gate_chain(k):
        rs = slice(k * L, (k + 1) * L)
        yield
        for h in range(NH_M):
            gates_out[k, h] = (_sigmoid_bf16(pa_ref[rs, A_GM + h * DV_M:A_GM + (h + 1) * DV_M])
                               * _sigmoid_bf16(pa_ref[rs, A_OM + h * DV_M:A_OM + (h + 1) * DV_M])
                               ).astype(F32)
            if h == 1:
                yield
        yield
        gates_out[k, "a"] = _sigmoid_bf16(pa_ref[rs, A_GA:A_GA + D_MODEL]).astype(F32)

    def store_chain(k):
        rs = slice(k * L, (k + 1) * L)
        for _ in range(MLSTM_STAGES):
            yield
        for hd in range(N_HEADS):
            ym_h = ym[k, (hd * HD) // DV_M][:, (hd * HD) % DV_M:(hd * HD) % DV_M + HD]
            y_ref[rs, hd * HD:(hd + 1) * HD] = (ym_h + ya[k, hd]).astype(y_ref.dtype)

    def delayed(chain, rounds):
        for _ in range(rounds):
            yield
        yield from chain

    chains = []
    for k in range(sub):
        block = [mlstm_head(k, h) for h in range(NH_M)] + [gate_chain(k)]
        block += [_swa_chain(c, k, kv, sink_ref.at[l], qa_ref, misc_ref, miscp_ref, gates_out, mk_ref, mv_ref, ya,
                             has_meta) for kv in range(N_KV)]
        block.append(store_chain(k))
        chains += [delayed(chain, k * BLOCK_LAG) for chain in block]
    _run_interleaved(chains)

    @pl.when(c == nc - 1)
    def _():
        ctf_ref[0] = ct_s[...]
        nf_ref[0] = n_s[...]
        mf_ref[0] = m_s[...]


def _mix(l, sinks, qa, misc, pa, kt, gt, bias8, nmh, ct0, n0, m0, misc_small, nb, nblk, sub, row0, valid_len,
         has_meta):
    nc = nblk // sub
    rows = sub * CHUNK
    r0 = row0 // rows
    rb = lambda b, c: r0 + b * nc + c
    rp = lambda b, c: (r0 + b * nc + c) * sub - jnp.minimum(c, 1)
    kw = N_KV * HD
    first4 = lambda b, c: (0, 0, 0, 0)
    st4 = lambda b, c: (b, 0, 0, 0)
    return pl.pallas_call(
        functools.partial(_mix_kernel, l=l, nc=nc, sub=sub, valid_len=valid_len, has_meta=has_meta),
        grid=(nb, nc),
        in_specs=[pl.BlockSpec(memory_space=pltpu.SMEM),
                  pl.BlockSpec((rows, M_TOT), lambda b, c: (rb(b, c), 0)),
                  pl.BlockSpec((CHUNK, 2 * kw), lambda b, c: (rp(b, c), 0)),
                  pl.BlockSpec((rows, A_TOT), lambda b, c: (rb(b, c), 0)),
                  pl.BlockSpec((rows, D_MODEL), lambda b, c: (rb(b, c), 0)),
                  pl.BlockSpec((NH_M * DQK_M, rows), lambda b, c: (0, rb(b, c))),
                  pl.BlockSpec((2 * NH_M, rows), lambda b, c: (0, rb(b, c))),
                  pl.BlockSpec((None, 2 * NH_M, 128), lambda b, c: (l, 0, 0)),
                  pl.BlockSpec((None, 1, D_MODEL), lambda b, c: (l, 0, 0)),
                  pl.BlockSpec((None, NH_M, DQK_M, DV_M), first4),
                  pl.BlockSpec((None, NH_M, DQK_M, 128), first4),
                  pl.BlockSpec((None, NH_M, 8, 128), first4),
                  pl.BlockSpec((N_META, kw), lambda b, c: (META_ROW0 // N_META, M_KA // kw)),
                  pl.BlockSpec((N_META, kw), lambda b, c: (META_ROW0 // N_META, M_VA // kw))],
        out_specs=[pl.BlockSpec((rows, D_MODEL), lambda b, c: (b * nc + c, 0)),
                   pl.BlockSpec((1, NH_M, DQK_M, DV_M), st4),
                   pl.BlockSpec((1, NH_M, DQK_M, 128), st4),
                   pl.BlockSpec((1, NH_M, 8, 128), st4)],
        out_shape=[jax.ShapeDtypeStruct((nb * nc * rows, D_MODEL), BF16),
                   jax.ShapeDtypeStruct((nb, NH_M, DQK_M, DV_M), F32),
                   jax.ShapeDtypeStruct((nb, NH_M, DQK_M, 128), F32),
                   jax.ShapeDtypeStruct((nb, NH_M, 8, 128), F32)],
        scratch_shapes=[pltpu.VMEM((NH_M, DQK_M, DV_M), F32),
                        pltpu.VMEM((NH_M, DQK_M, 128), F32),
                        pltpu.VMEM((NH_M, 8, 128), F32)],
        compiler_params=_cparams(2),
        name="mix",
    )(sinks, misc, misc, pa, qa, kt, gt, bias8, nmh, ct0, n0, m0, misc_small, misc_small)


def _swa_chain(c, k, kv, sink_ref, q_ref, misc_ref, miscp_ref, gates_out, mk_ref, mv_ref, ya, has_meta):
    L = WINDOW
    R = GQ * L
    rs = slice(k * L, (k + 1) * L)
    row = lax.broadcasted_iota(jnp.int32, (R, L), 0)
    col = lax.broadcasted_iota(jnp.int32, (R, L), 1)
    tril = col <= (row & (L - 1))
    head_of_row = lax.shift_right_logical(lax.broadcasted_iota(jnp.int32, (R, 1), 0), L.bit_length() - 1)
    mm = _mm
    cast = lambda a: a.astype(BF16)
    with_ones = lambda v: jnp.concatenate([v, jnp.ones((v.shape[0], HD), v.dtype)], axis=1)
    if k == 0:
        prev = lambda cols: cast(miscp_ref[:, cols])
        prev_ok = jnp.full((R, L), c, jnp.int32) > 0
    else:
        prev = lambda cols: cast(misc_ref[(k - 1) * L:k * L, cols])
        prev_ok = None
    ka = slice(M_KA + kv * HD, M_KA + (kv + 1) * HD)
    va = slice(M_VA + kv * HD, M_VA + (kv + 1) * HD)
    ks = slice(kv * HD, (kv + 1) * HD)
    heads = [kv * GQ + g for g in range(GQ)]
    q = jnp.concatenate([cast(q_ref[rs, hd * HD:(hd + 1) * HD]) for hd in heads], axis=0)
    sink = jnp.zeros((R, 1), F32)
    for g, hd in enumerate(heads):
        sink = jnp.where(head_of_row == g, sink_ref[hd], sink)
    yield
    yield
    s_cur = mm(q, cast(misc_ref[rs, ka]), nt=True)
    s_prev = mm(q, prev(ka), nt=True)
    if has_meta:
        s_m = mm(q, cast(mk_ref[:, ks]), nt=True)
    yield
    if prev_ok is not None:
        s_prev = jnp.where(prev_ok, s_prev, -jnp.inf)
    band = jnp.where(tril, s_cur, s_prev)
    mx = jnp.maximum(jnp.max(band, axis=1, keepdims=True), sink)
    if has_meta:
        mx = jnp.maximum(mx, jnp.max(s_m, axis=1, keepdims=True))
        p_m = cast(jnp.exp(s_m - mx))
    yield
    p = jnp.exp(band - mx)
    p_cur = cast(jnp.where(tril, p, 0.0))
    p_prev = cast(jnp.where(tril, 0.0, p))
    yield
    oe = (mm(p_cur, with_ones(cast(misc_ref[rs, va])))
          + mm(p_prev, with_ones(prev(va))))
    if has_meta:
        oe = oe + mm(p_m, with_ones(cast(mv_ref[:, ks])))
    yield
    denom = oe[:, HD:HD + 1] + jnp.exp(sink - mx)
    o = oe[:, 0:HD] / denom
    yield
    for g, hd in enumerate(heads):
        ya[k, hd] = o[g * L:(g + 1) * L] * gates_out[k, "a"][:, hd * HD:(hd + 1) * HD]


def _tail_kernel(y_ref, x_ref, gn_ref, gnext_ref, wo_ref, wu_ref, wd_ref, o_ref, *rest, tf, last_layer):
    half = x_ref.shape[0] // 2

    def rows(r):
        rs = slice(r * half, (r + 1) * half)
        x1 = x_ref[rs, :] + _dot(y_ref[rs, :], wo_ref[...])
        yield
        hf = _rms(x1, gn_ref[...]).astype(BF16)
        acc = x1
        for j in range(D_FF // tf):
            yield
            a = jnp.square(jnp.maximum(_dot(hf, wu_ref[:, j * tf:(j + 1) * tf]), 0.0)).astype(BF16)
            yield
            acc = acc + _dot(a, wd_ref[j * tf:(j + 1) * tf, :])
        yield
        if last_layer:
            o_ref[rs, :] = _rms(acc, gnext_ref[...])
        else:
            o_ref[rs, :] = acc
            rest[0][rs, :] = _rms(acc, gnext_ref[...]).astype(BF16)

    def delayed(chain, rounds):
        for _ in range(rounds):
            yield
        yield from chain

    _run_interleaved([rows(0), delayed(rows(1), 1)])


def _tail(l, y, x, g_mlp, g_next, weights, tm, tf):
    rows = x.shape[0]
    last_layer = l == DEPTH - 1
    row_spec = pl.BlockSpec((tm, D_MODEL), lambda i: (i, 0))
    resident = lambda r, c: pl.BlockSpec((None, r, c), lambda i: (l, 0, 0), pipeline_mode=pl.Buffered(1))
    out_specs, out_shape = [row_spec], [jax.ShapeDtypeStruct((rows, D_MODEL), F32)]
    if not last_layer:
        out_specs.append(row_spec)
        out_shape.append(jax.ShapeDtypeStruct((rows, D_MODEL), BF16))
    return pl.pallas_call(
        functools.partial(_tail_kernel, tf=tf, last_layer=last_layer),
        grid=(rows // tm,),
        in_specs=[row_spec, row_spec,
                  pl.BlockSpec((None, 1, D_MODEL), lambda i: (l, 0, 0)),
                  pl.BlockSpec((None, 1, D_MODEL), lambda i: (l + 1, 0, 0)),
                  resident(D_MODEL, D_MODEL), resident(D_MODEL, D_FF), resident(D_FF, D_MODEL)],
        out_specs=out_specs,
        out_shape=out_shape,
        compiler_params=_cparams(1),
        name="tail",
    )(y, x, g_mlp, g_next, *weights)


def _mlstm_step_kernel(q_ref, k_ref, v_ref, o_ref, gm_ref, ig_ref, fg_ref, m_ref, n_ref, c_ref,
                       kall_ref, vall_ref, bi_ref, bf_ref, nmh_ref, *rest):
    y_ref, co_ref, no_ref, mo_ref = rest[-4:]
    i = pl.program_id(0)
    nb = DEC_BLK
    row8 = lax.broadcasted_iota(jnp.int32, (nb, DV_M), 0)
    rowb = lax.broadcasted_iota(jnp.int32, (kall_ref.shape[0], DQK_M), 0)
    for h in range(NH_M):
        ds = slice(h * DQK_M, (h + 1) * DQK_M)
        vs = slice(h * DV_M, (h + 1) * DV_M)
        q, k, v, nrm = q_ref[:, ds], k_ref[:, ds], v_ref[:, vs], n_ref[:, ds]
        ig = ig_ref[:, ds] + bi_ref[:, ds]
        lf = _log_sigmoid(fg_ref[:, ds] + bf_ref[:, ds])
        m_prev = m_ref[:, ds]
        m_t = jnp.maximum(lf + m_prev, ig)
        w_inter = jnp.exp(lf + m_prev - m_t)
        w_in = jnp.exp(ig - m_t)
        s = jnp.sum(q * k, axis=1, keepdims=True) * w_in
        qb = q.astype(BF16)
        kall = kall_ref[:, ds].astype(BF16)
        vt = vall_ref[:, vs].T.astype(BF16)
        cq = jnp.zeros((nb, DV_M), F32)
        for bb in range(nb):
            c_old = c_ref[bb, h]
            cq = jnp.where(row8 == bb, _dot_nt(qb, c_old.astype(BF16)), cq)
            outer = _dot(vt, jnp.where(rowb == i * nb + bb, kall, jnp.zeros_like(kall)))
            co_ref[bb, h] = w_inter[bb:bb + 1, :] * c_old + w_in[bb:bb + 1, :] * outer
        num = w_inter[:, 0:1] * cq + s[:, 0:1] * v
        den = w_inter * jnp.sum(nrm * q, axis=1, keepdims=True) + s
        hh = num / jnp.maximum(jnp.abs(den), jnp.exp(-m_t))[:, 0:1]
        no_ref[:, ds] = w_inter * nrm + w_in * k
        mo_ref[:, ds] = m_t
        hn = hh * lax.rsqrt(jnp.mean(hh * hh, axis=1, keepdims=True) + EPS) * nmh_ref[:, vs]
        y_ref[:, vs] = jax.nn.sigmoid(gm_ref[:, vs]) * (jax.nn.sigmoid(o_ref[:, vs]) * hn)


def _mlstm_step(l, misc, km, pa, grep, m_all, n_all, c_all, c_out_prev, bi, bf, nmh):
    nbatch = c_all.shape[1]
    qd = NH_M * DQK_M
    rowq = lambda j: pl.BlockSpec((DEC_BLK, qd), lambda i: (i, j))
    rowv = lambda j: pl.BlockSpec((DEC_BLK, D_MODEL), lambda i: (i, j))
    lrow = pl.BlockSpec((None, DEC_BLK, qd), lambda i: (l, i, 0))
    lconst = lambda width: pl.BlockSpec((None, 1, width), lambda i: (l, 0, 0))
    cspec = pl.BlockSpec((None, DEC_BLK, NH_M, DV_M, DQK_M), lambda i: (l, i, 0, 0, 0))
    in_specs = [rowq(M_QM // qd), rowq(0), rowv(A_VM // D_MODEL), rowv(A_OM // D_MODEL), rowv(A_GM // D_MODEL),
                rowq(0), rowq(1), lrow, lrow, cspec,
                pl.BlockSpec((nbatch, qd), lambda i: (0, 0)),
                pl.BlockSpec((nbatch, D_MODEL), lambda i: (0, A_VM // D_MODEL)),
                lconst(qd), lconst(qd), lconst(D_MODEL)]
    args = [misc, km, pa, pa, pa, grep, grep, m_all, n_all, c_all, km, pa, bi, bf, nmh]
    aliases = {}
    if c_out_prev is not None:
        aliases = {len(args): 1}
        in_specs.append(pl.BlockSpec(memory_space=pl.ANY))
        args.append(c_out_prev)
    return pl.pallas_call(
        _mlstm_step_kernel,
        grid=(nbatch // DEC_BLK,),
        in_specs=in_specs,
        out_specs=[rowv(0), cspec, rowq(0), rowq(0)],
        out_shape=[jax.ShapeDtypeStruct((nbatch, D_MODEL), F32),
                   jax.ShapeDtypeStruct(c_all.shape, F32),
                   jax.ShapeDtypeStruct((nbatch, qd), F32),
                   jax.ShapeDtypeStruct((nbatch, qd), F32)],
        input_output_aliases=aliases,
        compiler_params=_cparams(1),
        name="mlstm_step",
    )(*args)


def _swa_step_kernel(q_ref, ga_ref, kn_ref, vn_ref, mk_ref, mv_ref, wk_ref, wv_ref, sink_ref, *rest):
    y_ref, ok_ref, ov_ref = rest[-3:]
    W = WINDOW
    hrow = lax.broadcasted_iota(jnp.int32, (N_HEADS, HD), 0)
    first = hrow < GQ
    wcol = lax.broadcasted_iota(jnp.int32, (N_HEADS, W), 1)
    wrow = lax.broadcasted_iota(jnp.int32, (N_KV * W, HD), 0)
    sink = sink_ref[...][:, 0:1]

    def per_kv(fn):
        a0, a1 = fn(0), fn(1)
        return jnp.where(first[:, 0:a0.shape[1]], a0, a1)

    def batch_element(bb):
        q = q_ref[bb]
        qb = q.astype(BF16)
        rows_of = lambda ref, kv, n: ref[bb, pl.ds(kv, n, stride=N_KV), :].astype(BF16)
        kn, vn = kn_ref[bb:bb + 1, :], vn_ref[bb:bb + 1, :]
        kvs = lambda a, kv: a[:, kv * HD:(kv + 1) * HD]
        s_w = jnp.where(wcol >= 1, per_kv(lambda kv: _dot_nt(qb, rows_of(wk_ref, kv, W))), -jnp.inf)
        s_m = per_kv(lambda kv: _dot_nt(qb, rows_of(mk_ref, kv, N_META)))
        kn8 = per_kv(lambda kv: jnp.broadcast_to(kvs(kn, kv), (N_HEADS, HD)))
        vn8 = per_kv(lambda kv: jnp.broadcast_to(kvs(vn, kv), (N_HEADS, HD)))
        s_n = jnp.sum(q * kn8, axis=1, keepdims=True)
        yield
        mx = jnp.maximum(jnp.maximum(jnp.max(s_w, axis=1, keepdims=True), jnp.max(s_m, axis=1, keepdims=True)),
                         jnp.maximum(s_n, sink))
        p_w, p_m, p_n = jnp.exp(s_w - mx), jnp.exp(s_m - mx), jnp.exp(s_n - mx)
        denom = (jnp.sum(p_w, axis=1, keepdims=True) + jnp.sum(p_m, axis=1, keepdims=True) + p_n
                 + jnp.exp(sink - mx))
        yield
        o = (per_kv(lambda kv: _dot(p_w.astype(BF16), rows_of(wv_ref, kv, W)))
             + per_kv(lambda kv: _dot(p_m.astype(BF16), rows_of(mv_ref, kv, N_META)))
             + p_n * vn8)
        yield
        y_ref[bb] = (o / denom) * jax.nn.sigmoid(ga_ref[bb])
        for src_ref, new, dst_ref in ((wk_ref, kn, ok_ref), (wv_ref, vn, ov_ref)):
            shifted = pltpu.roll(src_ref[bb], N_KV * (W - 1), 0)
            for kv in range(N_KV):
                shifted = jnp.where(wrow == N_KV * (W - 1) + kv, kvs(new, kv), shifted)
            dst_ref[bb] = shifted

    _run_interleaved([batch_element(bb) for bb in range(DEC_BLK)])


def _swa_step(l, q, ga, misc, mk_all, mv_all, wk_all, wv_all, ok_prev, ov_prev, sink8):
    nbatch = q.shape[0]
    kw = N_KV * HD
    hspec = pl.BlockSpec((DEC_BLK, N_HEADS, HD), lambda i: (i, 0, 0))
    mspec = pl.BlockSpec((None, DEC_BLK, N_KV * N_META, HD), lambda i: (l, i, 0, 0))
    wspec = pl.BlockSpec((None, DEC_BLK, N_KV * WINDOW, HD), lambda i: (l, i, 0, 0))
    in_specs = [hspec, hspec,
                pl.BlockSpec((DEC_BLK, kw), lambda i: (i, M_KA // kw)),
                pl.BlockSpec((DEC_BLK, kw), lambda i: (i, M_VA // kw)),
                mspec, mspec, wspec, wspec,
                pl.BlockSpec((None, N_HEADS, HD), lambda i: (l, 0, 0))]
    args = [q, ga, misc, misc, mk_all, mv_all, wk_all, wv_all, sink8]
    aliases = {}
    if ok_prev is not None:
        aliases = {len(args): 1, len(args) + 1: 2}
        in_specs += [pl.BlockSpec(memory_space=pl.ANY)] * 2
        args += [ok_prev, ov_prev]
    return pl.pallas_call(
        _swa_step_kernel,
        grid=(nbatch // DEC_BLK,),
        in_specs=in_specs,
        out_specs=[hspec, wspec, wspec],
        out_shape=[jax.ShapeDtypeStruct((nbatch, N_HEADS, HD), F32),
                   jax.ShapeDtypeStruct(wk_all.shape, F32),
                   jax.ShapeDtypeStruct(wv_all.shape, F32)],
        input_output_aliases=aliases,
        compiler_params=_cparams(1),
        name="swa_step",
    )(*args)


def _rope_tables(pos):
    half = ROT_DIM // 2
    inv = jnp.float32(ROPE_THETA) ** (-jnp.arange(half, dtype=F32) * 2.0 / ROT_DIM)
    ang = pos.astype(F32)[:, None] * inv[None, :]
    cos, sin = jnp.cos(ang), jnp.sin(ang)
    t = pos.shape[0]
    cs = jnp.concatenate([cos, cos, jnp.ones((t, HD - ROT_DIM), F32)], axis=1)
    s1 = jnp.concatenate([jnp.zeros((t, half), F32), sin, jnp.zeros((t, HD - ROT_DIM), F32)], axis=1)
    s2 = jnp.concatenate([-sin, jnp.zeros((t, HD - half), F32)], axis=1)
    return cs, s1, s2


PACK_ROWS = 128


def _pack_kernel(wt_ref, wa_ref, wb_ref):
    edges = np.concatenate([[0], np.cumsum(IN_WIDTHS)])
    q_m, k_m, v_m, ig, fg, o_m, q_a, k_a, v_a, g_m, g_a = [(int(edges[i]), int(edges[i + 1])) for i in range(11)]

    def put(dst_ref, dst0, seg):
        lo, hi = seg
        dst_ref[:, dst0:dst0 + hi - lo] = wt_ref[lo:hi, :].T.astype(BF16)

    for dst0, seg in ((A_VM, v_m), (A_OM, o_m), (A_GM, g_m), (A_GA, g_a)):
        put(wa_ref, dst0, seg)
    for dst0, seg in ((B_QA, q_a), (B_KA, k_a), (B_VA, v_a), (B_QM, q_m), (B_KM, k_m)):
        put(wb_ref, dst0, seg)
    tile = wt_ref[ig[0]:ig[0] + 128, :].T
    lane = lax.broadcasted_iota(jnp.int32, tile.shape, 1)
    wb_ref[:, B_G:B_TOT] = jnp.where(lane < 2 * NH_M, tile, 0.0).astype(BF16)


def _pack_in_weights(w_in):
    starts = np.concatenate([[0], np.cumsum(IN_WIDTHS)[:-1]])
    assert IN_WIDTHS[3] == IN_WIDTHS[4] == NH_M and all(s % 8 == 0 for i, s in enumerate(starts) if i != 4)
    depth, rows, cols = w_in.shape
    return pl.pallas_call(
        _pack_kernel,
        grid=(depth, rows // PACK_ROWS),
        in_specs=[pl.BlockSpec((None, cols, PACK_ROWS), lambda l, i: (l, 0, i))],
        out_specs=[pl.BlockSpec((None, PACK_ROWS, A_TOT), lambda l, i: (l, i, 0)),
                   pl.BlockSpec((None, PACK_ROWS, B_TOT), lambda l, i: (l, i, 0))],
        out_shape=[jax.ShapeDtypeStruct((depth, rows, A_TOT), BF16),
                   jax.ShapeDtypeStruct((depth, rows, B_TOT), BF16)],
        compiler_params=_cparams(2),
        name="pack_w_in",
    )(jnp.swapaxes(w_in, 1, 2))


def _rep(a, width=128):
    return jnp.repeat(a, width, axis=1)


def kernel(x_prompt, x_sample, state_C, state_n, state_m, cache_meta_k, cache_meta_v, cache_win_k, cache_win_v,
           meta_tokens, norm_mix, w_in, b_igate, b_fgate, norm_mh, attn_sinks, w_out, norm_mlp, w_up, w_down,
           norm_final):
    B, S, _ = x_prompt.shape
    DB = x_sample.shape[0]
    W = cache_win_k.shape[2]
    assert (S % CHUNK, W, x_sample.shape[1], DB % DEC_BLK) == (0, WINDOW, 1, 0)
    assert (state_C.dtype, cache_win_k.dtype, cache_win_v.dtype) == (F32, F32, F32)
    nchunk = S // CHUNK
    kw = N_KV * HD

    w_a, w_b = _pack_in_weights(w_in)
    tail_w = (w_out.astype(BF16), w_up.astype(BF16), w_down.astype(BF16))
    g_mix = norm_mix.astype(F32).reshape(DEPTH, 1, D_MODEL)
    g_mlp = norm_mlp.astype(F32).reshape(DEPTH, 1, D_MODEL)
    g_next = jnp.concatenate([g_mix, norm_final.astype(F32).reshape(1, 1, D_MODEL)], axis=0)
    nmh = norm_mh.astype(F32).reshape(DEPTH, 1, D_MODEL)
    sinks = attn_sinks.astype(F32)
    sink8 = jnp.broadcast_to(sinks[:, :, None], (DEPTH, N_HEADS, HD))
    bias8 = jnp.broadcast_to(jnp.concatenate([b_igate, b_fgate], axis=1).astype(F32)[:, :, None],
                             (DEPTH, 2 * NH_M, 128))
    bi_rep = _rep(b_igate.astype(F32)).reshape(DEPTH, 1, NH_M * 128)
    bf_rep = _rep(b_fgate.astype(F32)).reshape(DEPTH, 1, NH_M * 128)
    m_rep = jnp.repeat(state_m.astype(F32), 128, axis=2)
    n_all = state_n.astype(F32).reshape(DEPTH, DB, NH_M * DQK_M)
    rows_view = lambda a: a.reshape(DEPTH, DB, -1, HD)
    mk_all, mv_all = rows_view(cache_meta_k), rows_view(cache_meta_v)
    wk_all, wv_all = rows_view(cache_win_k), rows_view(cache_win_v)

    tabs_p = _rope_tables(N_META + jnp.arange(S, dtype=jnp.int32))
    pos_small = jnp.concatenate([jnp.full((DB,), PAST_LEN, jnp.int32), jnp.arange(N_META, dtype=jnp.int32),
                                 jnp.zeros((SMALL_ROWS - DB - N_META,), jnp.int32)])
    tabs_s = _rope_tables(pos_small)

    xp = x_prompt.reshape(B * S, D_MODEL)
    xs = jnp.concatenate([x_sample.reshape(DB, D_MODEL), meta_tokens.astype(F32),
                          jnp.zeros((SMALL_ROWS - DB - N_META, D_MODEL), F32)], axis=0)
    zero_state = (jnp.zeros((1, NH_M, DQK_M, DV_M), F32), jnp.zeros((1, NH_M, DQK_M, 128), F32),
                  jnp.zeros((1, NH_M, 8, 128), F32))

    outs = {k: [] for k in ("ct", "n", "m", "meta", "win", "sn", "sm")}
    sC = swk = swv = None
    hs_ = _norm(0, xs, g_mix, SMALL_ROWS, BF16)
    hp = _norm(0, xp, g_mix, NORM_TM, BF16)
    for l in range(DEPTH):
        pa_s, qa_s, misc_s, kt_s, gt_s, km_s, grep_s = _inproj(l, hs_, w_a, w_b, tabs_s, SMALL_ROWS, small=True)
        y_meta, ct_m, n_m, m_m = _mix(l, sinks, qa_s, misc_s, pa_s, kt_s, gt_s, bias8, nmh, *zero_state, misc_s,
                                      nb=1, nblk=1, sub=1, row0=META_ROW0, valid_len=N_META, has_meta=False)
        ys_m, sC, sn, sm = _mlstm_step(l, misc_s, km_s, pa_s, grep_s, m_rep, n_all, state_C, sC, bi_rep, bf_rep, nmh)
        ys_a, swk, swv = _swa_step(
            l, qa_s[:DB].reshape(DB, N_HEADS, HD), pa_s[:DB, A_GA:A_GA + D_MODEL].reshape(DB, N_HEADS, HD),
            misc_s, mk_all, mv_all, wk_all, wv_all, swk, swv, sink8)
        y_s = jnp.concatenate([(ys_m + ys_a.reshape(DB, D_MODEL)).astype(BF16), y_meta], axis=0)
        res = _tail(l, y_s, xs, g_mlp, g_next, tail_w, tm=SMALL_ROWS, tf=TAIL_TF)
        xs, hs_ = res[0], res[-1]

        pa, qa, misc, kt, gt = _inproj(l, hp, w_a, w_b, tabs_p, INPROJ_TM, small=False)
        y, ct_f, n_f, m_f = _mix(l, sinks, qa, misc, pa, kt, gt, bias8, nmh, ct_m, n_m, m_m, misc_s,
                                 nb=B, nblk=nchunk, sub=MIX_SUB, row0=0, valid_len=CHUNK, has_meta=True)
        res = _tail(l, y, xp, g_mlp, g_next, tail_w, tm=TAIL_TM, tf=TAIL_TF)
        xp, hp = res[0], res[-1]

        for key, val in (("ct", ct_f), ("n", n_f), ("m", m_f), ("sn", sn), ("sm", sm),
                         ("meta", misc_s[META_ROW0:META_ROW0 + N_META, M_KA:M_VA + kw]),
                         ("win", misc.reshape(B, S, M_TOT)[:, S - WINDOW:, M_KA:M_VA + kw])):
            outs[key].append(val)

    y_prompt = xp.reshape(B, S, D_MODEL)
    y_sample = xs[:DB].reshape(DB, 1, D_MODEL)
    st = lambda k: jnp.stack(outs[k])
    p_c = jnp.swapaxes(st("ct"), 3, 4).astype(state_C.dtype)
    p_n = st("n")[..., 0].astype(state_n.dtype)
    p_m = st("m")[..., 0, 0].astype(state_m.dtype)
    meta = jnp.broadcast_to(st("meta").astype(F32)[:, None], (DEPTH, B, N_META, 2 * kw))
    win = st("win").astype(F32)
    kv_of = lambda a, lo: a[..., lo:lo + kw].reshape(a.shape[:-1] + (N_KV, HD))
    s_n = st("sn").reshape(DEPTH, DB, NH_M, DQK_M).astype(state_n.dtype)
    s_m = st("sm").reshape(DEPTH, DB, NH_M, 128)[..., 0].astype(state_m.dtype)
    return (y_prompt, y_sample, p_c, p_n, p_m, kv_of(meta, 0), kv_of(meta, kw), kv_of(win, 0), kv_of(win, kw),
            sC, s_n, s_m, swk.reshape(cache_win_k.shape), swv.reshape(cache_win_v.shape))
```
